```python
import jax, jax.numpy as jnp
from jax import lax
import numpy as np


D_MODEL = 2048
BATCH = 2
SEQ = 4096
DEPTH = 1
DEC_BATCH = 1
DEC_SEQ = 8192
PAST_LEN = 128

CHUNK = 128
RET_HEADS = 8
RET_HEAD_DIM = 128
RET_WIDTH = RET_HEADS * RET_HEAD_DIM
SGU_GROUPS = 8
SGU_GROUP_DIM = 128
SGU_WIDTH = SGU_GROUPS * SGU_GROUP_DIM
MIX_WIDTH = RET_WIDTH + SGU_WIDTH
IN_COLS = 4 * RET_WIDTH + 2 * SGU_WIDTH
D_FF = 5632
CONV_WIDTH = 3
ROPE_BASE = 10000.0
EPS = 1e-6

kernel_name = 'hymba_gmlp_retnet_encoder'


def rms_norm(x, w):
    xf = x.astype(jnp.float32)
    y = xf * lax.rsqrt(jnp.mean(xf * xf, axis=-1, keepdims=True) + EPS)
    return (y * w.astype(jnp.float32)).astype(x.dtype)


def rotary(x):
    seq, dh = x.shape[1], x.shape[3]
    inv_freq = ROPE_BASE ** (-jnp.arange(0, dh, 2, dtype=jnp.float32) / dh)
    ang = jnp.arange(seq, dtype=jnp.float32)[:, None] * inv_freq[None, :]
    cos = jnp.cos(ang)[None, :, None, :].astype(x.dtype)
    sin = jnp.sin(ang)[None, :, None, :].astype(x.dtype)
    x1, x2 = jnp.split(x, 2, axis=-1)
    return jnp.concatenate([x1 * cos - x2 * sin, x2 * cos + x1 * sin], axis=-1)


def retention_direction(q, k, v, log_gamma, strict):
    b, h, s, dh = q.shape
    c = s // CHUNK
    qc = q.reshape(b, h, c, CHUNK, dh)
    kc = k.reshape(b, h, c, CHUNK, dh)
    vc = v.reshape(b, h, c, CHUNK, dh)
    lg = log_gamma.astype(jnp.float32)
    idx = jnp.arange(CHUNK, dtype=jnp.float32)
    diff = idx[:, None] - idx[None, :]
    mask = (diff > 0) if strict else (diff >= 0)
    dmat = jnp.where(mask[None], jnp.exp(lg[:, None, None] * jnp.maximum(diff, 0.0)[None]), 0.0).astype(q.dtype)
    zeta = jnp.exp(lg[:, None] * (CHUNK - 1 - idx)[None, :]).astype(q.dtype)
    xi = jnp.exp(lg[:, None] * (idx + 1.0)[None, :]).astype(q.dtype)
    chunk_decay = jnp.exp(lg * CHUNK).astype(q.dtype)
    scores = jnp.einsum('bhcnd,bhcmd->bhcnm', qc, kc) * dmat[None, :, None]
    inner = jnp.einsum('bhcnm,bhcme->bhcne', scores, vc)
    kv = jnp.einsum('bhcmd,bhcme->cbhde', kc * zeta[None, :, None, :, None], vc)

    def step(state, kv_c):
        return state * chunk_decay[None, :, None, None] + kv_c, state

    _, r_prev = lax.scan(step, jnp.zeros((b, h, dh, dh), kv.dtype), kv)
    cross = jnp.einsum('bhcnd,cbhde->bhcne', qc * xi[None, :, None, :, None], r_prev)
    return (inner + cross).reshape(b, h, s, dh)


def retention_mixer(q, k, v, g, log_decay_fwd, log_decay_bwd, gn_w):
    b, s, _ = q.shape
    shp = (b, s, RET_HEADS, RET_HEAD_DIM)
    qh = rotary(q.reshape(shp)) * (RET_HEAD_DIM ** -0.5)
    kh = rotary(k.reshape(shp))
    vh = v.reshape(shp)
    qh, kh, vh = (t.transpose(0, 2, 1, 3) for t in (qh, kh, vh))
    fwd = retention_direction(qh, kh, vh, log_decay_fwd, False)
    bwd = jnp.flip(retention_direction(jnp.flip(qh, 2), jnp.flip(kh, 2), jnp.flip(vh, 2), log_decay_bwd, True), 2)
    o = (fwd + bwd).transpose(0, 2, 1, 3).astype(jnp.float32)
    mu = jnp.mean(o, axis=-1, keepdims=True)
    var = jnp.mean(jnp.square(o - mu), axis=-1, keepdims=True)
    o = ((o - mu) * lax.rsqrt(var + EPS)).reshape(b, s, RET_WIDTH) * gn_w.astype(jnp.float32)
    return jax.nn.silu(g) * o.astype(q.dtype)


def spatial_gating_mixer(zu, zv, norm_w, w_s, bias):
    b, s, _ = zu.shape
    u = jax.nn.gelu(zu)
    v = rms_norm(jax.nn.gelu(zv), norm_w)
    vc = v.reshape(b, s // CHUNK, CHUNK, SGU_GROUPS, SGU_GROUP_DIM)
    sp = jnp.einsum('gnm,bcmgd->bcngd', w_s, vc) + bias.T[None, None, :, :, None]
    return u * sp.reshape(b, s, SGU_WIDTH)


def conv_ffn(h, w_up, conv_w, conv_b, w_down):
    up = h @ w_up
    pad = jnp.pad(up, ((0, 0), (1, 1), (0, 0)))
    up = pad[:, :-2] * conv_w[0] + pad[:, 1:-1] * conv_w[1] + pad[:, 2:] * conv_w[2] + conv_b
    a, bb = jnp.split(up, 2, axis=-1)
    return (jax.nn.silu(a) * bb) @ w_down


def encoder_layer(x, norm1_w, w_in, lg_fwd, lg_bwd, ret_gn_w, sgu_norm_w, sgu_w_s, sgu_b,
                  w_out, norm2_w, w_up, conv_w, conv_b, w_down):
    h = rms_norm(x, norm1_w)
    z = h @ w_in
    q, k, v, g, zu, zv = jnp.split(z, 6, axis=-1)
    ret = retention_mixer(q, k, v, g, lg_fwd, lg_bwd, ret_gn_w)
    sgu = spatial_gating_mixer(zu, zv, sgu_norm_w, sgu_w_s, sgu_b)
    x = x + jnp.concatenate([ret, sgu], axis=-1) @ w_out
    x = x + conv_ffn(rms_norm(x, norm2_w), w_up, conv_w, conv_b, w_down)
    return x


def encoder_trunk(x, layer_params, final_norm_w):
    for l in range(DEPTH):
        x = encoder_layer(x, *[p[l] for p in layer_params])
    return rms_norm(x, final_norm_w)


def setup_inputs(seed: int = 0) -> dict:
    key = jax.random.key(seed)
    ks = jax.random.split(key, 18)

    def nrm(k, shape, scale):
        return jax.random.normal(k, shape, jnp.float32) * scale

    base_ld = jnp.log(1.0 - 2.0 ** (-5.0 - jnp.arange(RET_HEADS, dtype=jnp.float32)))
    return {
        'x_prompt': nrm(ks[0], (BATCH, SEQ, D_MODEL), 1.0),
        'x_sample': nrm(ks[1], (DEC_BATCH, DEC_SEQ, D_MODEL), 1.0),
        'norm1_w': 1.0 + nrm(ks[2], (DEPTH, D_MODEL), 0.02),
        'w_in': nrm(ks[3], (DEPTH, D_MODEL, IN_COLS), D_MODEL ** -0.5),
        'ret_log_decay_fwd': base_ld[None] * (1.0 + nrm(ks[4], (DEPTH, RET_HEADS), 0.05)),
        'ret_log_decay_bwd': base_ld[None] * (1.0 + nrm(ks[5], (DEPTH, RET_HEADS), 0.05)),
        'ret_gn_w': 1.0 + nrm(ks[6], (DEPTH, RET_WIDTH), 0.02),
        'sgu_norm_w': 1.0 + nrm(ks[7], (DEPTH, SGU_WIDTH), 0.02),
        'sgu_w_s': nrm(ks[8], (DEPTH, SGU_GROUPS, CHUNK, CHUNK), CHUNK ** -0.5),
        'sgu_b': 1.0 + nrm(ks[9], (DEPTH, SGU_GROUPS, CHUNK), 0.01),
        'w_out': nrm(ks[10], (DEPTH, MIX_WIDTH, D_MODEL), MIX_WIDTH ** -0.5),
        'norm2_w': 1.0 + nrm(ks[11], (DEPTH, D_MODEL), 0.02),
        'w_up': nrm(ks[12], (DEPTH, D_MODEL, 2 * D_FF), D_MODEL ** -0.5),
        'conv_w': nrm(ks[13], (DEPTH, CONV_WIDTH, 2 * D_FF), CONV_WIDTH ** -0.5),
        'conv_b': nrm(ks[14], (DEPTH, 2 * D_FF), 0.01),
        'w_down': nrm(ks[15], (DEPTH, D_FF, D_MODEL), D_FF ** -0.5),
        'final_norm_w': 1.0 + nrm(ks[16], (D_MODEL,), 0.02),
    }


def reference(x_prompt, x_sample, norm1_w, w_in, ret_log_decay_fwd, ret_log_decay_bwd, ret_gn_w,
              sgu_norm_w, sgu_w_s, sgu_b, w_out, norm2_w, w_up, conv_w, conv_b, w_down, final_norm_w):
    layer_params = (norm1_w, w_in, ret_log_decay_fwd, ret_log_decay_bwd, ret_gn_w, sgu_norm_w,
                    sgu_w_s, sgu_b, w_out, norm2_w, w_up, conv_w, conv_b, w_down)
    y_prompt = encoder_trunk(x_prompt, layer_params, final_norm_w)
    y_sample = encoder_trunk(x_sample, layer_params, final_norm_w)
    return (y_prompt, y_sample)
```

```python
import functools
import math

import jax
import jax.numpy as jnp
from jax import lax
from jax.experimental import pallas as pl
from jax.experimental.pallas import tpu as pltpu

D_MODEL = 2048
CHUNK = 128
HEADS = 8
HEAD_DIM = 128
RET_WIDTH = HEADS * HEAD_DIM
GROUPS = 8
SGU_WIDTH = GROUPS * CHUNK
IN_COLS = 4 * RET_WIDTH + 2 * SGU_WIDTH
IN_SLABS = IN_COLS // 128
D_FF = 5632
ROPE_BASE = 10000.0
EPS = 1e-6
LANES = 128
BF16_ROWS = 16
VMEM_LIMIT = 56 * 1024 * 1024

F32 = jnp.float32
BF16 = jnp.bfloat16


def _rms(x, w):
    ms = jnp.mean(x * x, axis=-1, keepdims=True)
    return x * lax.rsqrt(ms + EPS) * w


def _gelu_tanh(x):
    c = math.sqrt(2.0 / math.pi)
    return x * (0.5 * (1.0 + jnp.tanh(c * (x + 0.044715 * (x * x * x)))))


def _silu(x):
    return x / (1.0 + jnp.exp(-x))


def _inproj_kernel(x_ref, nw_ref, w_ref, z_ref, h_ref):
    @pl.when(pl.program_id(1) == 0)
    def _():
        h_ref[...] = _rms(x_ref[...], nw_ref[...]).astype(BF16)

    res = jnp.dot(h_ref[...], w_ref[...], preferred_element_type=F32)
    for s in range(z_ref.shape[0]):
        z_ref[s] = res[:, s * LANES:(s + 1) * LANES].astype(z_ref.dtype)


def _in_proj(x, norm_w, w_bf16, *, tm=1024, tn=1024):
    t = x.shape[0]
    return pl.pallas_call(
        _inproj_kernel,
        grid=(t // tm, IN_COLS // tn),
        in_specs=[
            pl.BlockSpec((tm, D_MODEL), lambda i, j: (i, 0)),
            pl.BlockSpec((1, D_MODEL), lambda i, j: (0, 0)),
            pl.BlockSpec((D_MODEL, tn), lambda i, j: (0, j)),
        ],
        out_specs=pl.BlockSpec((tn // LANES, tm, LANES), lambda i, j: (j, i, 0)),
        out_shape=jax.ShapeDtypeStruct((IN_SLABS, t, LANES), BF16),
        scratch_shapes=[pltpu.VMEM((tm, D_MODEL), BF16)],
        compiler_params=pltpu.CompilerParams(
            dimension_semantics=("parallel", "arbitrary"),
            vmem_limit_bytes=VMEM_LIMIT),
        name="in_proj",
    )(x, norm_w.reshape(1, D_MODEL), w_bf16)


def _ret_kernel(q_ref, k_ref, v_ref, g_ref, cos_ref, sin_ref, lgf_ref, lgb_ref, gnw_ref,
                o_ref, qs_ref, qxf_ref, qxb_ref, kt_ref, ktf_ref, ktb_ref, rb_ref):
    seq = q_ref.shape[1]
    n_chunks = seq // CHUNK
    lgf = lgf_ref[0, 0:1, :]
    lgb = lgb_ref[0, 0:1, :]
    row = lax.broadcasted_iota(jnp.int32, (CHUNK, CHUNK), 0).astype(F32)
    col = lax.broadcasted_iota(jnp.int32, (CHUNK, CHUNK), 1).astype(F32)
    diff = row - col
    dcomb = jnp.where(diff >= 0, jnp.exp(lgf * diff), jnp.exp(lgb * (-diff)))
    xi_f = jnp.exp(lgf * (row + 1.0))
    xi_b = jnp.exp(lgb * (CHUNK - row))
    zeta_f = jnp.exp(lgf * (CHUNK - 1.0 - col))
    zeta_b = jnp.exp(lgb * col)
    decay_f = jnp.exp(lgf * CHUNK)
    decay_b = jnp.exp(lgb * CHUNK)
    scale = HEAD_DIM ** -0.5

    def rows(c):
        return pl.ds(pl.multiple_of(c * CHUNK, CHUNK), CHUNK)

    def prep(c, carry):
        sl = rows(c)
        q = q_ref[0, sl, :].astype(F32)
        k = k_ref[0, sl, :].astype(F32)
        cos = cos_ref[sl, :]
        sin = sin_ref[sl, :]
        qr = (q * cos + pltpu.roll(q, HEAD_DIM // 2, 1) * sin) * scale
        kr = k * cos + pltpu.roll(k, HEAD_DIM // 2, 1) * sin
        kt = kr.T
        qs_ref[sl, :] = qr.astype(BF16)
        qxf_ref[sl, :] = (qr * xi_f).astype(BF16)
        qxb_ref[sl, :] = (qr * xi_b).astype(BF16)
        kt_ref[c] = kt.astype(BF16)
        ktf_ref[c] = (kt * zeta_f).astype(BF16)
        ktb_ref[c] = (kt * zeta_b).astype(BF16)
        return carry

    lax.fori_loop(0, n_chunks, prep, 0)

    def bwd(i, state):
        c = n_chunks - 1 - i
        rb_ref[c] = state.astype(BF16)
        kv = jnp.dot(ktb_ref[c], v_ref[0, rows(c), :], preferred_element_type=F32)
        return state * decay_b + kv

    lax.fori_loop(0, n_chunks, bwd, jnp.zeros((HEAD_DIM, HEAD_DIM), F32))

    gnw = gnw_ref[0, 0:1, :]

    def fwd(c, state):
        sl = rows(c)
        v = v_ref[0, sl, :]
        s = jnp.dot(qs_ref[sl, :], kt_ref[c], preferred_element_type=F32) * dcomb
        o = jnp.dot(s.astype(BF16), v, preferred_element_type=F32)
        o = o + jnp.dot(qxf_ref[sl, :], state.astype(BF16), preferred_element_type=F32)
        o = o + jnp.dot(qxb_ref[sl, :], rb_ref[c], preferred_element_type=F32)
        mu = jnp.mean(o, axis=-1, keepdims=True)
        d = o - mu
        var = jnp.mean(d * d, axis=-1, keepdims=True)
        on = d * lax.rsqrt(var + EPS) * gnw
        g = g_ref[0, sl, :].astype(F32)
        o_ref[0, sl, :] = (_silu(g) * on).astype(o_ref.dtype)
        kv = jnp.dot(ktf_ref[c], v, preferred_element_type=F32)
        return state * decay_f + kv

    lax.fori_loop(0, n_chunks, fwd, jnp.zeros((HEAD_DIM, HEAD_DIM), F32))


def _retention(z, cos, sin, lgf, lgb, gnw, *, batch, seq):
    t = batch * seq
    n_chunks = seq // CHUNK

    def zspec(base):
        return pl.BlockSpec((1, seq, LANES), lambda b, h: (base + h, b, 0))

    table = pl.BlockSpec((seq, LANES), lambda b, h: (0, 0), pipeline_mode=pl.Buffered(1))
    per_head = pl.BlockSpec((1, 8, LANES), lambda b, h: (h, 0, 0))
    return pl.pallas_call(
        _ret_kernel,
        grid=(batch, HEADS),
        in_specs=[zspec(0), zspec(HEADS), zspec(2 * HEADS), zspec(3 * HEADS),
                  table, table, per_head, per_head, per_head],
        out_specs=pl.BlockSpec((1, seq, LANES), lambda b, h: (h, b, 0)),
        out_shape=jax.ShapeDtypeStruct((HEADS, t, LANES), BF16),
        scratch_shapes=[pltpu.VMEM((seq, LANES), BF16)] * 3
        + [pltpu.VMEM((n_chunks, HEAD_DIM, CHUNK), BF16)] * 3
        + [pltpu.VMEM((n_chunks, HEAD_DIM, HEAD_DIM), BF16)],
        compiler_params=pltpu.CompilerParams(
            dimension_semantics=("parallel", "parallel"),
            vmem_limit_bytes=VMEM_LIMIT),
        name="retention",
    )(z, z, z, z, cos, sin, lgf, lgb, gnw)


def _sgu_kernel(zu_ref, zv_ref, nw_ref, ws_ref, b_ref, o_ref):
    rows = zu_ref.shape[1]
    n_chunks = rows // CHUNK
    v = _gelu_tanh(zv_ref[...].astype(F32))
    ms = jnp.sum(jnp.sum(v * v, axis=0), axis=-1, keepdims=True) * (1.0 / SGU_WIDTH)
    inv = lax.rsqrt(ms + EPS)
    for g in range(GROUPS):
        vn = (v[g] * inv * nw_ref[g, 0:1, :]).astype(BF16)
        rhs = jnp.concatenate(
            [vn[c * CHUNK:(c + 1) * CHUNK] for c in range(n_chunks)], axis=1)
        sp = jnp.dot(ws_ref[g], rhs, preferred_element_type=F32)
        u = _gelu_tanh(zu_ref[g].astype(F32))
        bias = b_ref[g]
        for c in range(n_chunks):
            sl = slice(c * CHUNK, (c + 1) * CHUNK)
            o_ref[g, sl, :] = (u[sl] * (sp[:, sl] + bias)).astype(o_ref.dtype)


def _sgu(z, norm_w, ws_bf16, bias, *, rows=512):
    t = z.shape[1]
    u_base = 4 * HEADS // GROUPS
    return pl.pallas_call(
        _sgu_kernel,
        grid=(t // rows,),
        in_specs=[
            pl.BlockSpec((GROUPS, rows, LANES), lambda i: (u_base, i, 0)),
            pl.BlockSpec((GROUPS, rows, LANES), lambda i: (u_base + 1, i, 0)),
            pl.BlockSpec((GROUPS, 8, LANES), lambda i: (0, 0, 0)),
            pl.BlockSpec((GROUPS, CHUNK, CHUNK), lambda i: (0, 0, 0)),
            pl.BlockSpec((GROUPS, CHUNK, LANES), lambda i: (0, 0, 0)),
        ],
        out_specs=pl.BlockSpec((GROUPS, rows, LANES), lambda i: (0, i, 0)),
        out_shape=jax.ShapeDtypeStruct((GROUPS, t, LANES), BF16),
        compiler_params=pltpu.CompilerParams(
            dimension_semantics=("parallel",),
            vmem_limit_bytes=VMEM_LIMIT),
        name="sgu",
    )(z, z, norm_w, ws_bf16, bias)


def _outproj_kernel(ret_ref, sgu_ref, x_ref, w_ref, nw_ref, x2_ref, h2_ref):
    mix = jnp.concatenate([ret_ref[s] for s in range(HEADS)]
                          + [sgu_ref[s] for s in range(GROUPS)], axis=1)
    x2 = x_ref[...] + jnp.dot(mix, w_ref[...], preferred_element_type=F32)
    x2_ref[...] = x2
    h2_ref[...] = _rms(x2, nw_ref[...]).astype(BF16)


def _out_proj(ret, sgu, x, w_bf16, norm_w, *, tm=512):
    t = x.shape[0]
    return pl.pallas_call(
        _outproj_kernel,
        grid=(t // tm,),
        in_specs=[
            pl.BlockSpec((HEADS, tm, LANES), lambda i: (0, i, 0)),
            pl.BlockSpec((GROUPS, tm, LANES), lambda i: (0, i, 0)),
            pl.BlockSpec((tm, D_MODEL), lambda i: (i, 0)),
            pl.BlockSpec((D_MODEL, D_MODEL), lambda i: (0, 0)),
            pl.BlockSpec((1, D_MODEL), lambda i: (0, 0)),
        ],
        out_specs=[pl.BlockSpec((tm, D_MODEL), lambda i: (i, 0)),
                   pl.BlockSpec((tm, D_MODEL), lambda i: (i, 0))],
        out_shape=[jax.ShapeDtypeStruct((t, D_MODEL), F32),
                   jax.ShapeDtypeStruct((t, D_MODEL), BF16)],
        compiler_params=pltpu.CompilerParams(
            dimension_semantics=("parallel",),
            vmem_limit_bytes=VMEM_LIMIT),
        name="out_proj",
    )(ret, sgu, x, w_bf16, norm_w.reshape(1, D_MODEL))


def _ffn_kernel(hp_ref, h_ref, hn_ref, x2_ref, wa_ref, wb_ref, cwa_ref, cwb_ref,
                cba_ref, cbb_ref, wd_ref, fw_ref, o_ref, lhs_ref, *, tiles_per_seq):
    i = pl.program_id(0)
    j = pl.program_id(1)
    tm = h_ref.shape[0]
    halo = hp_ref.shape[0]
    n_rows = tm + 2 * halo

    @pl.when(j == 0)
    def _():
        pos = i % tiles_per_seq
        hp = hp_ref[...]
        hn = hn_ref[...]
        lhs_ref[0:halo, :] = jnp.where(pos == 0, jnp.zeros_like(hp), hp)
        lhs_ref[halo:halo + tm, :] = h_ref[...]
        lhs_ref[halo + tm:, :] = jnp.where(pos == tiles_per_seq - 1, jnp.zeros_like(hn), hn)

    lhs = lhs_ref[...]

    def conv_branch(w_ref, cw_ref, cb_ref):
        u = jnp.dot(lhs, w_ref[...], preferred_element_type=F32)
        prev = pltpu.roll(u, 1, 0)[halo:halo + tm]
        nxt = pltpu.roll(u, n_rows - 1, 0)[halo:halo + tm]
        cur = u[halo:halo + tm]
        return prev * cw_ref[0:1, :] + cur * cw_ref[1:2, :] + nxt * cw_ref[2:3, :] + cb_ref[...]

    a = conv_branch(wa_ref, cwa_ref, cba_ref)
    b = conv_branch(wb_ref, cwb_ref, cbb_ref)
    act = (_silu(a) * b).astype(BF16)
    contrib = jnp.dot(act, wd_ref[...], preferred_element_type=F32)

    @pl.when(j == 0)
    def _():
        o_ref[...] = contrib

    @pl.when(j > 0)
    def _():
        o_ref[...] += contrib

    @pl.when(j == pl.num_programs(1) - 1)
    def _():
        o_ref[...] = _rms(x2_ref[...] + o_ref[...], fw_ref[...])


def _ffn(h2, x2, w_up_bf16, conv_w, conv_b, w_down_bf16, final_w, *, seq, tm=512, tf=512):
    t = h2.shape[0]
    halo = BF16_ROWS
    n_j = D_FF // tf
    halo_blocks = tm // halo
    last_halo_block = t // halo - 1
    kernel = functools.partial(_ffn_kernel, tiles_per_seq=seq // tm)
    return pl.pallas_call(
        kernel,
        grid=(t // tm, n_j),
        in_specs=[
            pl.BlockSpec((halo, D_MODEL), lambda i, j: (jnp.maximum(i * halo_blocks - 1, 0), 0)),
            pl.BlockSpec((tm, D_MODEL), lambda i, j: (i, 0)),
            pl.BlockSpec((halo, D_MODEL),
                         lambda i, j: (jnp.minimum((i + 1) * halo_blocks, last_halo_block), 0)),
            pl.BlockSpec((tm, D_MODEL), lambda i, j: (i, 0)),
            pl.BlockSpec((D_MODEL, tf), lambda i, j: (0, j)),
            pl.BlockSpec((D_MODEL, tf), lambda i, j: (0, n_j + j)),
            pl.BlockSpec((3, tf), lambda i, j: (0, j)),
            pl.BlockSpec((3, tf), lambda i, j: (0, n_j + j)),
            pl.BlockSpec((1, tf), lambda i, j: (0, j)),
            pl.BlockSpec((1, tf), lambda i, j: (0, n_j + j)),
            pl.BlockSpec((tf, D_MODEL), lambda i, j: (j, 0)),
            pl.BlockSpec((1, D_MODEL), lambda i, j: (0, 0)),
        ],
        out_specs=pl.BlockSpec((tm, D_MODEL), lambda i, j: (i, 0)),
        out_shape=jax.ShapeDtypeStruct((t, D_MODEL), F32),
        scratch_shapes=[pltpu.VMEM((tm + 2 * halo, D_MODEL), BF16)],
        compiler_params=pltpu.CompilerParams(
            dimension_semantics=("parallel", "arbitrary"),
            vmem_limit_bytes=VMEM_LIMIT),
        name="ffn",
    )(h2, h2, h2, x2, w_up_bf16, w_up_bf16, conv_w, conv_w,
      conv_b.reshape(1, 2 * D_FF), conv_b.reshape(1, 2 * D_FF), w_down_bf16,
      final_w.reshape(1, D_MODEL))


def _rope_tables(seq):
    inv_freq = ROPE_BASE ** (-jnp.arange(0, HEAD_DIM, 2, dtype=F32) / HEAD_DIM)
    ang = jnp.arange(seq, dtype=F32)[:, None] * inv_freq[None, :]
    cos = jnp.cos(ang)
    sin = jnp.sin(ang)
    return jnp.concatenate([cos, cos], axis=-1), jnp.concatenate([-sin, sin], axis=-1)


def _per_slab(vec, slabs):
    return jnp.broadcast_to(vec.reshape(slabs, 1, LANES), (slabs, 8, LANES))


def _trunk(x, p):
    batch, seq, _ = x.shape
    xt = x.reshape(batch * seq, D_MODEL)
    z = _in_proj(xt, p["norm1_w"], p["w_in"])
    ret = _retention(z, p["cos"], p["sin"], p["lgf"], p["lgb"], p["gnw"], batch=batch, seq=seq)
    sgu = _sgu(z, p["sgu_nw"], p["w_s"], p["sgu_b"])
    x2, h2 = _out_proj(ret, sgu, xt, p["w_out"], p["norm2_w"])
    y = _ffn(h2, x2, p["w_up"], p["conv_w"], p["conv_b"], p["w_down"], p["final_w"], seq=seq)
    return y.reshape(batch, seq, D_MODEL)


def kernel(x_prompt, x_sample, norm1_w, w_in, ret_log_decay_fwd, ret_log_decay_bwd, ret_gn_w, sgu_norm_w, sgu_w_s, sgu_b, w_out, norm2_w, w_up, conv_w, conv_b, w_down, final_norm_w):
    max_seq = max(x_prompt.shape[1], x_sample.shape[1])
    cos, sin = _rope_tables(max_seq)
    p = {
        "norm1_w": norm1_w[0],
        "w_in": w_in[0].astype(BF16),
        "cos": cos,
        "sin": sin,
        "lgf": jnp.broadcast_to(ret_log_decay_fwd[0].astype(F32)[:, None, None], (HEADS, 8, LANES)),
        "lgb": jnp.broadcast_to(ret_log_decay_bwd[0].astype(F32)[:, None, None], (HEADS, 8, LANES)),
        "gnw": _per_slab(ret_gn_w[0], HEADS),
        "sgu_nw": _per_slab(sgu_norm_w[0], GROUPS),
        "w_s": sgu_w_s[0].astype(BF16),
        "sgu_b": jnp.broadcast_to(sgu_b[0][:, :, None], (GROUPS, CHUNK, LANES)),
        "w_out": w_out[0].astype(BF16),
        "norm2_w": norm2_w[0],
        "w_up": w_up[0].astype(BF16),
        "conv_w": conv_w[0],
        "conv_b": conv_b[0],
        "w_down": w_down[0].astype(BF16),
        "final_w": final_norm_w,
    }
    return (_trunk(x_prompt, p), _trunk(x_sample, p))
```

```python
import functools
import math

import jax
import jax.numpy as jnp
from jax import lax
from jax.experimental import pallas as pl
from jax.experimental.pallas import tpu as pltpu

D_MODEL = 2048
CHUNK = 128
HEADS = 8
HEAD_DIM = 128
RET_WIDTH = HEADS * HEAD_DIM
GROUPS = 8
SGU_WIDTH = GROUPS * CHUNK
IN_COLS = 4 * RET_WIDTH + 2 * SGU_WIDTH
IN_SLABS = IN_COLS // 128
D_FF = 5632
ROPE_BASE = 10000.0
EPS = 1e-6
LANES = 128
BF16_ROWS = 16
VMEM_LIMIT = 56 * 1024 * 1024

F32 = jnp.float32
BF16 = jnp.bfloat16


def _rms(x, w):
    ms = jnp.mean(x * x, axis=-1, keepdims=True)
    return x * lax.rsqrt(ms + EPS) * w


def _gelu_tanh(x):
    c = math.sqrt(2.0 / math.pi)
    return x * (0.5 * (1.0 + jnp.tanh(c * (x + 0.044715 * (x * x * x)))))


def _silu(x):
    return x / (1.0 + jnp.exp(-x))


def _inproj_kernel(x_ref, nw_ref, w_ref, z_ref, h_ref):
    @pl.when(pl.program_id(1) == 0)
    def _():
        h_ref[...] = _rms(x_ref[...], nw_ref[...]).astype(BF16)

    res = jnp.dot(h_ref[...], w_ref[...], preferred_element_type=F32)
    for s in range(z_ref.shape[0]):
        z_ref[s] = res[:, s * LANES:(s + 1) * LANES].astype(z_ref.dtype)


def _in_proj(x, norm_w, w_bf16, *, tm=1024, tn=1024):
    t = x.shape[0]
    return pl.pallas_call(
        _inproj_kernel,
        grid=(t // tm, IN_COLS // tn),
        in_specs=[
            pl.BlockSpec((tm, D_MODEL), lambda i, j: (i, 0)),
            pl.BlockSpec((1, D_MODEL), lambda i, j: (0, 0)),
            pl.BlockSpec((D_MODEL, tn), lambda i, j: (0, j)),
        ],
        out_specs=pl.BlockSpec((tn // LANES, tm, LANES), lambda i, j: (j, i, 0)),
        out_shape=jax.ShapeDtypeStruct((IN_SLABS, t, LANES), BF16),
        scratch_shapes=[pltpu.VMEM((tm, D_MODEL), BF16)],
        compiler_params=pltpu.CompilerParams(
            dimension_semantics=("parallel", "arbitrary"),
            vmem_limit_bytes=VMEM_LIMIT),
        name="in_proj",
    )(x, norm_w.reshape(1, D_MODEL), w_bf16)


def _ret_kernel(q_ref, k_ref, v_ref, g_ref, cos_ref, sin_ref, lgf_ref, lgb_ref, gnw_ref,
                o_ref, lhs_ref, kt_ref, kvf_ref, rb_ref):
    seq = q_ref.shape[1]
    n_chunks = seq // CHUNK
    lgf = lgf_ref[0, 0:1, :]
    lgb = lgb_ref[0, 0:1, :]
    row = lax.broadcasted_iota(jnp.int32, (CHUNK, CHUNK), 0).astype(F32)
    col = lax.broadcasted_iota(jnp.int32, (CHUNK, CHUNK), 1).astype(F32)
    diff = row - col
    dcomb = jnp.where(diff >= 0, jnp.exp(lgf * diff), jnp.exp(lgb * (-diff)))
    xi_f = jnp.exp(lgf * (row + 1.0))
    xi_b = jnp.exp(lgb * (CHUNK - row))
    zeta_f = jnp.exp(lgf * (CHUNK - 1.0 - col))
    zeta_b = jnp.exp(lgb * col)
    decay_f = jnp.exp(lgf * CHUNK)
    decay_b = jnp.exp(lgb * CHUNK)
    scale = HEAD_DIM ** -0.5

    def rows(c):
        return pl.ds(pl.multiple_of(c * CHUNK, CHUNK), CHUNK)

    def prep(i, state):
        c = n_chunks - 1 - i
        sl = rows(c)
        q = q_ref[0, sl, :].astype(F32)
        k = k_ref[0, sl, :].astype(F32)
        cos = cos_ref[sl, :]
        sin = sin_ref[sl, :]
        qr = (q * cos + pltpu.roll(q, HEAD_DIM // 2, 1) * sin) * scale
        kr = k * cos + pltpu.roll(k, HEAD_DIM // 2, 1) * sin
        kt = kr.T
        lhs_ref[sl, :] = jnp.concatenate([qr, qr * xi_f, qr * xi_b], axis=1).astype(BF16)
        kt_ref[c] = kt.astype(BF16)
        ktz = jnp.concatenate([kt * zeta_f, kt * zeta_b], axis=0).astype(BF16)
        kv = jnp.dot(ktz, v_ref[0, sl, :], preferred_element_type=F32)
        kvf_ref[c] = kv[:HEAD_DIM]
        rb_ref[c] = state.astype(BF16)
        return state * decay_b + kv[HEAD_DIM:]

    lax.fori_loop(0, n_chunks, prep, jnp.zeros((HEAD_DIM, HEAD_DIM), F32), unroll=4)

    gnw = gnw_ref[0, 0:1, :]

    def fwd(c, state):
        sl = rows(c)
        s = jnp.dot(lhs_ref[sl, 0:HEAD_DIM], kt_ref[c], preferred_element_type=F32) * dcomb
        lhs = jnp.concatenate([s.astype(BF16), lhs_ref[sl, HEAD_DIM:]], axis=1)
        rhs = jnp.concatenate([v_ref[0, sl, :], state.astype(BF16), rb_ref[c]], axis=0)
        o = jnp.dot(lhs, rhs, preferred_element_type=F32)
        mu = jnp.mean(o, axis=-1, keepdims=True)
        d = o - mu
        var = jnp.mean(d * d, axis=-1, keepdims=True)
        on = d * lax.rsqrt(var + EPS) * gnw
        g = g_ref[0, sl, :].astype(F32)
        o_ref[0, sl, :] = (_silu(g) * on).astype(o_ref.dtype)
        return state * decay_f + kvf_ref[c]

    lax.fori_loop(0, n_chunks, fwd, jnp.zeros((HEAD_DIM, HEAD_DIM), F32), unroll=8)


def _retention(z, cos, sin, lgf, lgb, gnw, *, batch, seq):
    t = batch * seq
    n_chunks = seq // CHUNK

    def zspec(base):
        return pl.BlockSpec((1, seq, LANES), lambda b, h: (base + h, b, 0))

    table = pl.BlockSpec((seq, LANES), lambda b, h: (0, 0), pipeline_mode=pl.Buffered(1))
    per_head = pl.BlockSpec((1, 8, LANES), lambda b, h: (h, 0, 0))
    return pl.pallas_call(
        _ret_kernel,
        grid=(batch, HEADS),
        in_specs=[zspec(0), zspec(HEADS), zspec(2 * HEADS), zspec(3 * HEADS),
                  table, table, per_head, per_head, per_head],
        out_specs=pl.BlockSpec((1, seq, LANES), lambda b, h: (h, b, 0)),
        out_shape=jax.ShapeDtypeStruct((HEADS, t, LANES), BF16),
        scratch_shapes=[
            pltpu.VMEM((seq, 3 * HEAD_DIM), BF16),
            pltpu.VMEM((n_chunks, HEAD_DIM, CHUNK), BF16),
            pltpu.VMEM((n_chunks, HEAD_DIM, HEAD_DIM), F32),
            pltpu.VMEM((n_chunks, HEAD_DIM, HEAD_DIM), BF16),
        ],
        compiler_params=pltpu.CompilerParams(
            dimension_semantics=("parallel", "parallel"),
            vmem_limit_bytes=VMEM_LIMIT),
        name="retention",
    )(z, z, z, z, cos, sin, lgf, lgb, gnw)


def _sgu_kernel(zu_ref, zv_ref, nw_ref, ws_ref, b_ref, o_ref):
    rows = zu_ref.shape[1]
    n_chunks = rows // CHUNK
    v = _gelu_tanh(zv_ref[...].astype(F32))
    ms = jnp.sum(jnp.sum(v * v, axis=0), axis=-1, keepdims=True) * (1.0 / SGU_WIDTH)
    inv = lax.rsqrt(ms + EPS)
    for g in range(GROUPS):
        vn = (v[g] * inv * nw_ref[g, 0:1, :]).astype(BF16)
        rhs = jnp.concatenate(
            [vn[c * CHUNK:(c + 1) * CHUNK] for c in range(n_chunks)], axis=1)
        sp = jnp.dot(ws_ref[g], rhs, preferred_element_type=F32)
        u = _gelu_tanh(zu_ref[g].astype(F32))
        bias = b_ref[g]
        for c in range(n_chunks):
            sl = slice(c * CHUNK, (c + 1) * CHUNK)
            o_ref[g, sl, :] = (u[sl] * (sp[:, sl] + bias)).astype(o_ref.dtype)


def _sgu(z, norm_w, ws_bf16, bias, *, rows=512):
    t = z.shape[1]
    u_base = 4 * HEADS // GROUPS
    return pl.pallas_call(
        _sgu_kernel,
        grid=(t // rows,),
        in_specs=[
            pl.BlockSpec((GROUPS, rows, LANES), lambda i: (u_base, i, 0)),
            pl.BlockSpec((GROUPS, rows, LANES), lambda i: (u_base + 1, i, 0)),
            pl.BlockSpec((GROUPS, 8, LANES), lambda i: (0, 0, 0)),
            pl.BlockSpec((GROUPS, CHUNK, CHUNK), lambda i: (0, 0, 0)),
            pl.BlockSpec((GROUPS, CHUNK, LANES), lambda i: (0, 0, 0)),
        ],
        out_specs=pl.BlockSpec((GROUPS, rows, LANES), lambda i: (0, i, 0)),
        out_shape=jax.ShapeDtypeStruct((GROUPS, t, LANES), BF16),
        compiler_params=pltpu.CompilerParams(
            dimension_semantics=("parallel",),
            vmem_limit_bytes=VMEM_LIMIT),
        name="sgu",
    )(z, z, norm_w, ws_bf16, bias)


def _outproj_kernel(ret_ref, sgu_ref, x_ref, w_ref, nw_ref, x2_ref, h2_ref):
    mix = jnp.concatenate([ret_ref[s] for s in range(HEADS)]
                          + [sgu_ref[s] for s in range(GROUPS)], axis=1)
    x2 = x_ref[...] + jnp.dot(mix, w_ref[...], preferred_element_type=F32)
    x2_ref[...] = x2
    h2_ref[...] = _rms(x2, nw_ref[...]).astype(BF16)


def _out_proj(ret, sgu, x, w_bf16, norm_w, *, tm=512):
    t = x.shape[0]
    return pl.pallas_call(
        _outproj_kernel,
        grid=(t // tm,),
        in_specs=[
            pl.BlockSpec((HEADS, tm, LANES), lambda i: (0, i, 0)),
            pl.BlockSpec((GROUPS, tm, LANES), lambda i: (0, i, 0)),
            pl.BlockSpec((tm, D_MODEL), lambda i: (i, 0)),
            pl.BlockSpec((D_MODEL, D_MODEL), lambda i: (0, 0)),
            pl.BlockSpec((1, D_MODEL), lambda i: (0, 0)),
        ],
        out_specs=[pl.BlockSpec((tm, D_MODEL), lambda i: (i, 0)),
                   pl.BlockSpec((tm, D_MODEL), lambda i: (i, 0))],
        out_shape=[jax.ShapeDtypeStruct((t, D_MODEL), F32),
                   jax.ShapeDtypeStruct((t, D_MODEL), BF16)],
        compiler_params=pltpu.CompilerParams(
            dimension_semantics=("parallel",),
            vmem_limit_bytes=VMEM_LIMIT),
        name="out_proj",
    )(ret, sgu, x, w_bf16, norm_w.reshape(1, D_MODEL))


def _ffn_kernel(hp_ref, h_ref, hn_ref, x2_ref, wa_ref, wb_ref, cwa_ref, cwb_ref,
                cba_ref, cbb_ref, wd_ref, fw_ref, o_ref, lhs_ref, *, tiles_per_seq):
    i = pl.program_id(0)
    j = pl.program_id(1)
    tm = h_ref.shape[0]
    halo = hp_ref.shape[0]
    n_rows = tm + 2 * halo

    @pl.when(j == 0)
    def _():
        pos = i % tiles_per_seq
        hp = hp_ref[...]
        hn = hn_ref[...]
        lhs_ref[0:halo, :] = jnp.where(pos == 0, jnp.zeros_like(hp), hp)
        lhs_ref[halo:halo + tm, :] = h_ref[...]
        lhs_ref[halo + tm:, :] = jnp.where(pos == tiles_per_seq - 1, jnp.zeros_like(hn), hn)

    lhs = lhs_ref[...]

    def conv_branch(w_ref, cw_ref, cb_ref):
        u = jnp.dot(lhs, w_ref[...], preferred_element_type=F32)
        prev = pltpu.roll(u, 1, 0)[halo:halo + tm]
        nxt = pltpu.roll(u, n_rows - 1, 0)[halo:halo + tm]
        cur = u[halo:halo + tm]
        return prev * cw_ref[0:1, :] + cur * cw_ref[1:2, :] + nxt * cw_ref[2:3, :] + cb_ref[...]

    a = conv_branch(wa_ref, cwa_ref, cba_ref)
    b = conv_branch(wb_ref, cwb_ref, cbb_ref)
    act = (_silu(a) * b).astype(BF16)
    contrib = jnp.dot(act, wd_ref[...], preferred_element_type=F32)

    @pl.when(j == 0)
    def _():
        o_ref[...] = contrib

    @pl.when(j > 0)
    def _():
        o_ref[...] += contrib

    @pl.when(j == pl.num_programs(1) - 1)
    def _():
        o_ref[...] = _rms(x2_ref[...] + o_ref[...], fw_ref[...])


def _ffn(h2, x2, w_up_bf16, conv_w, conv_b, w_down_bf16, final_w, *, seq, tm=512, tf=512):
    t = h2.shape[0]
    halo = BF16_ROWS
    n_j = D_FF // tf
    halo_blocks = tm // halo
    last_halo_block = t // halo - 1
    kernel = functools.partial(_ffn_kernel, tiles_per_seq=seq // tm)
    return pl.pallas_call(
        kernel,
        grid=(t // tm, n_j),
        in_specs=[
            pl.BlockSpec((halo, D_MODEL), lambda i, j: (jnp.maximum(i * halo_blocks - 1, 0), 0)),
            pl.BlockSpec((tm, D_MODEL), lambda i, j: (i, 0)),
            pl.BlockSpec((halo, D_MODEL),
                         lambda i, j: (jnp.minimum((i + 1) * halo_blocks, last_halo_block), 0)),
            pl.BlockSpec((tm, D_MODEL), lambda i, j: (i, 0)),
            pl.BlockSpec((D_MODEL, tf), lambda i, j: (0, j)),
            pl.BlockSpec((D_MODEL, tf), lambda i, j: (0, n_j + j)),
            pl.BlockSpec((3, tf), lambda i, j: (0, j)),
            pl.BlockSpec((3, tf), lambda i, j: (0, n_j + j)),
            pl.BlockSpec((1, tf), lambda i, j: (0, j)),
            pl.BlockSpec((1, tf), lambda i, j: (0, n_j + j)),
            pl.BlockSpec((tf, D_MODEL), lambda i, j: (j, 0)),
            pl.BlockSpec((1, D_MODEL), lambda i, j: (0, 0)),
        ],
        out_specs=pl.BlockSpec((tm, D_MODEL), lambda i, j: (i, 0)),
        out_shape=jax.ShapeDtypeStruct((t, D_MODEL), F32),
        scratch_shapes=[pltpu.VMEM((tm + 2 * halo, D_MODEL), BF16)],
        compiler_params=pltpu.CompilerParams(
            dimension_semantics=("parallel", "arbitrary"),
            vmem_limit_bytes=VMEM_LIMIT),
        name="ffn",
    )(h2, h2, h2, x2, w_up_bf16, w_up_bf16, conv_w, conv_w,
      conv_b.reshape(1, 2 * D_FF), conv_b.reshape(1, 2 * D_FF), w_down_bf16,
      final_w.reshape(1, D_MODEL))


def _rope_tables(seq):
    inv_freq = ROPE_BASE ** (-jnp.arange(0, HEAD_DIM, 2, dtype=F32) / HEAD_DIM)
    ang = jnp.arange(seq, dtype=F32)[:, None] * inv_freq[None, :]
    cos = jnp.cos(ang)
    sin = jnp.sin(ang)
    return jnp.concatenate([cos, cos], axis=-1), jnp.concatenate([-sin, sin], axis=-1)


def _per_slab(vec, slabs):
    return jnp.broadcast_to(vec.reshape(slabs, 1, LANES), (slabs, 8, LANES))


def _trunk(x, p):
    batch, seq, _ = x.shape
    xt = x.reshape(batch * seq, D_MODEL)
    z = _in_proj(xt, p["norm1_w"], p["w_in"])
    ret = _retention(z, p["cos"], p["sin"], p["lgf"], p["lgb"], p["gnw"], batch=batch, seq=seq)
    sgu = _sgu(z, p["sgu_nw"], p["w_s"], p["sgu_b"])
    x2, h2 = _out_proj(ret, sgu, xt, p["w_out"], p["norm2_w"])
    y = _ffn(h2, x2, p["w_up"], p["conv_w"], p["conv_b"], p["w_down"], p["final_w"], seq=seq)
    return y.reshape(batch, seq, D_MODEL)


def kernel(x_prompt, x_sample, norm1_w, w_in, ret_log_decay_fwd, ret_log_decay_bwd, ret_gn_w, sgu_norm_w, sgu_w_s, sgu_b, w_out, norm2_w, w_up, conv_w, conv_b, w_down, final_norm_w):
    max_seq = max(x_prompt.shape[1], x_sample.shape[1])
    cos, sin = _rope_tables(max_seq)
    p = {
        "norm1_w": norm1_w[0],
        "w_in": w_in[0].astype(BF16),
        "cos": cos,
        "sin": sin,
        "lgf": jnp.broadcast_to(ret_log_decay_fwd[0].astype(F32)[:, None, None], (HEADS, 8, LANES)),
        "lgb": jnp.broadcast_to(ret_log_decay_bwd[0].astype(F32)[:, None, None], (HEADS, 8, LANES)),
        "gnw": _per_slab(ret_gn_w[0], HEADS),
        "sgu_nw": _per_slab(sgu_norm_w[0], GROUPS),
        "w_s": sgu_w_s[0].astype(BF16),
        "sgu_b": jnp.broadcast_to(sgu_b[0][:, :, None], (GROUPS, CHUNK, LANES)),
        "w_out": w_out[0].astype(BF16),
        "norm2_w": norm2_w[0],
        "w_up": w_up[0].astype(BF16),
        "conv_w": conv_w[0],
        "conv_b": conv_b[0],
        "w_down": w_down[0].astype(BF16),
        "final_w": final_norm_w,
    }
    return (_trunk(x_prompt, p), _trunk(x_sample, p))
```

```python
import functools
import math

import jax
import jax.numpy as jnp
from jax import lax
from jax.experimental import pallas as pl
from jax.experimental.pallas import tpu as pltpu

D_MODEL = 2048
CHUNK = 128
HEADS = 8
HEAD_DIM = 128
RET_WIDTH = HEADS * HEAD_DIM
GROUPS = 8
SGU_WIDTH = GROUPS * CHUNK
IN_COLS = 4 * RET_WIDTH + 2 * SGU_WIDTH
IN_SLABS = IN_COLS // 128
D_FF = 5632
ROPE_BASE = 10000.0
EPS = 1e-6
LANES = 128
SUBLANES = 8
BF16_ROWS = 16
MXU_COLS = 256
VMEM_LIMIT = 56 * 1024 * 1024

F32 = jnp.float32
BF16 = jnp.bfloat16


def _rms(x, w):
    ms = jnp.mean(x * x, axis=-1, keepdims=True)
    return x * lax.rsqrt(ms + EPS) * w


def _gelu_tanh(x):
    c = math.sqrt(2.0 / math.pi)
    return x * (0.5 * (1.0 + jnp.tanh(c * (x + 0.044715 * (x * x * x)))))


def _silu(x):
    h = 0.5 * x
    return h + h * jnp.tanh(h)


def _inproj_kernel(x_ref, nw_ref, w_ref, z_ref, h_ref):
    @pl.when(pl.program_id(1) == 0)
    def _():
        h_ref[...] = _rms(x_ref[...], nw_ref[...]).astype(BF16)

    res = jnp.dot(h_ref[...], w_ref[...], preferred_element_type=F32)
    for s in range(z_ref.shape[0]):
        z_ref[s] = res[:, s * LANES:(s + 1) * LANES].astype(z_ref.dtype)


def _in_proj(x, norm_w, w_bf16, *, tm=1024, tn=1024):
    t = x.shape[0]
    return pl.pallas_call(
        _inproj_kernel,
        grid=(t // tm, IN_COLS // tn),
        in_specs=[
            pl.BlockSpec((tm, D_MODEL), lambda i, j: (i, 0)),
            pl.BlockSpec((1, D_MODEL), lambda i, j: (0, 0)),
            pl.BlockSpec((D_MODEL, tn), lambda i, j: (0, j)),
        ],
        out_specs=pl.BlockSpec((tn // LANES, tm, LANES), lambda i, j: (j, i, 0)),
        out_shape=jax.ShapeDtypeStruct((IN_SLABS, t, LANES), BF16),
        scratch_shapes=[pltpu.VMEM((tm, D_MODEL), BF16)],
        compiler_params=pltpu.CompilerParams(
            dimension_semantics=("parallel", "arbitrary"),
            vmem_limit_bytes=VMEM_LIMIT),
        name="in_proj",
    )(x, norm_w.reshape(1, D_MODEL), w_bf16)


def _ret_kernel(q_ref, k_ref, v_ref, g_ref, cos_ref, sin_ref, lgf_ref, lgb_ref, gnw_ref,
                o_ref, lhs_ref, kt_ref, kvf_ref, rb_ref):
    seq = q_ref.shape[1]
    n_chunks = seq // CHUNK
    lgf = lgf_ref[0, 0:1, :]
    lgb = lgb_ref[0, 0:1, :]
    row = lax.broadcasted_iota(jnp.int32, (CHUNK, CHUNK), 0).astype(F32)
    col = lax.broadcasted_iota(jnp.int32, (CHUNK, CHUNK), 1).astype(F32)
    diff = row - col
    dcomb = jnp.where(diff >= 0, jnp.exp(lgf * diff), jnp.exp(lgb * (-diff)))
    xi_f = jnp.exp(lgf * (row + 1.0))
    xi_b = jnp.exp(lgb * (CHUNK - row))
    zeta_f = jnp.exp(lgf * (CHUNK - 1.0 - col))
    zeta_b = jnp.exp(lgb * col)
    decay_f = jnp.exp(lgf * CHUNK)
    decay_b = jnp.exp(lgb * CHUNK)
    scale = HEAD_DIM ** -0.5

    def rows(c):
        return pl.ds(pl.multiple_of(c * CHUNK, CHUNK), CHUNK)

    def prep(i, state):
        c = n_chunks - 1 - i
        sl = rows(c)
        q = q_ref[0, sl, :].astype(F32)
        k = k_ref[0, sl, :].astype(F32)
        cos = cos_ref[sl, :]
        sin = sin_ref[sl, :]
        qr = (q * cos + pltpu.roll(q, HEAD_DIM // 2, 1) * sin) * scale
        kr = k * cos + pltpu.roll(k, HEAD_DIM // 2, 1) * sin
        kt = kr.T
        lhs_ref[sl, :] = jnp.concatenate([qr, qr * xi_f, qr * xi_b], axis=1).astype(BF16)
        kt_ref[c] = kt.astype(BF16)
        ktz = jnp.concatenate([kt * zeta_f, kt * zeta_b], axis=0).astype(BF16)
        kv = jnp.dot(ktz, v_ref[0, sl, :], preferred_element_type=F32)
        kvf_ref[c] = kv[:HEAD_DIM]
        rb_ref[c] = state.astype(BF16)
        return state * decay_b + kv[HEAD_DIM:]

    lax.fori_loop(0, n_chunks, prep, jnp.zeros((HEAD_DIM, HEAD_DIM), F32), unroll=4)

    gnw = gnw_ref[0, 0:1, :]

    def fwd(c, state):
        sl = rows(c)
        s = jnp.dot(lhs_ref[sl, 0:HEAD_DIM], kt_ref[c], preferred_element_type=F32) * dcomb
        lhs = jnp.concatenate([s.astype(BF16), lhs_ref[sl, HEAD_DIM:]], axis=1)
        rhs = jnp.concatenate([v_ref[0, sl, :], state.astype(BF16), rb_ref[c]], axis=0)
        o = jnp.dot(lhs, rhs, preferred_element_type=F32)
        mu = jnp.mean(o, axis=-1, keepdims=True)
        d = o - mu
        var = jnp.mean(d * d, axis=-1, keepdims=True)
        on = d * lax.rsqrt(var + EPS) * gnw
        g = g_ref[0, sl, :].astype(F32)
        o_ref[0, sl, :] = (_silu(g) * on).astype(o_ref.dtype)
        return state * decay_f + kvf_ref[c]

    lax.fori_loop(0, n_chunks, fwd, jnp.zeros((HEAD_DIM, HEAD_DIM), F32), unroll=8)


def _retention(z, cos, sin, lgf, lgb, gnw, *, batch, seq):
    t = batch * seq
    n_chunks = seq // CHUNK

    def zspec(base):
        return pl.BlockSpec((1, seq, LANES), lambda b, h: (base + h, b, 0))

    table = pl.BlockSpec((seq, LANES), lambda b, h: (0, 0), pipeline_mode=pl.Buffered(1))
    per_head = pl.BlockSpec((1, 8, LANES), lambda b, h: (h, 0, 0))
    return pl.pallas_call(
        _ret_kernel,
        grid=(batch, HEADS),
        in_specs=[zspec(0), zspec(HEADS), zspec(2 * HEADS), zspec(3 * HEADS),
                  table, table, per_head, per_head, per_head],
        out_specs=pl.BlockSpec((1, seq, LANES), lambda b, h: (h, b, 0)),
        out_shape=jax.ShapeDtypeStruct((HEADS, t, LANES), BF16),
        scratch_shapes=[
            pltpu.VMEM((seq, 3 * HEAD_DIM), BF16),
            pltpu.VMEM((n_chunks, HEAD_DIM, CHUNK), BF16),
            pltpu.VMEM((n_chunks, HEAD_DIM, HEAD_DIM), F32),
            pltpu.VMEM((n_chunks, HEAD_DIM, HEAD_DIM), BF16),
        ],
        compiler_params=pltpu.CompilerParams(
            dimension_semantics=("parallel", "parallel"),
            vmem_limit_bytes=VMEM_LIMIT),
        name="retention",
    )(z, z, z, z, cos, sin, lgf, lgb, gnw)


def _sgu_kernel(zu_ref, zv_ref, nw_ref, ws_ref, b_ref, o_ref):
    rows = zu_ref.shape[1]
    n_chunks = rows // CHUNK
    v = _gelu_tanh(zv_ref[...].astype(F32))
    ms = jnp.sum(jnp.sum(v * v, axis=0), axis=-1, keepdims=True) * (1.0 / SGU_WIDTH)
    inv = lax.rsqrt(ms + EPS)
    for g in range(GROUPS):
        vn = (v[g] * inv * nw_ref[g, 0:1, :]).astype(BF16)
        rhs = jnp.concatenate(
            [vn[c * CHUNK:(c + 1) * CHUNK] for c in range(n_chunks)], axis=1)
        sp = jnp.dot(ws_ref[g], rhs, preferred_element_type=F32)
        u = _gelu_tanh(zu_ref[g].astype(F32))
        bias = b_ref[g]
        for c in range(n_chunks):
            sl = slice(c * CHUNK, (c + 1) * CHUNK)
            o_ref[g, sl, :] = (u[sl] * (sp[:, sl] + bias)).astype(o_ref.dtype)


def _sgu(z, norm_w, ws_bf16, bias, *, rows=512):
    t = z.shape[1]
    u_base = 4 * HEADS // GROUPS
    return pl.pallas_call(
        _sgu_kernel,
        grid=(t // rows,),
        in_specs=[
            pl.BlockSpec((GROUPS, rows, LANES), lambda i: (u_base, i, 0)),
            pl.BlockSpec((GROUPS, rows, LANES), lambda i: (u_base + 1, i, 0)),
            pl.BlockSpec((GROUPS, 8, LANES), lambda i: (0, 0, 0)),
            pl.BlockSpec((GROUPS, CHUNK, CHUNK), lambda i: (0, 0, 0)),
            pl.BlockSpec((GROUPS, CHUNK, LANES), lambda i: (0, 0, 0)),
        ],
        out_specs=pl.BlockSpec((GROUPS, rows, LANES), lambda i: (0, i, 0)),
        out_shape=jax.ShapeDtypeStruct((GROUPS, t, LANES), BF16),
        compiler_params=pltpu.CompilerParams(
            dimension_semantics=("parallel",),
            vmem_limit_bytes=VMEM_LIMIT),
        name="sgu",
    )(z, z, norm_w, ws_bf16, bias)


def _outproj_kernel(ret_ref, sgu_ref, x_ref, w_ref, nw_ref, x2_ref, h2_ref):
    mix = jnp.concatenate([ret_ref[s] for s in range(HEADS)]
                          + [sgu_ref[s] for s in range(GROUPS)], axis=1)
    x2 = x_ref[...] + jnp.dot(mix, w_ref[...], preferred_element_type=F32)
    x2_ref[...] = x2
    h2_ref[...] = _rms(x2, nw_ref[...]).astype(BF16)


def _out_proj(ret, sgu, x, w_bf16, norm_w, *, tm=512):
    t = x.shape[0]
    return pl.pallas_call(
        _outproj_kernel,
        grid=(t // tm,),
        in_specs=[
            pl.BlockSpec((HEADS, tm, LANES), lambda i: (0, i, 0)),
            pl.BlockSpec((GROUPS, tm, LANES), lambda i: (0, i, 0)),
            pl.BlockSpec((tm, D_MODEL), lambda i: (i, 0)),
            pl.BlockSpec((D_MODEL, D_MODEL), lambda i: (0, 0)),
            pl.BlockSpec((1, D_MODEL), lambda i: (0, 0)),
        ],
        out_specs=[pl.BlockSpec((tm, D_MODEL), lambda i: (i, 0)),
                   pl.BlockSpec((tm, D_MODEL), lambda i: (i, 0))],
        out_shape=[jax.ShapeDtypeStruct((t, D_MODEL), F32),
                   jax.ShapeDtypeStruct((t, D_MODEL), BF16)],
        compiler_params=pltpu.CompilerParams(
            dimension_semantics=("parallel",),
            vmem_limit_bytes=VMEM_LIMIT),
        name="out_proj",
    )(ret, sgu, x, w_bf16, norm_w.reshape(1, D_MODEL))


def _ffn_up_kernel(hp_ref, h_ref, hn_ref, wa_ref, wb_ref, cwa_ref, cwb_ref, cba_ref, cbb_ref,
                   o_ref, lhs_ref, u0_ref, u1_ref, *, n_j, tiles_per_seq, row_block):
    s = pl.program_id(0)
    last = pl.num_programs(0) - 2
    sm = jnp.minimum(s, last)
    i = sm // n_j
    j = sm % n_j
    tm = h_ref.shape[0]
    tf = wa_ref.shape[1]
    halo = hp_ref.shape[0]

    @pl.when(s == 0)
    def _():
        u1_ref[...] = jnp.zeros_like(u1_ref)

    @pl.when((j == 0) & (s <= last))
    def _():
        pos = i % tiles_per_seq
        hp = hp_ref[...]
        hn = hn_ref[...]
        lhs_ref[0:halo, :] = jnp.where(pos == 0, jnp.zeros_like(hp), hp)
        lhs_ref[halo:halo + tm, :] = h_ref[...]
        lhs_ref[halo + tm:, :] = jnp.where(pos == tiles_per_seq - 1, jnp.zeros_like(hn), hn)

    def step(uw_ref, ur_ref):
        def conv(r, c0, cw_ref, cb_ref):
            n = row_block + 2 * SUBLANES
            u = ur_ref[halo - SUBLANES + r:halo + SUBLANES + r + row_block, c0:c0 + tf]
            prev = pltpu.roll(u, 1, 0)[SUBLANES:SUBLANES + row_block]
            nxt = pltpu.roll(u, n - 1, 0)[SUBLANES:SUBLANES + row_block]
            cur = u[SUBLANES:SUBLANES + row_block]
            return (prev * cw_ref[0:1, :] + cur * cw_ref[1:2, :] + nxt * cw_ref[2:3, :]
                    + cb_ref[...])

        def finish_rows(r):
            a = conv(r, 0, cwa_ref, cba_ref)
            b = conv(r, tf, cwb_ref, cbb_ref)
            o_ref[r:r + row_block, :] = (_silu(a) * b).astype(o_ref.dtype)

        for r in range(0, tm, row_block):
            finish_rows(r)
        lhs = lhs_ref[...]
        uw_ref[:, 0:tf] = jnp.dot(lhs, wa_ref[...], preferred_element_type=F32)
        uw_ref[:, tf:] = jnp.dot(lhs, wb_ref[...], preferred_element_type=F32)

    @pl.when(s % 2 == 0)
    def _():
        step(u0_ref, u1_ref)

    @pl.when(s % 2 == 1)
    def _():
        step(u1_ref, u0_ref)


def _ffn_up(h2, w_up_bf16, conv_w, conv_b, *, seq, tm=1024, tf=512, row_block=32):
    t = h2.shape[0]
    halo = BF16_ROWS
    n_j = D_FF // tf
    n_mm = (t // tm) * n_j
    halo_blocks = tm // halo
    last_halo_block = t // halo - 1

    def mm(s):
        return jnp.minimum(s, n_mm - 1)

    def ep(s):
        return jnp.maximum(s - 1, 0)

    kernel = functools.partial(_ffn_up_kernel, n_j=n_j, tiles_per_seq=seq // tm,
                               row_block=row_block)
    return pl.pallas_call(
        kernel,
        grid=(n_mm + 1,),
        in_specs=[
            pl.BlockSpec((halo, D_MODEL),
                         lambda s: (jnp.maximum((mm(s) // n_j) * halo_blocks - 1, 0), 0)),
            pl.BlockSpec((tm, D_MODEL), lambda s: (mm(s) // n_j, 0)),
            pl.BlockSpec((halo, D_MODEL),
                         lambda s: (jnp.minimum((mm(s) // n_j + 1) * halo_blocks, last_halo_block), 0)),
            pl.BlockSpec((D_MODEL, tf), lambda s: (0, mm(s) % n_j)),
            pl.BlockSpec((D_MODEL, tf), lambda s: (0, n_j + mm(s) % n_j)),
            pl.BlockSpec((3, tf), lambda s: (0, ep(s) % n_j)),
            pl.BlockSpec((3, tf), lambda s: (0, n_j + ep(s) % n_j)),
            pl.BlockSpec((1, tf), lambda s: (0, ep(s) % n_j)),
            pl.BlockSpec((1, tf), lambda s: (0, n_j + ep(s) % n_j)),
        ],
        out_specs=pl.BlockSpec((tm, tf), lambda s: (ep(s) // n_j, ep(s) % n_j)),
        out_shape=jax.ShapeDtypeStruct((t, D_FF), BF16),
        scratch_shapes=[pltpu.VMEM((tm + 2 * halo, D_MODEL), BF16),
                        pltpu.VMEM((tm + 2 * halo, 2 * tf), F32),
                        pltpu.VMEM((tm + 2 * halo, 2 * tf), F32)],
        compiler_params=pltpu.CompilerParams(
            dimension_semantics=("arbitrary",),
            vmem_limit_bytes=VMEM_LIMIT),
        name="ffn_up",
    )(h2, h2, h2, w_up_bf16, w_up_bf16, conv_w, conv_w,
      conv_b.reshape(1, 2 * D_FF), conv_b.reshape(1, 2 * D_FF))


def _ffn_down_kernel(act_ref, w_ref, x2_ref, fw_ref, o_ref):
    y = x2_ref[...] + jnp.dot(act_ref[...], w_ref[...], preferred_element_type=F32)
    o_ref[...] = _rms(y, fw_ref[...])


def _ffn_down(act, w_down_bf16, x2, final_w, *, tm=256):
    t = act.shape[0]
    return pl.pallas_call(
        _ffn_down_kernel,
        grid=(t // tm,),
        in_specs=[
            pl.BlockSpec((tm, D_FF), lambda i: (i, 0)),
            pl.BlockSpec((D_FF, D_MODEL), lambda i: (0, 0), pipeline_mode=pl.Buffered(1)),
            pl.BlockSpec((tm, D_MODEL), lambda i: (i, 0)),
            pl.BlockSpec((1, D_MODEL), lambda i: (0, 0)),
        ],
        out_specs=pl.BlockSpec((tm, D_MODEL), lambda i: (i, 0)),
        out_shape=jax.ShapeDtypeStruct((t, D_MODEL), F32),
        compiler_params=pltpu.CompilerParams(
            dimension_semantics=("parallel",),
            vmem_limit_bytes=VMEM_LIMIT),
        name="ffn_down",
    )(act, w_down_bf16, x2, final_w.reshape(1, D_MODEL))


def _ffn(h2, x2, w_up_bf16, conv_w, conv_b, w_down_bf16, final_w, *, seq):
    act = _ffn_up(h2, w_up_bf16, conv_w, conv_b, seq=seq)
    return _ffn_down(act, w_down_bf16, x2, final_w)


def _rope_tables(seq):
    inv_freq = ROPE_BASE ** (-jnp.arange(0, HEAD_DIM, 2, dtype=F32) / HEAD_DIM)
    ang = jnp.arange(seq, dtype=F32)[:, None] * inv_freq[None, :]
    cos = jnp.cos(ang)
    sin = jnp.sin(ang)
    return jnp.concatenate([cos, cos], axis=-1), jnp.concatenate([-sin, sin], axis=-1)


def _per_slab(vec, slabs):
    return jnp.broadcast_to(vec.reshape(slabs, 1, LANES), (slabs, 8, LANES))


def _trunk(x, p):
    batch, seq, _ = x.shape
    xt = x.reshape(batch * seq, D_MODEL)
    z = _in_proj(xt, p["norm1_w"], p["w_in"])
    ret = _retention(z, p["cos"], p["sin"], p["lgf"], p["lgb"], p["gnw"], batch=batch, seq=seq)
    sgu = _sgu(z, p["sgu_nw"], p["w_s"], p["sgu_b"])
    x2, h2 = _out_proj(ret, sgu, xt, p["w_out"], p["norm2_w"])
    y = _ffn(h2, x2, p["w_up"], p["conv_w"], p["conv_b"], p["w_down"], p["final_w"], seq=seq)
    return y.reshape(batch, seq, D_MODEL)


def kernel(x_prompt, x_sample, norm1_w, w_in, ret_log_decay_fwd, ret_log_decay_bwd, ret_gn_w, sgu_norm_w, sgu_w_s, sgu_b, w_out, norm2_w, w_up, conv_w, conv_b, w_down, final_norm_w):
    max_seq = max(x_prompt.shape[1], x_sample.shape[1])
    cos, sin = _rope_tables(max_seq)
    p = {
        "norm1_w": norm1_w[0],
        "w_in": w_in[0].astype(BF16),
        "cos": cos,
        "sin": sin,
        "lgf": jnp.broadcast_to(ret_log_decay_fwd[0].astype(F32)[:, None, None], (HEADS, 8, LANES)),
        "lgb": jnp.broadcast_to(ret_log_decay_bwd[0].astype(F32)[:, None, None], (HEADS, 8, LANES)),
        "gnw": _per_slab(ret_gn_w[0], HEADS),
        "sgu_nw": _per_slab(sgu_norm_w[0], GROUPS),
        "w_s": sgu_w_s[0].astype(BF16),
        "sgu_b": jnp.broadcast_to(sgu_b[0][:, :, None], (GROUPS, CHUNK, LANES)),
        "w_out": w_out[0].astype(BF16),
        "norm2_w": norm2_w[0],
        "w_up": w_up[0].astype(BF16),
        "conv_w": conv_w[0],
        "conv_b": conv_b[0],
        "w_down": w_down[0].astype(BF16),
        "final_w": final_norm_w,
    }
    return (_trunk(x_prompt, p), _trunk(x_sample, p))
```

```python
import functools
import math

import jax
import jax.numpy as jnp
from jax import lax
from jax.experimental import pallas as pl
from jax.experimental.pallas import tpu as pltpu

D_MODEL = 2048
CHUNK = 128
HEADS = 8
HEAD_DIM = 128
RET_WIDTH = HEADS * HEAD_DIM
GROUPS = 8
SGU_WIDTH = GROUPS * CHUNK
IN_COLS = 4 * RET_WIDTH + 2 * SGU_WIDTH
IN_SLABS = IN_COLS // 128
D_FF = 5632
ROPE_BASE = 10000.0
EPS = 1e-6
LANES = 128
SUBLANES = 8
BF16_ROWS = 16
IN_PROJ_TILE = 1024
FFN_TILE = 512
VMEM_LIMIT = 56 * 1024 * 1024

F32 = jnp.float32
BF16 = jnp.bfloat16


def _rms(x, w):
    ms = jnp.mean(x * x, axis=-1, keepdims=True)
    return x * lax.rsqrt(ms + EPS) * w


def _gelu_tanh(x):
    c = math.sqrt(2.0 / math.pi)
    return x * (0.5 * (1.0 + jnp.tanh(c * (x + 0.044715 * (x * x * x)))))


def _silu(x):
    h = 0.5 * x
    return h + h * jnp.tanh(h)


def _inproj_kernel(x_ref, nw_ref, w_ref, z_ref, h_ref):
    @pl.when(pl.program_id(1) == 0)
    def _():
        h_ref[...] = _rms(x_ref[...], nw_ref[...]).astype(BF16)

    res = jnp.dot(h_ref[...], w_ref[...], preferred_element_type=F32)
    for s in range(z_ref.shape[0]):
        z_ref[s] = res[:, s * LANES:(s + 1) * LANES].astype(z_ref.dtype)


def _in_proj(x, norm_w, w_tiles, *, tm=1024):
    t = x.shape[0]
    n_tiles, _, tn = w_tiles.shape
    return pl.pallas_call(
        _inproj_kernel,
        grid=(t // tm, n_tiles),
        in_specs=[
            pl.BlockSpec((tm, D_MODEL), lambda i, j: (i, 0)),
            pl.BlockSpec((1, D_MODEL), lambda i, j: (0, 0)),
            pl.BlockSpec((None, D_MODEL, tn), lambda i, j: (j, 0, 0)),
        ],
        out_specs=pl.BlockSpec((tn // LANES, tm, LANES), lambda i, j: (j, i, 0)),
        out_shape=jax.ShapeDtypeStruct((IN_SLABS, t, LANES), BF16),
        scratch_shapes=[pltpu.VMEM((tm, D_MODEL), BF16)],
        compiler_params=pltpu.CompilerParams(
            dimension_semantics=("parallel", "arbitrary"),
            vmem_limit_bytes=VMEM_LIMIT),
        name="in_proj",
    )(x, norm_w.reshape(1, D_MODEL), w_tiles)


def _ret_kernel(q_ref, k_ref, v_ref, g_ref, cos_ref, sin_ref, lgf_ref, lgb_ref, gnw_ref,
                o_ref, lhs_ref, kt_ref, kvf_ref, rb_ref):
    seq = q_ref.shape[1]
    n_chunks = seq // CHUNK
    lgf = lgf_ref[0, 0:1, :]
    lgb = lgb_ref[0, 0:1, :]
    row = lax.broadcasted_iota(jnp.int32, (CHUNK, CHUNK), 0).astype(F32)
    col = lax.broadcasted_iota(jnp.int32, (CHUNK, CHUNK), 1).astype(F32)
    diff = row - col
    dcomb = jnp.where(diff >= 0, jnp.exp(lgf * diff), jnp.exp(lgb * (-diff)))
    xi_f = jnp.exp(lgf * (row + 1.0))
    xi_b = jnp.exp(lgb * (CHUNK - row))
    zeta_f = jnp.exp(lgf * (CHUNK - 1.0 - col))
    zeta_b = jnp.exp(lgb * col)
    decay_f = jnp.exp(lgf * CHUNK)
    decay_b = jnp.exp(lgb * CHUNK)
    scale = HEAD_DIM ** -0.5

    def rows(c):
        return pl.ds(pl.multiple_of(c * CHUNK, CHUNK), CHUNK)

    def prep(i, state):
        c = n_chunks - 1 - i
        sl = rows(c)
        q = q_ref[0, sl, :].astype(F32)
        k = k_ref[0, sl, :].astype(F32)
        cos = cos_ref[sl, :]
        sin = sin_ref[sl, :]
        qr = (q * cos + pltpu.roll(q, HEAD_DIM // 2, 1) * sin) * scale
        kr = k * cos + pltpu.roll(k, HEAD_DIM // 2, 1) * sin
        kt = kr.T
        lhs_ref[sl, :] = jnp.concatenate([qr, qr * xi_f, qr * xi_b], axis=1).astype(BF16)
        kt_ref[c] = kt.astype(BF16)
        ktz = jnp.concatenate([kt * zeta_f, kt * zeta_b], axis=0).astype(BF16)
        kv = jnp.dot(ktz, v_ref[0, sl, :], preferred_element_type=F32)
        kvf_ref[c] = kv[:HEAD_DIM]
        rb_ref[c] = state.astype(BF16)
        return state * decay_b + kv[HEAD_DIM:]

    lax.fori_loop(0, n_chunks, prep, jnp.zeros((HEAD_DIM, HEAD_DIM), F32), unroll=4)

    gnw = gnw_ref[0, 0:1, :]

    def fwd(c, state):
        sl = rows(c)
        s = jnp.dot(lhs_ref[sl, 0:HEAD_DIM], kt_ref[c], preferred_element_type=F32) * dcomb
        lhs = jnp.concatenate([s.astype(BF16), lhs_ref[sl, HEAD_DIM:]], axis=1)
        rhs = jnp.concatenate([v_ref[0, sl, :], state.astype(BF16), rb_ref[c]], axis=0)
        o = jnp.dot(lhs, rhs, preferred_element_type=F32)
        mu = jnp.mean(o, axis=-1, keepdims=True)
        d = o - mu
        var = jnp.mean(d * d, axis=-1, keepdims=True)
        on = d * lax.rsqrt(var + EPS) * gnw
        g = g_ref[0, sl, :].astype(F32)
        o_ref[0, sl, :] = (_silu(g) * on).astype(o_ref.dtype)
        return state * decay_f + kvf_ref[c]

    lax.fori_loop(0, n_chunks, fwd, jnp.zeros((HEAD_DIM, HEAD_DIM), F32), unroll=8)


def _retention(z, cos, sin, lgf, lgb, gnw, *, batch, seq):
    t = batch * seq
    n_chunks = seq // CHUNK

    def zspec(base):
        return pl.BlockSpec((1, seq, LANES), lambda b, h: (base + h, b, 0))

    table = pl.BlockSpec((seq, LANES), lambda b, h: (0, 0), pipeline_mode=pl.Buffered(1))
    per_head = pl.BlockSpec((1, 8, LANES), lambda b, h: (h, 0, 0))
    return pl.pallas_call(
        _ret_kernel,
        grid=(batch, HEADS),
        in_specs=[zspec(0), zspec(HEADS), zspec(2 * HEADS), zspec(3 * HEADS),
                  table, table, per_head, per_head, per_head],
        out_specs=pl.BlockSpec((1, seq, LANES), lambda b, h: (h, b, 0)),
        out_shape=jax.ShapeDtypeStruct((HEADS, t, LANES), BF16),
        scratch_shapes=[
            pltpu.VMEM((seq, 3 * HEAD_DIM), BF16),
            pltpu.VMEM((n_chunks, HEAD_DIM, CHUNK), BF16),
            pltpu.VMEM((n_chunks, HEAD_DIM, HEAD_DIM), F32),
            pltpu.VMEM((n_chunks, HEAD_DIM, HEAD_DIM), BF16),
        ],
        compiler_params=pltpu.CompilerParams(
            dimension_semantics=("parallel", "parallel"),
            vmem_limit_bytes=VMEM_LIMIT),
        name="retention",
    )(z, z, z, z, cos, sin, lgf, lgb, gnw)


def _sgu_kernel(zu_ref, zv_ref, nw_ref, ws_ref, b_ref, o_ref):
    rows = zu_ref.shape[1]
    n_chunks = rows // CHUNK
    v = _gelu_tanh(zv_ref[...].astype(F32))
    ms = jnp.sum(jnp.sum(v * v, axis=0), axis=-1, keepdims=True) * (1.0 / SGU_WIDTH)
    inv = lax.rsqrt(ms + EPS)
    for g in range(GROUPS):
        vn = (v[g] * inv * nw_ref[g, 0:1, :]).astype(BF16)
        rhs = jnp.concatenate(
            [vn[c * CHUNK:(c + 1) * CHUNK] for c in range(n_chunks)], axis=1)
        sp = jnp.dot(ws_ref[g], rhs, preferred_element_type=F32)
        u = _gelu_tanh(zu_ref[g].astype(F32))
        bias = b_ref[g]
        for c in range(n_chunks):
            sl = slice(c * CHUNK, (c + 1) * CHUNK)
            o_ref[g, sl, :] = (u[sl] * (sp[:, sl] + bias)).astype(o_ref.dtype)


def _sgu(z, norm_w, ws_bf16, bias, *, rows=512):
    t = z.shape[1]
    u_base = 4 * HEADS // GROUPS
    return pl.pallas_call(
        _sgu_kernel,
        grid=(t // rows,),
        in_specs=[
            pl.BlockSpec((GROUPS, rows, LANES), lambda i: (u_base, i, 0)),
            pl.BlockSpec((GROUPS, rows, LANES), lambda i: (u_base + 1, i, 0)),
            pl.BlockSpec((GROUPS, 8, LANES), lambda i: (0, 0, 0)),
            pl.BlockSpec((GROUPS, CHUNK, CHUNK), lambda i: (0, 0, 0)),
            pl.BlockSpec((GROUPS, CHUNK, LANES), lambda i: (0, 0, 0)),
        ],
        out_specs=pl.BlockSpec((GROUPS, rows, LANES), lambda i: (0, i, 0)),
        out_shape=jax.ShapeDtypeStruct((GROUPS, t, LANES), BF16),
        compiler_params=pltpu.CompilerParams(
            dimension_semantics=("parallel",),
            vmem_limit_bytes=VMEM_LIMIT),
        name="sgu",
    )(z, z, norm_w, ws_bf16, bias)


def _outproj_kernel(ret_ref, sgu_ref, x_ref, w_ref, nw_ref, x2_ref, h2_ref):
    mix = jnp.concatenate([ret_ref[s] for s in range(HEADS)]
                          + [sgu_ref[s] for s in range(GROUPS)], axis=1)
    x2 = x_ref[...] + jnp.dot(mix, w_ref[...], preferred_element_type=F32)
    x2_ref[...] = x2
    h2_ref[...] = _rms(x2, nw_ref[...]).astype(BF16)


def _out_proj(ret, sgu, x, w_bf16, norm_w, *, tm=512):
    t = x.shape[0]
    return pl.pallas_call(
        _outproj_kernel,
        grid=(t // tm,),
        in_specs=[
            pl.BlockSpec((HEADS, tm, LANES), lambda i: (0, i, 0)),
            pl.BlockSpec((GROUPS, tm, LANES), lambda i: (0, i, 0)),
            pl.BlockSpec((tm, D_MODEL), lambda i: (i, 0)),
            pl.BlockSpec((D_MODEL, D_MODEL), lambda i: (0, 0)),
            pl.BlockSpec((1, D_MODEL), lambda i: (0, 0)),
        ],
        out_specs=[pl.BlockSpec((tm, D_MODEL), lambda i: (i, 0)),
                   pl.BlockSpec((tm, D_MODEL), lambda i: (i, 0))],
        out_shape=[jax.ShapeDtypeStruct((t, D_MODEL), F32),
                   jax.ShapeDtypeStruct((t, D_MODEL), BF16)],
        compiler_params=pltpu.CompilerParams(
            dimension_semantics=("parallel",),
            vmem_limit_bytes=VMEM_LIMIT),
        name="out_proj",
    )(ret, sgu, x, w_bf16, norm_w.reshape(1, D_MODEL))


def _ffn_up_kernel(hp_ref, h_ref, hn_ref, wa_ref, wb_ref, cwa_ref, cwb_ref, cba_ref, cbb_ref,
                   o_ref, lhs_ref, u0_ref, u1_ref, *, n_j, tiles_per_seq, row_block):
    s = pl.program_id(0)
    last = pl.num_programs(0) - 2
    sm = jnp.minimum(s, last)
    i = sm // n_j
    j = sm % n_j
    tm = h_ref.shape[0]
    tf = wa_ref.shape[1]
    halo = hp_ref.shape[0]

    @pl.when(s == 0)
    def _():
        u1_ref[...] = jnp.zeros_like(u1_ref)

    @pl.when((j == 0) & (s <= last))
    def _():
        pos = i % tiles_per_seq
        hp = hp_ref[...]
        hn = hn_ref[...]
        lhs_ref[0:halo, :] = jnp.where(pos == 0, jnp.zeros_like(hp), hp)
        lhs_ref[halo:halo + tm, :] = h_ref[...]
        lhs_ref[halo + tm:, :] = jnp.where(pos == tiles_per_seq - 1, jnp.zeros_like(hn), hn)

    def step(uw_ref, ur_ref):
        def conv(r, c0, cw_ref, cb_ref):
            n = row_block + 2 * SUBLANES
            u = ur_ref[halo - SUBLANES + r:halo + SUBLANES + r + row_block, c0:c0 + tf]
            prev = pltpu.roll(u, 1, 0)[SUBLANES:SUBLANES + row_block]
            nxt = pltpu.roll(u, n - 1, 0)[SUBLANES:SUBLANES + row_block]
            cur = u[SUBLANES:SUBLANES + row_block]
            return (prev * cw_ref[0:1, :] + cur * cw_ref[1:2, :] + nxt * cw_ref[2:3, :]
                    + cb_ref[...])

        def finish_rows(r):
            a = conv(r, 0, cwa_ref, cba_ref)
            b = conv(r, tf, cwb_ref, cbb_ref)
            o_ref[r:r + row_block, :] = (_silu(a) * b).astype(o_ref.dtype)

        for r in range(0, tm, row_block):
            finish_rows(r)
        lhs = lhs_ref[...]
        uw_ref[:, 0:tf] = jnp.dot(lhs, wa_ref[...], preferred_element_type=F32)
        uw_ref[:, tf:] = jnp.dot(lhs, wb_ref[...], preferred_element_type=F32)

    @pl.when(s % 2 == 0)
    def _():
        step(u0_ref, u1_ref)

    @pl.when(s % 2 == 1)
    def _():
        step(u1_ref, u0_ref)


def _ffn_up(h2, w_up_tiles, conv_w, conv_b, *, seq, tm=1024, row_block=32):
    t = h2.shape[0]
    halo = BF16_ROWS
    tf = w_up_tiles.shape[2]
    n_j = D_FF // tf
    n_mm = (t // tm) * n_j
    halo_blocks = tm // halo
    last_halo_block = t // halo - 1

    def mm(s):
        return jnp.minimum(s, n_mm - 1)

    def ep(s):
        return jnp.maximum(s - 1, 0)

    kernel = functools.partial(_ffn_up_kernel, n_j=n_j, tiles_per_seq=seq // tm,
                               row_block=row_block)
    return pl.pallas_call(
        kernel,
        grid=(n_mm + 1,),
        in_specs=[
            pl.BlockSpec((halo, D_MODEL),
                         lambda s: (jnp.maximum((mm(s) // n_j) * halo_blocks - 1, 0), 0)),
            pl.BlockSpec((tm, D_MODEL), lambda s: (mm(s) // n_j, 0)),
            pl.BlockSpec((halo, D_MODEL),
                         lambda s: (jnp.minimum((mm(s) // n_j + 1) * halo_blocks, last_halo_block), 0)),
            pl.BlockSpec((None, D_MODEL, tf), lambda s: (mm(s) % n_j, 0, 0)),
            pl.BlockSpec((None, D_MODEL, tf), lambda s: (n_j + mm(s) % n_j, 0, 0)),
            pl.BlockSpec((3, tf), lambda s: (0, ep(s) % n_j)),
            pl.BlockSpec((3, tf), lambda s: (0, n_j + ep(s) % n_j)),
            pl.BlockSpec((1, tf), lambda s: (0, ep(s) % n_j)),
            pl.BlockSpec((1, tf), lambda s: (0, n_j + ep(s) % n_j)),
        ],
        out_specs=pl.BlockSpec((None, tm, tf), lambda s: (ep(s) % n_j, ep(s) // n_j, 0)),
        out_shape=jax.ShapeDtypeStruct((n_j, t, tf), BF16),
        scratch_shapes=[pltpu.VMEM((tm + 2 * halo, D_MODEL), BF16),
                        pltpu.VMEM((tm + 2 * halo, 2 * tf), F32),
                        pltpu.VMEM((tm + 2 * halo, 2 * tf), F32)],
        compiler_params=pltpu.CompilerParams(
            dimension_semantics=("arbitrary",),
            vmem_limit_bytes=VMEM_LIMIT),
        name="ffn_up",
    )(h2, h2, h2, w_up_tiles, w_up_tiles, conv_w, conv_w,
      conv_b.reshape(1, 2 * D_FF), conv_b.reshape(1, 2 * D_FF))


def _ffn_down_kernel(act_ref, w_ref, x2_ref, fw_ref, o_ref):
    act = jnp.concatenate([act_ref[j] for j in range(act_ref.shape[0])], axis=1)
    y = x2_ref[...] + jnp.dot(act, w_ref[...], preferred_element_type=F32)
    o_ref[...] = _rms(y, fw_ref[...])


def _ffn_down(act, w_down_bf16, x2, final_w, *, tm=256):
    n_j, t, tf = act.shape
    return pl.pallas_call(
        _ffn_down_kernel,
        grid=(t // tm,),
        in_specs=[
            pl.BlockSpec((n_j, tm, tf), lambda i: (0, i, 0)),
            pl.BlockSpec((D_FF, D_MODEL), lambda i: (0, 0), pipeline_mode=pl.Buffered(1)),
            pl.BlockSpec((tm, D_MODEL), lambda i: (i, 0)),
            pl.BlockSpec((1, D_MODEL), lambda i: (0, 0)),
        ],
        out_specs=pl.BlockSpec((tm, D_MODEL), lambda i: (i, 0)),
        out_shape=jax.ShapeDtypeStruct((t, D_MODEL), F32),
        compiler_params=pltpu.CompilerParams(
            dimension_semantics=("parallel",),
            vmem_limit_bytes=VMEM_LIMIT),
        name="ffn_down",
    )(act, w_down_bf16, x2, final_w.reshape(1, D_MODEL))


def _ffn(h2, x2, w_up_tiles, conv_w, conv_b, w_down_bf16, final_w, *, seq):
    act = _ffn_up(h2, w_up_tiles, conv_w, conv_b, seq=seq)
    return _ffn_down(act, w_down_bf16, x2, final_w)


def _rope_tables(seq):
    inv_freq = ROPE_BASE ** (-jnp.arange(0, HEAD_DIM, 2, dtype=F32) / HEAD_DIM)
    ang = jnp.arange(seq, dtype=F32)[:, None] * inv_freq[None, :]
    cos = jnp.cos(ang)
    sin = jnp.sin(ang)
    return jnp.concatenate([cos, cos], axis=-1), jnp.concatenate([-sin, sin], axis=-1)


def _col_tiles(w, tile):
    k, n = w.shape
    return w.astype(BF16).reshape(k, n // tile, tile).transpose(1, 0, 2)


def _per_slab(vec, slabs):
    return jnp.broadcast_to(vec.reshape(slabs, 1, LANES), (slabs, 8, LANES))


def _trunk(x, p):
    batch, seq, _ = x.shape
    xt = x.reshape(batch * seq, D_MODEL)
    z = _in_proj(xt, p["norm1_w"], p["w_in"])
    ret = _retention(z, p["cos"], p["sin"], p["lgf"], p["lgb"], p["gnw"], batch=batch, seq=seq)
    sgu = _sgu(z, p["sgu_nw"], p["w_s"], p["sgu_b"])
    x2, h2 = _out_proj(ret, sgu, xt, p["w_out"], p["norm2_w"])
    y = _ffn(h2, x2, p["w_up"], p["conv_w"], p["conv_b"], p["w_down"], p["final_w"], seq=seq)
    return y.reshape(batch, seq, D_MODEL)


def kernel(x_prompt, x_sample, norm1_w, w_in, ret_log_decay_fwd, ret_log_decay_bwd, ret_gn_w, sgu_norm_w, sgu_w_s, sgu_b, w_out, norm2_w, w_up, conv_w, conv_b, w_down, final_norm_w):
    max_seq = max(x_prompt.shape[1], x_sample.shape[1])
    cos, sin = _rope_tables(max_seq)
    p = {
        "norm1_w": norm1_w[0],
        "w_in": _col_tiles(w_in[0], IN_PROJ_TILE),
        "cos": cos,
        "sin": sin,
        "lgf": jnp.broadcast_to(ret_log_decay_fwd[0].astype(F32)[:, None, None], (HEADS, 8, LANES)),
        "lgb": jnp.broadcast_to(ret_log_decay_bwd[0].astype(F32)[:, None, None], (HEADS, 8, LANES)),
        "gnw": _per_slab(ret_gn_w[0], HEADS),
        "sgu_nw": _per_slab(sgu_norm_w[0], GROUPS),
        "w_s": sgu_w_s[0].astype(BF16),
        "sgu_b": jnp.broadcast_to(sgu_b[0][:, :, None], (GROUPS, CHUNK, LANES)),
        "w_out": w_out[0].astype(BF16),
        "norm2_w": norm2_w[0],
        "w_up": _col_tiles(w_up[0], FFN_TILE),
        "conv_w": conv_w[0],
        "conv_b": conv_b[0],
        "w_down": w_down[0].astype(BF16),
        "final_w": final_norm_w,
    }
    return (_trunk(x_prompt, p), _trunk(x_sample, p))
```

```python
import functools
import math

import jax
import jax.numpy as jnp
from jax import lax
from jax.experimental import pallas as pl
from jax.experimental.pallas import tpu as pltpu

D_MODEL = 2048
CHUNK = 128
HEADS = 8
HEAD_DIM = 128
RET_WIDTH = HEADS * HEAD_DIM
GROUPS = 8
SGU_WIDTH = GROUPS * CHUNK
IN_COLS = 4 * RET_WIDTH + 2 * SGU_WIDTH
IN_SLABS = IN_COLS // 128
D_FF = 5632
ROPE_BASE = 10000.0
EPS = 1e-6
LANES = 128
SUBLANES = 8
BF16_ROWS = 16
IN_PROJ_TILE = 1024
FFN_TILE = 512
VMEM_LIMIT = 56 * 1024 * 1024

F32 = jnp.float32
BF16 = jnp.bfloat16


def _rms(x, w):
    ms = jnp.mean(x * x, axis=-1, keepdims=True)
    return x * lax.rsqrt(ms + EPS) * w


def _gelu_tanh(x):
    c = math.sqrt(2.0 / math.pi)
    return x * (0.5 * (1.0 + jnp.tanh(c * (x + 0.044715 * (x * x * x)))))


def _silu(x):
    h = 0.5 * x
    return h + h * jnp.tanh(h)


def _inproj_kernel(x_ref, nw_ref, w_ref, z_ref, h_ref):
    @pl.when(pl.program_id(1) == 0)
    def _():
        h_ref[...] = _rms(x_ref[...], nw_ref[...]).astype(BF16)

    res = jnp.dot(h_ref[...], w_ref[...], preferred_element_type=F32)
    for s in range(z_ref.shape[0]):
        z_ref[s] = res[:, s * LANES:(s + 1) * LANES].astype(z_ref.dtype)


def _in_proj(x, norm_w, w_bf16, *, tm=1024, tn=IN_PROJ_TILE):
    t = x.shape[0]
    return pl.pallas_call(
        _inproj_kernel,
        grid=(t // tm, IN_COLS // tn),
        in_specs=[
            pl.BlockSpec((tm, D_MODEL), lambda i, j: (i, 0)),
            pl.BlockSpec((1, D_MODEL), lambda i, j: (0, 0)),
            pl.BlockSpec((D_MODEL, tn), lambda i, j: (0, j)),
        ],
        out_specs=pl.BlockSpec((tn // LANES, tm, LANES), lambda i, j: (j, i, 0)),
        out_shape=jax.ShapeDtypeStruct((IN_SLABS, t, LANES), BF16),
        scratch_shapes=[pltpu.VMEM((tm, D_MODEL), BF16)],
        compiler_params=pltpu.CompilerParams(
            dimension_semantics=("parallel", "arbitrary"),
            vmem_limit_bytes=VMEM_LIMIT),
        name="in_proj",
    )(x, norm_w.reshape(1, D_MODEL), w_bf16)


def _ret_kernel(q_ref, k_ref, v_ref, g_ref, cos_ref, sin_ref, lgf_ref, lgb_ref, gnw_ref,
                o_ref, lhs_ref, kt_ref, kvf_ref, rb_ref):
    seq = q_ref.shape[1]
    n_chunks = seq // CHUNK
    lgf = lgf_ref[0, 0:1, :]
    lgb = lgb_ref[0, 0:1, :]
    row = lax.broadcasted_iota(jnp.int32, (CHUNK, CHUNK), 0).astype(F32)
    col = lax.broadcasted_iota(jnp.int32, (CHUNK, CHUNK), 1).astype(F32)
    diff = row - col
    dcomb = jnp.where(diff >= 0, jnp.exp(lgf * diff), jnp.exp(lgb * (-diff)))
    xi_f = jnp.exp(lgf * (row + 1.0))
    xi_b = jnp.exp(lgb * (CHUNK - row))
    zeta_f = jnp.exp(lgf * (CHUNK - 1.0 - col))
    zeta_b = jnp.exp(lgb * col)
    decay_f = jnp.exp(lgf * CHUNK)
    decay_b = jnp.exp(lgb * CHUNK)
    scale = HEAD_DIM ** -0.5

    def rows(c):
        return pl.ds(pl.multiple_of(c * CHUNK, CHUNK), CHUNK)

    def prep(i, state):
        c = n_chunks - 1 - i
        sl = rows(c)
        q = q_ref[0, sl, :].astype(F32)
        k = k_ref[0, sl, :].astype(F32)
        cos = cos_ref[sl, :]
        sin = sin_ref[sl, :]
        qr = (q * cos + pltpu.roll(q, HEAD_DIM // 2, 1) * sin) * scale
        kr = k * cos + pltpu.roll(k, HEAD_DIM // 2, 1) * sin
        kt = kr.T
        lhs_ref[sl, :] = jnp.concatenate([qr, qr * xi_f, qr * xi_b], axis=1).astype(BF16)
        kt_ref[c] = kt.astype(BF16)
        ktz = jnp.concatenate([kt * zeta_f, kt * zeta_b], axis=0).astype(BF16)
        kv = jnp.dot(ktz, v_ref[0, sl, :], preferred_element_type=F32)
        kvf_ref[c] = kv[:HEAD_DIM]
        rb_ref[c] = state.astype(BF16)
        return state * decay_b + kv[HEAD_DIM:]

    lax.fori_loop(0, n_chunks, prep, jnp.zeros((HEAD_DIM, HEAD_DIM), F32), unroll=4)

    gnw = gnw_ref[0, 0:1, :]

    def fwd(c, state):
        sl = rows(c)
        s = jnp.dot(lhs_ref[sl, 0:HEAD_DIM], kt_ref[c], preferred_element_type=F32) * dcomb
        lhs = jnp.concatenate([s.astype(BF16), lhs_ref[sl, HEAD_DIM:]], axis=1)
        rhs = jnp.concatenate([v_ref[0, sl, :], state.astype(BF16), rb_ref[c]], axis=0)
        o = jnp.dot(lhs, rhs, preferred_element_type=F32)
        mu = jnp.mean(o, axis=-1, keepdims=True)
        d = o - mu
        var = jnp.mean(d * d, axis=-1, keepdims=True)
        on = d * lax.rsqrt(var + EPS) * gnw
        g = g_ref[0, sl, :].astype(F32)
        o_ref[0, sl, :] = (_silu(g) * on).astype(o_ref.dtype)
        return state * decay_f + kvf_ref[c]

    lax.fori_loop(0, n_chunks, fwd, jnp.zeros((HEAD_DIM, HEAD_DIM), F32), unroll=8)


def _retention(z, cos, sin, lgf, lgb, gnw, *, batch, seq):
    t = batch * seq
    n_chunks = seq // CHUNK

    def zspec(base):
        return pl.BlockSpec((1, seq, LANES), lambda b, h: (base + h, b, 0))

    table = pl.BlockSpec((seq, LANES), lambda b, h: (0, 0), pipeline_mode=pl.Buffered(1))
    per_head = pl.BlockSpec((1, 8, LANES), lambda b, h: (h, 0, 0))
    return pl.pallas_call(
        _ret_kernel,
        grid=(batch, HEADS),
        in_specs=[zspec(0), zspec(HEADS), zspec(2 * HEADS), zspec(3 * HEADS),
                  table, table, per_head, per_head, per_head],
        out_specs=pl.BlockSpec((1, seq, LANES), lambda b, h: (h, b, 0)),
        out_shape=jax.ShapeDtypeStruct((HEADS, t, LANES), BF16),
        scratch_shapes=[
            pltpu.VMEM((seq, 3 * HEAD_DIM), BF16),
            pltpu.VMEM((n_chunks, HEAD_DIM, CHUNK), BF16),
            pltpu.VMEM((n_chunks, HEAD_DIM, HEAD_DIM), F32),
            pltpu.VMEM((n_chunks, HEAD_DIM, HEAD_DIM), BF16),
        ],
        compiler_params=pltpu.CompilerParams(
            dimension_semantics=("parallel", "parallel"),
            vmem_limit_bytes=VMEM_LIMIT),
        name="retention",
    )(z, z, z, z, cos, sin, lgf, lgb, gnw)


def _sgu_kernel(zu_ref, zv_ref, nw_ref, ws_ref, b_ref, o_ref):
    rows = zu_ref.shape[1]
    n_chunks = rows // CHUNK
    v = _gelu_tanh(zv_ref[...].astype(F32))
    ms = jnp.sum(jnp.sum(v * v, axis=0), axis=-1, keepdims=True) * (1.0 / SGU_WIDTH)
    inv = lax.rsqrt(ms + EPS)
    for g in range(GROUPS):
        vn = (v[g] * inv * nw_ref[g, 0:1, :]).astype(BF16)
        rhs = jnp.concatenate(
            [vn[c * CHUNK:(c + 1) * CHUNK] for c in range(n_chunks)], axis=1)
        sp = jnp.dot(ws_ref[g], rhs, preferred_element_type=F32)
        u = _gelu_tanh(zu_ref[g].astype(F32))
        bias = b_ref[g]
        for c in range(n_chunks):
            sl = slice(c * CHUNK, (c + 1) * CHUNK)
            o_ref[g, sl, :] = (u[sl] * (sp[:, sl] + bias)).astype(o_ref.dtype)


def _sgu(z, norm_w, ws_bf16, bias, *, rows=512):
    t = z.shape[1]
    u_base = 4 * HEADS // GROUPS
    return pl.pallas_call(
        _sgu_kernel,
        grid=(t // rows,),
        in_specs=[
            pl.BlockSpec((GROUPS, rows, LANES), lambda i: (u_base, i, 0)),
            pl.BlockSpec((GROUPS, rows, LANES), lambda i: (u_base + 1, i, 0)),
            pl.BlockSpec((GROUPS, 8, LANES), lambda i: (0, 0, 0)),
            pl.BlockSpec((GROUPS, CHUNK, CHUNK), lambda i: (0, 0, 0)),
            pl.BlockSpec((GROUPS, CHUNK, LANES), lambda i: (0, 0, 0)),
        ],
        out_specs=pl.BlockSpec((GROUPS, rows, LANES), lambda i: (0, i, 0)),
        out_shape=jax.ShapeDtypeStruct((GROUPS, t, LANES), BF16),
        compiler_params=pltpu.CompilerParams(
            dimension_semantics=("parallel",),
            vmem_limit_bytes=VMEM_LIMIT),
        name="sgu",
    )(z, z, norm_w, ws_bf16, bias)


def _outproj_kernel(ret_ref, sgu_ref, x_ref, w_ref, nw_ref, x2_ref, h2_ref):
    mix = jnp.concatenate([ret_ref[s] for s in range(HEADS)]
                          + [sgu_ref[s] for s in range(GROUPS)], axis=1)
    x2 = x_ref[...] + jnp.dot(mix, w_ref[...], preferred_element_type=F32)
    x2_ref[...] = x2
    h2_ref[...] = _rms(x2, nw_ref[...]).astype(BF16)


def _out_proj(ret, sgu, x, w_bf16, norm_w, *, tm=512):
    t = x.shape[0]
    return pl.pallas_call(
        _outproj_kernel,
        grid=(t // tm,),
        in_specs=[
            pl.BlockSpec((HEADS, tm, LANES), lambda i: (0, i, 0)),
            pl.BlockSpec((GROUPS, tm, LANES), lambda i: (0, i, 0)),
            pl.BlockSpec((tm, D_MODEL), lambda i: (i, 0)),
            pl.BlockSpec((D_MODEL, D_MODEL), lambda i: (0, 0)),
            pl.BlockSpec((1, D_MODEL), lambda i: (0, 0)),
        ],
        out_specs=[pl.BlockSpec((tm, D_MODEL), lambda i: (i, 0)),
                   pl.BlockSpec((tm, D_MODEL), lambda i: (i, 0))],
        out_shape=[jax.ShapeDtypeStruct((t, D_MODEL), F32),
                   jax.ShapeDtypeStruct((t, D_MODEL), BF16)],
        compiler_params=pltpu.CompilerParams(
            dimension_semantics=("parallel",),
            vmem_limit_bytes=VMEM_LIMIT),
        name="out_proj",
    )(ret, sgu, x, w_bf16, norm_w.reshape(1, D_MODEL))


def _ffn_up_kernel(hp_ref, h_ref, hn_ref, wa_ref, wb_ref, cwa_ref, cwb_ref, cba_ref, cbb_ref,
                   o_ref, lhs_ref, u0_ref, u1_ref, *, n_j, tiles_per_seq, row_block, vpu_first):
    s = pl.program_id(0)
    last = pl.num_programs(0) - 2
    sm = jnp.minimum(s, last)
    i = sm // n_j
    j = sm % n_j
    tm = h_ref.shape[0]
    tf = wa_ref.shape[1]
    halo = hp_ref.shape[0]

    @pl.when(s == 0)
    def _():
        u1_ref[...] = jnp.zeros_like(u1_ref)

    @pl.when((j == 0) & (s <= last))
    def _():
        pos = i % tiles_per_seq
        hp = hp_ref[...]
        hn = hn_ref[...]
        lhs_ref[0:halo, :] = jnp.where(pos == 0, jnp.zeros_like(hp), hp)
        lhs_ref[halo:halo + tm, :] = h_ref[...]
        lhs_ref[halo + tm:, :] = jnp.where(pos == tiles_per_seq - 1, jnp.zeros_like(hn), hn)

    def step(uw_ref, ur_ref):
        def conv(r, c0, cw_ref, cb_ref):
            n = row_block + 2 * SUBLANES
            u = ur_ref[halo - SUBLANES + r:halo + SUBLANES + r + row_block, c0:c0 + tf]
            prev = pltpu.roll(u, 1, 0)[SUBLANES:SUBLANES + row_block]
            nxt = pltpu.roll(u, n - 1, 0)[SUBLANES:SUBLANES + row_block]
            cur = u[SUBLANES:SUBLANES + row_block]
            return (prev * cw_ref[0:1, :] + cur * cw_ref[1:2, :] + nxt * cw_ref[2:3, :]
                    + cb_ref[...])

        def finish_rows(r):
            a = conv(r, 0, cwa_ref, cba_ref)
            b = conv(r, tf, cwb_ref, cbb_ref)
            o_ref[r:r + row_block, :] = (_silu(a) * b).astype(o_ref.dtype)

        def finish_all():
            for r in range(0, tm, row_block):
                finish_rows(r)

        if vpu_first:
            finish_all()
        lhs = lhs_ref[...]
        uw_ref[:, 0:tf] = jnp.dot(lhs, wa_ref[...], preferred_element_type=F32)
        uw_ref[:, tf:] = jnp.dot(lhs, wb_ref[...], preferred_element_type=F32)
        if not vpu_first:
            finish_all()

    @pl.when(s % 2 == 0)
    def _():
        step(u0_ref, u1_ref)

    @pl.when(s % 2 == 1)
    def _():
        step(u1_ref, u0_ref)


def _ffn_up(h2, w_up_bf16, conv_w, conv_b, *, seq, tm=1024, tf=FFN_TILE, row_block=32,
            vpu_first=True):
    t = h2.shape[0]
    halo = BF16_ROWS
    n_j = D_FF // tf
    n_mm = (t // tm) * n_j
    halo_blocks = tm // halo
    last_halo_block = t // halo - 1

    def mm(s):
        return jnp.minimum(s, n_mm - 1)

    def ep(s):
        return jnp.maximum(s - 1, 0)

    kernel = functools.partial(_ffn_up_kernel, n_j=n_j, tiles_per_seq=seq // tm,
                               row_block=row_block, vpu_first=vpu_first)
    return pl.pallas_call(
        kernel,
        grid=(n_mm + 1,),
        in_specs=[
            pl.BlockSpec((halo, D_MODEL),
                         lambda s: (jnp.maximum((mm(s) // n_j) * halo_blocks - 1, 0), 0)),
            pl.BlockSpec((tm, D_MODEL), lambda s: (mm(s) // n_j, 0)),
            pl.BlockSpec((halo, D_MODEL),
                         lambda s: (jnp.minimum((mm(s) // n_j + 1) * halo_blocks, last_halo_block), 0)),
            pl.BlockSpec((D_MODEL, tf), lambda s: (0, mm(s) % n_j)),
            pl.BlockSpec((D_MODEL, tf), lambda s: (0, n_j + mm(s) % n_j)),
            pl.BlockSpec((3, tf), lambda s: (0, ep(s) % n_j)),
            pl.BlockSpec((3, tf), lambda s: (0, n_j + ep(s) % n_j)),
            pl.BlockSpec((1, tf), lambda s: (0, ep(s) % n_j)),
            pl.BlockSpec((1, tf), lambda s: (0, n_j + ep(s) % n_j)),
        ],
        out_specs=pl.BlockSpec((None, tm, tf), lambda s: (ep(s) % n_j, ep(s) // n_j, 0)),
        out_shape=jax.ShapeDtypeStruct((n_j, t, tf), BF16),
        scratch_shapes=[pltpu.VMEM((tm + 2 * halo, D_MODEL), BF16),
                        pltpu.VMEM((tm + 2 * halo, 2 * tf), F32),
                        pltpu.VMEM((tm + 2 * halo, 2 * tf), F32)],
        compiler_params=pltpu.CompilerParams(
            dimension_semantics=("arbitrary",),
            vmem_limit_bytes=VMEM_LIMIT),
        name="ffn_up",
    )(h2, h2, h2, w_up_bf16, w_up_bf16, conv_w, conv_w,
      conv_b.reshape(1, 2 * D_FF), conv_b.reshape(1, 2 * D_FF))


def _ffn_down_kernel(act_ref, w_ref, x2_ref, fw_ref, o_ref):
    act = jnp.concatenate([act_ref[j] for j in range(act_ref.shape[0])], axis=1)
    y = x2_ref[...] + jnp.dot(act, w_ref[...], preferred_element_type=F32)
    o_ref[...] = _rms(y, fw_ref[...])


def _ffn_down(act, w_down_bf16, x2, final_w, *, tm=256):
    n_j, t, tf = act.shape
    return pl.pallas_call(
        _ffn_down_kernel,
        grid=(t // tm,),
        in_specs=[
            pl.BlockSpec((n_j, tm, tf), lambda i: (0, i, 0)),
            pl.BlockSpec((D_FF, D_MODEL), lambda i: (0, 0), pipeline_mode=pl.Buffered(1)),
            pl.BlockSpec((tm, D_MODEL), lambda i: (i, 0)),
            pl.BlockSpec((1, D_MODEL), lambda i: (0, 0)),
        ],
        out_specs=pl.BlockSpec((tm, D_MODEL), lambda i: (i, 0)),
        out_shape=jax.ShapeDtypeStruct((t, D_MODEL), F32),
        compiler_params=pltpu.CompilerParams(
            dimension_semantics=("parallel",),
            vmem_limit_bytes=VMEM_LIMIT),
        name="ffn_down",
    )(act, w_down_bf16, x2, final_w.reshape(1, D_MODEL))


def _ffn(h2, x2, w_up_bf16, conv_w, conv_b, w_down_bf16, final_w, *, seq):
    if seq == 4096:
        act = _ffn_up(h2, w_up_bf16, conv_w, conv_b, seq=seq, tm=512, vpu_first=True)
    else:
        act = _ffn_up(h2, w_up_bf16, conv_w, conv_b, seq=seq, tm=1024, vpu_first=False)
    return _ffn_down(act, w_down_bf16, x2, final_w, tm=512 if seq == 4096 else 256)


def _rope_tables(seq):
    inv_freq = ROPE_BASE ** (-jnp.arange(0, HEAD_DIM, 2, dtype=F32) / HEAD_DIM)
    ang = jnp.arange(seq, dtype=F32)[:, None] * inv_freq[None, :]
    cos = jnp.cos(ang)
    sin = jnp.sin(ang)
    return jnp.concatenate([cos, cos], axis=-1), jnp.concatenate([-sin, sin], axis=-1)


def _col_tiles(w, tile):
    k, n = w.shape
    return w.astype(BF16).reshape(k, n // tile, tile).transpose(1, 0, 2)


def _per_slab(vec, slabs):
    return jnp.broadcast_to(vec.reshape(slabs, 1, LANES), (slabs, 8, LANES))


def _trunk(x, p):
    batch, seq, _ = x.shape
    xt = x.reshape(batch * seq, D_MODEL)
    z = _in_proj(xt, p["norm1_w"], p["w_in"], tn=2048 if seq == 4096 else 1024)
    ret = _retention(z, p["cos"], p["sin"], p["lgf"], p["lgb"], p["gnw"], batch=batch, seq=seq)
    sgu = _sgu(z, p["sgu_nw"], p["w_s"], p["sgu_b"])
    x2, h2 = _out_proj(ret, sgu, xt, p["w_out"], p["norm2_w"])
    y = _ffn(h2, x2, p["w_up"], p["conv_w"], p["conv_b"], p["w_down"], p["final_w"], seq=seq)
    return y.reshape(batch, seq, D_MODEL)


def kernel(x_prompt, x_sample, norm1_w, w_in, ret_log_decay_fwd, ret_log_decay_bwd, ret_gn_w, sgu_norm_w, sgu_w_s, sgu_b, w_out, norm2_w, w_up, conv_w, conv_b, w_down, final_norm_w):
    max_seq = max(x_prompt.shape[1], x_sample.shape[1])
    cos, sin = _rope_tables(max_seq)
    p = {
        "norm1_w": norm1_w[0],
        "w_in": w_in[0].astype(BF16),
        "cos": cos,
        "sin": sin,
        "lgf": jnp.broadcast_to(ret_log_decay_fwd[0].astype(F32)[:, None, None], (HEADS, 8, LANES)),
        "lgb": jnp.broadcast_to(ret_log_decay_bwd[0].astype(F32)[:, None, None], (HEADS, 8, LANES)),
        "gnw": _per_slab(ret_gn_w[0], HEADS),
        "sgu_nw": _per_slab(sgu_norm_w[0], GROUPS),
        "w_s": sgu_w_s[0].astype(BF16),
        "sgu_b": jnp.broadcast_to(sgu_b[0][:, :, None], (GROUPS, CHUNK, LANES)),
        "w_out": w_out[0].astype(BF16),
        "norm2_w": norm2_w[0],
        "w_up": w_up[0].astype(BF16),
        "conv_w": conv_w[0],
        "conv_b": conv_b[0],
        "w_down": w_down[0].astype(BF16),
        "final_w": final_norm_w,
    }
    return (_trunk(x_prompt, p), _trunk(x_sample, p))
```

```python
import functools
import math

import jax
import jax.numpy as jnp
import numpy as np
from jax import lax
from jax.experimental import pallas as pl
from jax.experimental.pallas import tpu as pltpu

D_MODEL = 2048
CHUNK = 128
HEADS = 8
HEAD_DIM = 128
RET_WIDTH = HEADS * HEAD_DIM
GROUPS = 8
SGU_WIDTH = GROUPS * CHUNK
IN_COLS = 4 * RET_WIDTH + 2 * SGU_WIDTH
IN_SLABS = IN_COLS // 128
D_FF = 5632
ROPE_BASE = 10000.0
EPS = 1e-6
LANES = 128
SUBLANES = 8
BF16_ROWS = 16
IN_PROJ_TILE = 2048
FFN_TILE = 512
VMEM_LIMIT = 56 * 1024 * 1024

F32 = jnp.float32
BF16 = jnp.bfloat16


def _rms(x, w):
    ms = jnp.mean(x * x, axis=-1, keepdims=True)
    return x * lax.rsqrt(ms + EPS) * w


def _gelu_tanh(x):
    c = math.sqrt(2.0 / math.pi)
    return x * (0.5 * (1.0 + jnp.tanh(c * (x + 0.044715 * (x * x * x)))))


def _silu(x):
    h = 0.5 * x
    return h + h * jnp.tanh(h)


def _inproj_kernel(x_ref, nw_ref, w_ref, z_ref, h_ref):
    @pl.when(pl.program_id(1) == 0)
    def _():
        h_ref[...] = _rms(x_ref[...], nw_ref[...]).astype(BF16)

    res = jnp.dot(h_ref[...], w_ref[...], preferred_element_type=F32)
    for s in range(z_ref.shape[0]):
        z_ref[s] = res[:, s * LANES:(s + 1) * LANES].astype(z_ref.dtype)


def _in_proj(x, norm_w, w_bf16, *, tm=1024, tn=IN_PROJ_TILE):
    t = x.shape[0]
    return pl.pallas_call(
        _inproj_kernel,
        grid=(t // tm, IN_COLS // tn),
        in_specs=[
            pl.BlockSpec((tm, D_MODEL), lambda i, j: (i, 0)),
            pl.BlockSpec((1, D_MODEL), lambda i, j: (0, 0)),
            pl.BlockSpec((D_MODEL, tn), lambda i, j: (0, j)),
        ],
        out_specs=pl.BlockSpec((tn // LANES, tm, LANES), lambda i, j: (j, i, 0)),
        out_shape=jax.ShapeDtypeStruct((IN_SLABS, t, LANES), BF16),
        scratch_shapes=[pltpu.VMEM((tm, D_MODEL), BF16)],
        compiler_params=pltpu.CompilerParams(
            dimension_semantics=("parallel", "arbitrary"),
            vmem_limit_bytes=VMEM_LIMIT),
        name="in_proj",
    )(x, norm_w.reshape(1, D_MODEL), w_bf16)


def _ret_kernel(q_ref, k_ref, v_ref, g_ref, cos_ref, sin_ref, lgf_ref, lgb_ref, gnw_ref,
                o_ref, lhs_ref, kt_ref, kvf_ref, rb_ref):
    seq = q_ref.shape[1]
    n_chunks = seq // CHUNK
    lgf = lgf_ref[0, 0:1, :]
    lgb = lgb_ref[0, 0:1, :]
    row = lax.broadcasted_iota(jnp.int32, (CHUNK, CHUNK), 0).astype(F32)
    col = lax.broadcasted_iota(jnp.int32, (CHUNK, CHUNK), 1).astype(F32)
    diff = row - col
    dcomb = jnp.where(diff >= 0, jnp.exp(lgf * diff), jnp.exp(lgb * (-diff)))
    xi_f = jnp.exp(lgf * (row + 1.0))
    xi_b = jnp.exp(lgb * (CHUNK - row))
    zeta_f = jnp.exp(lgf * (CHUNK - 1.0 - col))
    zeta_b = jnp.exp(lgb * col)
    decay_f = jnp.exp(lgf * CHUNK)
    decay_b = jnp.exp(lgb * CHUNK)
    scale = HEAD_DIM ** -0.5

    def rows(c):
        return pl.ds(pl.multiple_of(c * CHUNK, CHUNK), CHUNK)

    def prep(i, state):
        c = n_chunks - 1 - i
        sl = rows(c)
        q = q_ref[0, sl, :].astype(F32)
        k = k_ref[0, sl, :].astype(F32)
        cos = cos_ref[sl, :]
        sin = sin_ref[sl, :]
        qr = (q * cos + pltpu.roll(q, HEAD_DIM // 2, 1) * sin) * scale
        kr = k * cos + pltpu.roll(k, HEAD_DIM // 2, 1) * sin
        kt = kr.T
        lhs_ref[sl, :] = jnp.concatenate([qr, qr * xi_f, qr * xi_b], axis=1).astype(BF16)
        kt_ref[c] = kt.astype(BF16)
        ktz = jnp.concatenate([kt * zeta_f, kt * zeta_b], axis=0).astype(BF16)
        kv = jnp.dot(ktz, v_ref[0, sl, :], preferred_element_type=F32)
        kvf_ref[c] = kv[:HEAD_DIM]
        rb_ref[c] = state.astype(BF16)
        return state * decay_b + kv[HEAD_DIM:]

    lax.fori_loop(0, n_chunks, prep, jnp.zeros((HEAD_DIM, HEAD_DIM), F32), unroll=4)

    gnw = gnw_ref[0, 0:1, :]

    def fwd(c, state):
        sl = rows(c)
        s = jnp.dot(lhs_ref[sl, 0:HEAD_DIM], kt_ref[c], preferred_element_type=F32) * dcomb
        lhs = jnp.concatenate([s.astype(BF16), lhs_ref[sl, HEAD_DIM:]], axis=1)
        rhs = jnp.concatenate([v_ref[0, sl, :], state.astype(BF16), rb_ref[c]], axis=0)
        o = jnp.dot(lhs, rhs, preferred_element_type=F32)
        mu = jnp.mean(o, axis=-1, keepdims=True)
        d = o - mu
        var = jnp.mean(d * d, axis=-1, keepdims=True)
        on = d * lax.rsqrt(var + EPS) * gnw
        g = g_ref[0, sl, :].astype(F32)
        o_ref[0, sl, :] = (_silu(g) * on).astype(o_ref.dtype)
        return state * decay_f + kvf_ref[c]

    lax.fori_loop(0, n_chunks, fwd, jnp.zeros((HEAD_DIM, HEAD_DIM), F32), unroll=8)


def _retention(z, cos, sin, lgf, lgb, gnw, *, batch, seq):
    t = batch * seq
    n_chunks = seq // CHUNK

    def zspec(base):
        return pl.BlockSpec((1, seq, LANES), lambda b, h: (base + h, b, 0))

    table = pl.BlockSpec((seq, LANES), lambda b, h: (0, 0), pipeline_mode=pl.Buffered(1))
    per_head = pl.BlockSpec((1, 8, LANES), lambda b, h: (h, 0, 0))
    return pl.pallas_call(
        _ret_kernel,
        grid=(batch, HEADS),
        in_specs=[zspec(0), zspec(HEADS), zspec(2 * HEADS), zspec(3 * HEADS),
                  table, table, per_head, per_head, per_head],
        out_specs=pl.BlockSpec((1, seq, LANES), lambda b, h: (h, b, 0)),
        out_shape=jax.ShapeDtypeStruct((HEADS, t, LANES), BF16),
        scratch_shapes=[
            pltpu.VMEM((seq, 3 * HEAD_DIM), BF16),
            pltpu.VMEM((n_chunks, HEAD_DIM, CHUNK), BF16),
            pltpu.VMEM((n_chunks, HEAD_DIM, HEAD_DIM), F32),
            pltpu.VMEM((n_chunks, HEAD_DIM, HEAD_DIM), BF16),
        ],
        compiler_params=pltpu.CompilerParams(
            dimension_semantics=("parallel", "parallel"),
            vmem_limit_bytes=VMEM_LIMIT),
        name="retention",
    )(z, z, z, z, cos, sin, lgf, lgb, gnw)


def _sgu_kernel(zu_ref, zv_ref, nw_ref, ws_ref, b_ref, o_ref):
    rows = zu_ref.shape[1]
    n_chunks = rows // CHUNK
    v = _gelu_tanh(zv_ref[...].astype(F32))
    ms = jnp.sum(jnp.sum(v * v, axis=0), axis=-1, keepdims=True) * (1.0 / SGU_WIDTH)
    inv = lax.rsqrt(ms + EPS)
    for g in range(GROUPS):
        vn = (v[g] * inv * nw_ref[g, 0:1, :]).astype(BF16)
        rhs = jnp.concatenate(
            [vn[c * CHUNK:(c + 1) * CHUNK] for c in range(n_chunks)], axis=1)
        sp = jnp.dot(ws_ref[g], rhs, preferred_element_type=F32)
        u = _gelu_tanh(zu_ref[g].astype(F32))
        bias = b_ref[g]
        for c in range(n_chunks):
            sl = slice(c * CHUNK, (c + 1) * CHUNK)
            o_ref[g, sl, :] = (u[sl] * (sp[:, sl] + bias)).astype(o_ref.dtype)


def _sgu(z, norm_w, ws_bf16, bias, *, rows=512):
    t = z.shape[1]
    u_base = 4 * HEADS // GROUPS
    return pl.pallas_call(
        _sgu_kernel,
        grid=(t // rows,),
        in_specs=[
            pl.BlockSpec((GROUPS, rows, LANES), lambda i: (u_base, i, 0)),
            pl.BlockSpec((GROUPS, rows, LANES), lambda i: (u_base + 1, i, 0)),
            pl.BlockSpec((GROUPS, 8, LANES), lambda i: (0, 0, 0)),
            pl.BlockSpec((GROUPS, CHUNK, CHUNK), lambda i: (0, 0, 0)),
            pl.BlockSpec((GROUPS, CHUNK, LANES), lambda i: (0, 0, 0)),
        ],
        out_specs=pl.BlockSpec((GROUPS, rows, LANES), lambda i: (0, i, 0)),
        out_shape=jax.ShapeDtypeStruct((GROUPS, t, LANES), BF16),
        compiler_params=pltpu.CompilerParams(
            dimension_semantics=("parallel",),
            vmem_limit_bytes=VMEM_LIMIT),
        name="sgu",
    )(z, z, norm_w, ws_bf16, bias)


def _outproj_kernel(ret_ref, sgu_ref, x_ref, w_ref, nw_ref, x2_ref, h2_ref):
    mix = jnp.concatenate([ret_ref[s] for s in range(HEADS)]
                          + [sgu_ref[s] for s in range(GROUPS)], axis=1)
    x2 = x_ref[...] + jnp.dot(mix, w_ref[...], preferred_element_type=F32)
    x2_ref[...] = x2
    h2_ref[...] = _rms(x2, nw_ref[...]).astype(BF16)


def _out_proj(ret, sgu, x, w_bf16, norm_w, *, tm=512):
    t = x.shape[0]
    return pl.pallas_call(
        _outproj_kernel,
        grid=(t // tm,),
        in_specs=[
            pl.BlockSpec((HEADS, tm, LANES), lambda i: (0, i, 0)),
            pl.BlockSpec((GROUPS, tm, LANES), lambda i: (0, i, 0)),
            pl.BlockSpec((tm, D_MODEL), lambda i: (i, 0)),
            pl.BlockSpec((D_MODEL, D_MODEL), lambda i: (0, 0)),
            pl.BlockSpec((1, D_MODEL), lambda i: (0, 0)),
        ],
        out_specs=[pl.BlockSpec((tm, D_MODEL), lambda i: (i, 0)),
                   pl.BlockSpec((tm, D_MODEL), lambda i: (i, 0))],
        out_shape=[jax.ShapeDtypeStruct((t, D_MODEL), F32),
                   jax.ShapeDtypeStruct((t, D_MODEL), BF16)],
        compiler_params=pltpu.CompilerParams(
            dimension_semantics=("parallel",),
            vmem_limit_bytes=VMEM_LIMIT),
        name="out_proj",
    )(ret, sgu, x, w_bf16, norm_w.reshape(1, D_MODEL))


def _ffn_up_kernel(hp_ref, h_ref, hn_ref, wa_ref, wb_ref, cwa_ref, cwb_ref, cba_ref, cbb_ref,
                   o_ref, lhs_ref, u0_ref, u1_ref, *, n_j, tiles_per_seq, row_block):
    s = pl.program_id(0)
    last = pl.num_programs(0) - 2
    sm = jnp.minimum(s, last)
    i = sm // n_j
    j = sm % n_j
    tm = h_ref.shape[0]
    tf = wa_ref.shape[1]
    halo = hp_ref.shape[0]

    @pl.when(s == 0)
    def _():
        u1_ref[...] = jnp.zeros_like(u1_ref)

    @pl.when((j == 0) & (s <= last))
    def _():
        pos = i % tiles_per_seq
        hp = hp_ref[...]
        hn = hn_ref[...]
        lhs_ref[0:halo, :] = jnp.where(pos == 0, jnp.zeros_like(hp), hp)
        lhs_ref[halo:halo + tm, :] = h_ref[...]
        lhs_ref[halo + tm:, :] = jnp.where(pos == tiles_per_seq - 1, jnp.zeros_like(hn), hn)

    def step(uw_ref, ur_ref):
        def conv(r, c0, cw_ref, cb_ref):
            n = row_block + 2 * SUBLANES
            u = ur_ref[halo - SUBLANES + r:halo + SUBLANES + r + row_block, c0:c0 + tf]
            prev = pltpu.roll(u, 1, 0)[SUBLANES:SUBLANES + row_block]
            nxt = pltpu.roll(u, n - 1, 0)[SUBLANES:SUBLANES + row_block]
            cur = u[SUBLANES:SUBLANES + row_block]
            return (prev * cw_ref[0:1, :] + cur * cw_ref[1:2, :] + nxt * cw_ref[2:3, :]
                    + cb_ref[...])

        def finish_rows(r):
            a = conv(r, 0, cwa_ref, cba_ref)
            b = conv(r, tf, cwb_ref, cbb_ref)
            o_ref[r:r + row_block, :] = (_silu(a) * b).astype(o_ref.dtype)

        for r in range(0, tm, row_block):
            finish_rows(r)
        lhs = lhs_ref[...]
        uw_ref[:, 0:tf] = jnp.dot(lhs, wa_ref[...], preferred_element_type=F32)
        uw_ref[:, tf:] = jnp.dot(lhs, wb_ref[...], preferred_element_type=F32)

    @pl.when(s % 2 == 0)
    def _():
        step(u0_ref, u1_ref)

    @pl.when(s % 2 == 1)
    def _():
        step(u1_ref, u0_ref)


def _ffn_up(h2, w_up_bf16, conv_w, conv_b, *, seq, tm=1024, tf=FFN_TILE, row_block=32):
    t = h2.shape[0]
    halo = BF16_ROWS
    n_j = D_FF // tf
    n_mm = (t // tm) * n_j
    halo_blocks = tm // halo
    last_halo_block = t // halo - 1

    def mm(s):
        return jnp.minimum(s, n_mm - 1)

    def ep(s):
        return jnp.maximum(s - 1, 0)

    kernel = functools.partial(_ffn_up_kernel, n_j=n_j, tiles_per_seq=seq // tm,
                               row_block=row_block)
    return pl.pallas_call(
        kernel,
        grid=(n_mm + 1,),
        in_specs=[
            pl.BlockSpec((halo, D_MODEL),
                         lambda s: (jnp.maximum((mm(s) // n_j) * halo_blocks - 1, 0), 0)),
            pl.BlockSpec((tm, D_MODEL), lambda s: (mm(s) // n_j, 0)),
            pl.BlockSpec((halo, D_MODEL),
                         lambda s: (jnp.minimum((mm(s) // n_j + 1) * halo_blocks, last_halo_block), 0)),
            pl.BlockSpec((D_MODEL, tf), lambda s: (0, mm(s) % n_j)),
            pl.BlockSpec((D_MODEL, tf), lambda s: (0, n_j + mm(s) % n_j)),
            pl.BlockSpec((3, tf), lambda s: (0, ep(s) % n_j)),
            pl.BlockSpec((3, tf), lambda s: (0, n_j + ep(s) % n_j)),
            pl.BlockSpec((1, tf), lambda s: (0, ep(s) % n_j)),
            pl.BlockSpec((1, tf), lambda s: (0, n_j + ep(s) % n_j)),
        ],
        out_specs=pl.BlockSpec((None, tm, tf), lambda s: (ep(s) % n_j, ep(s) // n_j, 0)),
        out_shape=jax.ShapeDtypeStruct((n_j, t, tf), BF16),
        scratch_shapes=[pltpu.VMEM((tm + 2 * halo, D_MODEL), BF16),
                        pltpu.VMEM((tm + 2 * halo, 2 * tf), F32),
                        pltpu.VMEM((tm + 2 * halo, 2 * tf), F32)],
        compiler_params=pltpu.CompilerParams(
            dimension_semantics=("arbitrary",),
            vmem_limit_bytes=VMEM_LIMIT),
        name="ffn_up",
    )(h2, h2, h2, w_up_bf16, w_up_bf16, conv_w, conv_w,
      conv_b.reshape(1, 2 * D_FF), conv_b.reshape(1, 2 * D_FF))


def _ffn_down_kernel(act_ref, w_ref, x2_ref, fw_ref, o_ref):
    act = jnp.concatenate([act_ref[j] for j in range(act_ref.shape[0])], axis=1)
    y = x2_ref[...] + jnp.dot(act, w_ref[...], preferred_element_type=F32)
    o_ref[...] = _rms(y, fw_ref[...])


def _ffn_down(act, w_down_bf16, x2, final_w, *, tm=512):
    n_j, t, tf = act.shape
    return pl.pallas_call(
        _ffn_down_kernel,
        grid=(t // tm,),
        in_specs=[
            pl.BlockSpec((n_j, tm, tf), lambda i: (0, i, 0)),
            pl.BlockSpec((D_FF, D_MODEL), lambda i: (0, 0), pipeline_mode=pl.Buffered(1)),
            pl.BlockSpec((tm, D_MODEL), lambda i: (i, 0)),
            pl.BlockSpec((1, D_MODEL), lambda i: (0, 0)),
        ],
        out_specs=pl.BlockSpec((tm, D_MODEL), lambda i: (i, 0)),
        out_shape=jax.ShapeDtypeStruct((t, D_MODEL), F32),
        compiler_params=pltpu.CompilerParams(
            dimension_semantics=("parallel",),
            vmem_limit_bytes=VMEM_LIMIT),
        name="ffn_down",
    )(act, w_down_bf16, x2, final_w.reshape(1, D_MODEL))


def _ffn(h2, x2, w_up_bf16, conv_w, conv_b, w_down_bf16, final_w, *, seq):
    act = _ffn_up(h2, w_up_bf16, conv_w, conv_b, seq=seq)
    return _ffn_down(act, w_down_bf16, x2, final_w)


def _rope_tables(seq):
    inv_freq = ROPE_BASE ** (-np.arange(0, HEAD_DIM, 2, dtype=np.float64) / HEAD_DIM)
    ang = np.arange(seq, dtype=np.float64)[:, None] * inv_freq[None, :]
    cos = np.cos(ang).astype(np.float32)
    sin = np.sin(ang).astype(np.float32)
    return (jnp.asarray(np.concatenate([cos, cos], axis=-1)),
            jnp.asarray(np.concatenate([-sin, sin], axis=-1)))


def _col_tiles(w, tile):
    k, n = w.shape
    return w.astype(BF16).reshape(k, n // tile, tile).transpose(1, 0, 2)


def _per_slab(vec, slabs):
    return jnp.broadcast_to(vec.reshape(slabs, 1, LANES), (slabs, 8, LANES))


def _trunk(x, p):
    batch, seq, _ = x.shape
    xt = x.reshape(batch * seq, D_MODEL)
    z = _in_proj(xt, p["norm1_w"], p["w_in"])
    ret = _retention(z, p["cos"], p["sin"], p["lgf"], p["lgb"], p["gnw"], batch=batch, seq=seq)
    sgu = _sgu(z, p["sgu_nw"], p["w_s"], p["sgu_b"])
    x2, h2 = _out_proj(ret, sgu, xt, p["w_out"], p["norm2_w"])
    y = _ffn(h2, x2, p["w_up"], p["conv_w"], p["conv_b"], p["w_down"], p["final_w"], seq=seq)
    return y.reshape(batch, seq, D_MODEL)


def kernel(x_prompt, x_sample, norm1_w, w_in, ret_log_decay_fwd, ret_log_decay_bwd, ret_gn_w, sgu_norm_w, sgu_w_s, sgu_b, w_out, norm2_w, w_up, conv_w, conv_b, w_down, final_norm_w):
    max_seq = max(x_prompt.shape[1], x_sample.shape[1])
    cos, sin = _rope_tables(max_seq)
    p = {
        "norm1_w": norm1_w[0],
        "w_in": w_in[0].astype(BF16),
        "cos": cos,
        "sin": sin,
        "lgf": jnp.broadcast_to(ret_log_decay_fwd[0].astype(F32)[:, None, None], (HEADS, 8, LANES)),
        "lgb": jnp.broadcast_to(ret_log_decay_bwd[0].astype(F32)[:, None, None], (HEADS, 8, LANES)),
        "gnw": _per_slab(ret_gn_w[0], HEADS),
        "sgu_nw": _per_slab(sgu_norm_w[0], GROUPS),
        "w_s": sgu_w_s[0].astype(BF16),
        "sgu_b": jnp.broadcast_to(sgu_b[0][:, :, None], (GROUPS, CHUNK, LANES)),
        "w_out": w_out[0].astype(BF16),
        "norm2_w": norm2_w[0],
        "w_up": w_up[0].astype(BF16),
        "conv_w": conv_w[0],
        "conv_b": conv_b[0],
        "w_down": w_down[0].astype(BF16),
        "final_w": final_norm_w,
    }
    return (_trunk(x_prompt, p), _trunk(x_sample, p))
```

```python
import functools
import math

import jax
import jax.numpy as jnp
import numpy as np
from jax import lax
from jax.experimental import pallas as pl
from jax.experimental.pallas import tpu as pltpu

D_MODEL = 2048
CHUNK = 128
HEADS = 8
HEAD_DIM = 128
RET_WIDTH = HEADS * HEAD_DIM
GROUPS = 8
SGU_WIDTH = GROUPS * CHUNK
IN_COLS = 4 * RET_WIDTH + 2 * SGU_WIDTH
IN_SLABS = IN_COLS // 128
D_FF = 5632
ROPE_BASE = 10000.0
EPS = 1e-6
LANES = 128
SUBLANES = 8
BF16_ROWS = 16
IN_PROJ_TILE = 2048
FFN_TILE = 512
FFN_ROWS = 1024
VMEM_LIMIT = 56 * 1024 * 1024

F32 = jnp.float32
BF16 = jnp.bfloat16


def _rms(x, w):
    ms = jnp.mean(x * x, axis=-1, keepdims=True)
    return x * lax.rsqrt(ms + EPS) * w


def _gelu_tanh(x):
    c = math.sqrt(2.0 / math.pi)
    h = 0.5 * x
    return h + h * jnp.tanh(x * (c + (c * 0.044715) * (x * x)))


def _silu(x):
    h = 0.5 * x
    return h + h * jnp.tanh(h)


def _inproj_kernel(x_ref, nw_ref, w_ref, z_ref, h_ref):
    @pl.when(pl.program_id(1) == 0)
    def _():
        h_ref[...] = _rms(x_ref[...], nw_ref[...]).astype(BF16)

    res = jnp.dot(h_ref[...], w_ref[...], preferred_element_type=F32)
    for s in range(z_ref.shape[0]):
        z_ref[s] = res[:, s * LANES:(s + 1) * LANES].astype(z_ref.dtype)


def _in_proj(x, norm_w, w_bf16, *, tm=1024, tn=IN_PROJ_TILE):
    t = x.shape[0]
    return pl.pallas_call(
        _inproj_kernel,
        grid=(t // tm, IN_COLS // tn),
        in_specs=[
            pl.BlockSpec((tm, D_MODEL), lambda i, j: (i, 0)),
            pl.BlockSpec((1, D_MODEL), lambda i, j: (0, 0)),
            pl.BlockSpec((D_MODEL, tn), lambda i, j: (0, j)),
        ],
        out_specs=pl.BlockSpec((tn // LANES, tm, LANES), lambda i, j: (j, i, 0)),
        out_shape=jax.ShapeDtypeStruct((IN_SLABS, t, LANES), BF16),
        scratch_shapes=[pltpu.VMEM((tm, D_MODEL), BF16)],
        compiler_params=pltpu.CompilerParams(
            dimension_semantics=("parallel", "arbitrary"),
            vmem_limit_bytes=VMEM_LIMIT),
        name="in_proj",
    )(x, norm_w.reshape(1, D_MODEL), w_bf16)


def _ret_kernel(q_ref, k_ref, v_ref, g_ref, cos_ref, sin_ref, lgf_ref, lgb_ref, gnw_ref,
                o_ref, lhs_ref, kt_ref, kvf_ref, rb_ref):
    seq = q_ref.shape[1]
    n_chunks = seq // CHUNK
    lgf = lgf_ref[0, 0:1, :]
    lgb = lgb_ref[0, 0:1, :]
    row = lax.broadcasted_iota(jnp.int32, (CHUNK, CHUNK), 0).astype(F32)
    col = lax.broadcasted_iota(jnp.int32, (CHUNK, CHUNK), 1).astype(F32)
    diff = row - col
    scale = HEAD_DIM ** -0.5
    dcomb = jnp.where(diff >= 0, jnp.exp(lgf * diff), jnp.exp(lgb * (-diff))) * scale
    xi_f = jnp.exp(lgf * (row + 1.0)) * scale
    xi_b = jnp.exp(lgb * (CHUNK - row)) * scale
    zeta_f = jnp.exp(lgf * (CHUNK - 1.0 - col))
    zeta_b = jnp.exp(lgb * col)
    decay_f = jnp.exp(lgf * CHUNK)
    decay_b = jnp.exp(lgb * CHUNK)

    def rows(c):
        return pl.ds(pl.multiple_of(c * CHUNK, CHUNK), CHUNK)

    def prep(i, state):
        c = n_chunks - 1 - i
        sl = rows(c)
        q = q_ref[0, sl, :].astype(F32)
        k = k_ref[0, sl, :].astype(F32)
        cos = cos_ref[sl, :]
        sin = sin_ref[sl, :]
        qr = q * cos + pltpu.roll(q, HEAD_DIM // 2, 1) * sin
        kr = k * cos + pltpu.roll(k, HEAD_DIM // 2, 1) * sin
        kt = kr.T
        lhs_ref[sl, :] = jnp.concatenate([qr, qr * xi_f, qr * xi_b], axis=1).astype(BF16)
        kt_ref[c] = kt.astype(BF16)
        ktz = jnp.concatenate([kt * zeta_f, kt * zeta_b], axis=0).astype(BF16)
        kv = jnp.dot(ktz, v_ref[0, sl, :], preferred_element_type=F32)
        kvf_ref[c] = kv[:HEAD_DIM]
        rb_ref[c] = state.astype(BF16)
        return state * decay_b + kv[HEAD_DIM:]

    lax.fori_loop(0, n_chunks, prep, jnp.zeros((HEAD_DIM, HEAD_DIM), F32), unroll=4)

    gnw = gnw_ref[0, 0:1, :]

    def fwd(c, state):
        sl = rows(c)
        s = jnp.dot(lhs_ref[sl, 0:HEAD_DIM], kt_ref[c], preferred_element_type=F32) * dcomb
        lhs = jnp.concatenate([s.astype(BF16), lhs_ref[sl, HEAD_DIM:]], axis=1)
        rhs = jnp.concatenate([v_ref[0, sl, :], state.astype(BF16), rb_ref[c]], axis=0)
        o = jnp.dot(lhs, rhs, preferred_element_type=F32)
        mu = jnp.mean(o, axis=-1, keepdims=True)
        d = o - mu
        var = jnp.mean(d * d, axis=-1, keepdims=True)
        on = d * lax.rsqrt(var + EPS) * gnw
        g = g_ref[0, sl, :].astype(F32)
        o_ref[0, sl, :] = (_silu(g) * on).astype(o_ref.dtype)
        return state * decay_f + kvf_ref[c]

    lax.fori_loop(0, n_chunks, fwd, jnp.zeros((HEAD_DIM, HEAD_DIM), F32), unroll=8)


def _retention(z, cos, sin, lgf, lgb, gnw, *, batch, seq):
    t = batch * seq
    n_chunks = seq // CHUNK

    def zspec(base):
        return pl.BlockSpec((1, seq, LANES), lambda b, h: (base + h, b, 0))

    table = pl.BlockSpec((seq, LANES), lambda b, h: (0, 0), pipeline_mode=pl.Buffered(1))
    per_head = pl.BlockSpec((1, 8, LANES), lambda b, h: (h, 0, 0))
    return pl.pallas_call(
        _ret_kernel,
        grid=(batch, HEADS),
        in_specs=[zspec(0), zspec(HEADS), zspec(2 * HEADS), zspec(3 * HEADS),
                  table, table, per_head, per_head, per_head],
        out_specs=pl.BlockSpec((1, seq, LANES), lambda b, h: (h, b, 0)),
        out_shape=jax.ShapeDtypeStruct((HEADS, t, LANES), BF16),
        scratch_shapes=[
            pltpu.VMEM((seq, 3 * HEAD_DIM), BF16),
            pltpu.VMEM((n_chunks, HEAD_DIM, CHUNK), BF16),
            pltpu.VMEM((n_chunks, HEAD_DIM, HEAD_DIM), F32),
            pltpu.VMEM((n_chunks, HEAD_DIM, HEAD_DIM), BF16),
        ],
        compiler_params=pltpu.CompilerParams(
            dimension_semantics=("parallel", "parallel"),
            vmem_limit_bytes=VMEM_LIMIT),
        name="retention",
    )(z, z, z, z, cos, sin, lgf, lgb, gnw)


def _sgu_kernel(zu_ref, zv_ref, nw_ref, ws_ref, b_ref, o_ref):
    rows = zu_ref.shape[1]
    n_chunks = rows // CHUNK
    v = _gelu_tanh(zv_ref[...].astype(F32))
    ms = jnp.sum(jnp.sum(v * v, axis=0), axis=-1, keepdims=True) * (1.0 / SGU_WIDTH)
    inv = lax.rsqrt(ms + EPS)
    for g in range(GROUPS):
        vn = (v[g] * inv * nw_ref[g, 0:1, :]).astype(BF16)
        rhs = jnp.concatenate(
            [vn[c * CHUNK:(c + 1) * CHUNK] for c in range(n_chunks)], axis=1)
        sp = jnp.dot(ws_ref[g], rhs, preferred_element_type=F32)
        u = _gelu_tanh(zu_ref[g].astype(F32))
        bias = b_ref[g]
        for c in range(n_chunks):
            sl = slice(c * CHUNK, (c + 1) * CHUNK)
            o_ref[g, sl, :] = (u[sl] * (sp[:, sl] + bias)).astype(o_ref.dtype)


def _sgu(z, norm_w, ws_bf16, bias, *, rows=512):
    t = z.shape[1]
    u_base = 4 * HEADS // GROUPS
    return pl.pallas_call(
        _sgu_kernel,
        grid=(t // rows,),
        in_specs=[
            pl.BlockSpec((GROUPS, rows, LANES), lambda i: (u_base, i, 0)),
            pl.BlockSpec((GROUPS, rows, LANES), lambda i: (u_base + 1, i, 0)),
            pl.BlockSpec((GROUPS, 8, LANES), lambda i: (0, 0, 0)),
            pl.BlockSpec((GROUPS, CHUNK, CHUNK), lambda i: (0, 0, 0)),
            pl.BlockSpec((GROUPS, CHUNK, LANES), lambda i: (0, 0, 0)),
        ],
        out_specs=pl.BlockSpec((GROUPS, rows, LANES), lambda i: (0, i, 0)),
        out_shape=jax.ShapeDtypeStruct((GROUPS, t, LANES), BF16),
        compiler_params=pltpu.CompilerParams(
            dimension_semantics=("parallel",),
            vmem_limit_bytes=VMEM_LIMIT),
        name="sgu",
    )(z, z, norm_w, ws_bf16, bias)


def _outproj_kernel(ret_ref, sgu_ref, x_ref, w_ref, nw_ref, x2_ref, h2_ref):
    mix = jnp.concatenate([ret_ref[s] for s in range(HEADS)]
                          + [sgu_ref[s] for s in range(GROUPS)], axis=1)
    x2 = x_ref[...] + jnp.dot(mix, w_ref[...], preferred_element_type=F32)
    x2_ref[...] = x2
    h2_ref[...] = _rms(x2, nw_ref[...]).astype(BF16)


def _out_proj(ret, sgu, x, w_bf16, norm_w, *, tm=512):
    t = x.shape[0]
    return pl.pallas_call(
        _outproj_kernel,
        grid=(t // tm,),
        in_specs=[
            pl.BlockSpec((HEADS, tm, LANES), lambda i: (0, i, 0)),
            pl.BlockSpec((GROUPS, tm, LANES), lambda i: (0, i, 0)),
            pl.BlockSpec((tm, D_MODEL), lambda i: (i, 0)),
            pl.BlockSpec((D_MODEL, D_MODEL), lambda i: (0, 0)),
            pl.BlockSpec((1, D_MODEL), lambda i: (0, 0)),
        ],
        out_specs=[pl.BlockSpec((tm, D_MODEL), lambda i: (i, 0)),
                   pl.BlockSpec((tm, D_MODEL), lambda i: (i, 0))],
        out_shape=[jax.ShapeDtypeStruct((t, D_MODEL), F32),
                   jax.ShapeDtypeStruct((t, D_MODEL), BF16)],
        compiler_params=pltpu.CompilerParams(
            dimension_semantics=("parallel",),
            vmem_limit_bytes=VMEM_LIMIT),
        name="out_proj",
    )(ret, sgu, x, w_bf16, norm_w.reshape(1, D_MODEL))


def _ffn_up_kernel(hp_ref, h_ref, hn_ref, wa_ref, wb_ref, cwa_ref, cwb_ref, cba_ref, cbb_ref,
                   o_ref, lhs_ref, u0_ref, u1_ref, *, n_j, tiles_per_seq, row_block, col_block):
    s = pl.program_id(0)
    last = pl.num_programs(0) - 2
    sm = jnp.minimum(s, last)
    i = sm // n_j
    j = sm % n_j
    tm = h_ref.shape[0]
    tf = wa_ref.shape[1]
    halo = hp_ref.shape[0]

    @pl.when(s == 0)
    def _():
        u1_ref[...] = jnp.zeros_like(u1_ref)

    @pl.when((j == 0) & (s <= last))
    def _():
        pos = i % tiles_per_seq
        hp = hp_ref[...]
        hn = hn_ref[...]
        lhs_ref[0:halo, :] = jnp.where(pos == 0, jnp.zeros_like(hp), hp)
        lhs_ref[halo:halo + tm, :] = h_ref[...]
        lhs_ref[halo + tm:, :] = jnp.where(pos == tiles_per_seq - 1, jnp.zeros_like(hn), hn)

    def step(uw_ref, ur_ref):
        def conv(r, c, base, cw_ref, cb_ref):
            n = row_block + 2 * SUBLANES
            u = ur_ref[halo - SUBLANES + r:halo + SUBLANES + r + row_block,
                       base + c:base + c + col_block]
            prev = pltpu.roll(u, 1, 0)[SUBLANES:SUBLANES + row_block]
            nxt = pltpu.roll(u, n - 1, 0)[SUBLANES:SUBLANES + row_block]
            cur = u[SUBLANES:SUBLANES + row_block]
            cols = slice(c, c + col_block)
            return (prev * cw_ref[0:1, cols] + cur * cw_ref[1:2, cols] + nxt * cw_ref[2:3, cols]
                    + cb_ref[:, cols])

        def finish_block(r, c):
            a = conv(r, c, 0, cwa_ref, cba_ref)
            b = conv(r, c, tf, cwb_ref, cbb_ref)
            o_ref[r:r + row_block, c:c + col_block] = (_silu(a) * b).astype(o_ref.dtype)

        for r in range(0, tm, row_block):
            for c in range(0, tf, col_block):
                finish_block(r, c)
        lhs = lhs_ref[...]
        uw_ref[:, 0:tf] = jnp.dot(lhs, wa_ref[...], preferred_element_type=F32)
        uw_ref[:, tf:] = jnp.dot(lhs, wb_ref[...], preferred_element_type=F32)

    @pl.when(s % 2 == 0)
    def _():
        step(u0_ref, u1_ref)

    @pl.when(s % 2 == 1)
    def _():
        step(u1_ref, u0_ref)


def _ffn_up(h2, w_up_bf16, conv_w, conv_b, *, seq, tm=FFN_ROWS, tf=FFN_TILE, row_block=16,
            col_block=128):
    t = h2.shape[0]
    halo = BF16_ROWS
    n_j = D_FF // tf
    n_mm = (t // tm) * n_j
    halo_blocks = tm // halo
    last_halo_block = t // halo - 1

    def mm(s):
        return jnp.minimum(s, n_mm - 1)

    def ep(s):
        return jnp.maximum(s - 1, 0)

    kernel = functools.partial(_ffn_up_kernel, n_j=n_j, tiles_per_seq=seq // tm,
                               row_block=row_block, col_block=col_block)
    return pl.pallas_call(
        kernel,
        grid=(n_mm + 1,),
        in_specs=[
            pl.BlockSpec((halo, D_MODEL),
                         lambda s: (jnp.maximum((mm(s) // n_j) * halo_blocks - 1, 0), 0)),
            pl.BlockSpec((tm, D_MODEL), lambda s: (mm(s) // n_j, 0)),
            pl.BlockSpec((halo, D_MODEL),
                         lambda s: (jnp.minimum((mm(s) // n_j + 1) * halo_blocks, last_halo_block), 0)),
            pl.BlockSpec((D_MODEL, tf), lambda s: (0, mm(s) % n_j)),
            pl.BlockSpec((D_MODEL, tf), lambda s: (0, n_j + mm(s) % n_j)),
            pl.BlockSpec((3, tf), lambda s: (0, ep(s) % n_j)),
            pl.BlockSpec((3, tf), lambda s: (0, n_j + ep(s) % n_j)),
            pl.BlockSpec((1, tf), lambda s: (0, ep(s) % n_j)),
            pl.BlockSpec((1, tf), lambda s: (0, n_j + ep(s) % n_j)),
        ],
        out_specs=pl.BlockSpec((None, tm, tf), lambda s: (ep(s) % n_j, ep(s) // n_j, 0)),
        out_shape=jax.ShapeDtypeStruct((n_j, t, tf), BF16),
        scratch_shapes=[pltpu.VMEM((tm + 2 * halo, D_MODEL), BF16),
                        pltpu.VMEM((tm + 2 * halo, 2 * tf), F32),
                        pltpu.VMEM((tm + 2 * halo, 2 * tf), F32)],
        compiler_params=pltpu.CompilerParams(
            dimension_semantics=("arbitrary",),
            vmem_limit_bytes=VMEM_LIMIT),
        name="ffn_up",
    )(h2, h2, h2, w_up_bf16, w_up_bf16, conv_w, conv_w,
      conv_b.reshape(1, 2 * D_FF), conv_b.reshape(1, 2 * D_FF))


def _ffn_down_kernel(act_ref, w_ref, x2_ref, fw_ref, o_ref):
    act = jnp.concatenate([act_ref[j] for j in range(act_ref.shape[0])], axis=1)
    y = x2_ref[...] + jnp.dot(act, w_ref[...], preferred_element_type=F32)
    o_ref[...] = _rms(y, fw_ref[...])


def _ffn_down(act, w_down_bf16, x2, final_w, *, tm=512):
    n_j, t, tf = act.shape
    return pl.pallas_call(
        _ffn_down_kernel,
        grid=(t // tm,),
        in_specs=[
            pl.BlockSpec((n_j, tm, tf), lambda i: (0, i, 0)),
            pl.BlockSpec((D_FF, D_MODEL), lambda i: (0, 0), pipeline_mode=pl.Buffered(1)),
            pl.BlockSpec((tm, D_MODEL), lambda i: (i, 0)),
            pl.BlockSpec((1, D_MODEL), lambda i: (0, 0)),
        ],
        out_specs=pl.BlockSpec((tm, D_MODEL), lambda i: (i, 0)),
        out_shape=jax.ShapeDtypeStruct((t, D_MODEL), F32),
        compiler_params=pltpu.CompilerParams(
            dimension_semantics=("parallel",),
            vmem_limit_bytes=VMEM_LIMIT),
        name="ffn_down",
    )(act, w_down_bf16, x2, final_w.reshape(1, D_MODEL))


def _ffn(h2, x2, w_up_bf16, conv_w, conv_b, w_down_bf16, final_w, *, seq):
    act = _ffn_up(h2, w_up_bf16, conv_w, conv_b, seq=seq)
    return _ffn_down(act, w_down_bf16, x2, final_w)


def _rope_tables(seq):
    inv_freq = ROPE_BASE ** (-np.arange(0, HEAD_DIM, 2, dtype=np.float64) / HEAD_DIM)
    ang = np.arange(seq, dtype=np.float64)[:, None] * inv_freq[None, :]
    cos = np.cos(ang).astype(np.float32)
    sin = np.sin(ang).astype(np.float32)
    return (jnp.asarray(np.concatenate([cos, cos], axis=-1)),
            jnp.asarray(np.concatenate([-sin, sin], axis=-1)))


def _col_tiles(w, tile):
    k, n = w.shape
    return w.astype(BF16).reshape(k, n // tile, tile).transpose(1, 0, 2)


def _per_slab(vec, slabs):
    return jnp.broadcast_to(vec.reshape(slabs, 1, LANES), (slabs, 8, LANES))


def _trunk(x, p):
    batch, seq, _ = x.shape
    xt = x.reshape(batch * seq, D_MODEL)
    z = _in_proj(xt, p["norm1_w"], p["w_in"])
    ret = _retention(z, p["cos"], p["sin"], p["lgf"], p["lgb"], p["gnw"], batch=batch, seq=seq)
    sgu = _sgu(z, p["sgu_nw"], p["w_s"], p["sgu_b"])
    x2, h2 = _out_proj(ret, sgu, xt, p["w_out"], p["norm2_w"])
    y = _ffn(h2, x2, p["w_up"], p["conv_w"], p["conv_b"], p["w_down"], p["final_w"], seq=seq)
    return y.reshape(batch, seq, D_MODEL)


def kernel(x_prompt, x_sample, norm1_w, w_in, ret_log_decay_fwd, ret_log_decay_bwd, ret_gn_w, sgu_norm_w, sgu_w_s, sgu_b, w_out, norm2_w, w_up, conv_w, conv_b, w_down, final_norm_w):
    max_seq = max(x_prompt.shape[1], x_sample.shape[1])
    cos, sin = _rope_tables(max_seq)
    p = {
        "norm1_w": norm1_w[0],
        "w_in": w_in[0].astype(BF16),
        "cos": cos,
        "sin": sin,
        "lgf": jnp.broadcast_to(ret_log_decay_fwd[0].astype(F32)[:, None, None], (HEADS, 8, LANES)),
        "lgb": jnp.broadcast_to(ret_log_decay_bwd[0].astype(F32)[:, None, None], (HEADS, 8, LANES)),
        "gnw": _per_slab(ret_gn_w[0], HEADS),
        "sgu_nw": _per_slab(sgu_norm_w[0], GROUPS),
        "w_s": sgu_w_s[0].astype(BF16),
        "sgu_b": jnp.broadcast_to(sgu_b[0][:, :, None], (GROUPS, CHUNK, LANES)),
        "w_out": w_out[0].astype(BF16),
        "norm2_w": norm2_w[0],
        "w_up": w_up[0].astype(BF16),
        "conv_w": conv_w[0],
        "conv_b": conv_b[0],
        "w_down": w_down[0].astype(BF16),
        "final_w": final_norm_w,
    }
    return (_trunk(x_prompt, p), _trunk(x_sample, p))
```

```python
import functools
import math

import jax
import jax.numpy as jnp
import numpy as np
from jax import lax
from jax.experimental import pallas as pl
from jax.experimental.pallas import tpu as pltpu

D_MODEL = 2048
CHUNK = 128
HEADS = 8
HEAD_DIM = 128
RET_WIDTH = HEADS * HEAD_DIM
GROUPS = 8
SGU_WIDTH = GROUPS * CHUNK
IN_COLS = 4 * RET_WIDTH + 2 * SGU_WIDTH
IN_SLABS = IN_COLS // 128
D_FF = 5632
ROPE_BASE = 10000.0
EPS = 1e-6
LANES = 128
SUBLANES = 8
BF16_ROWS = 16
IN_PROJ_TILE = 2048
FFN_TILE = 512
FFN_ROWS = 1024
VMEM_LIMIT = 56 * 1024 * 1024

F32 = jnp.float32
BF16 = jnp.bfloat16


def _rms(x, w):
    ms = jnp.mean(x * x, axis=-1, keepdims=True)
    return x * lax.rsqrt(ms + EPS) * w


def _gelu_tanh(x):
    c = math.sqrt(2.0 / math.pi)
    h = 0.5 * x
    return h + h * jnp.tanh(x * (c + (c * 0.044715) * (x * x)))


def _silu(x):
    h = 0.5 * x
    return h + h * jnp.tanh(h)


def _inproj_kernel(x_ref, nw_ref, w_ref, z_ref, h_ref):
    @pl.when(pl.program_id(1) == 0)
    def _():
        h_ref[...] = _rms(x_ref[...], nw_ref[...]).astype(BF16)

    res = jnp.dot(h_ref[...], w_ref[...], preferred_element_type=F32)
    for s in range(z_ref.shape[0]):
        z_ref[s] = res[:, s * LANES:(s + 1) * LANES].astype(z_ref.dtype)


def _in_proj(x, norm_w, w_bf16, *, tm=1024, tn=IN_PROJ_TILE):
    t = x.shape[0]
    return pl.pallas_call(
        _inproj_kernel,
        grid=(t // tm, IN_COLS // tn),
        in_specs=[
            pl.BlockSpec((tm, D_MODEL), lambda i, j: (i, 0)),
            pl.BlockSpec((1, D_MODEL), lambda i, j: (0, 0)),
            pl.BlockSpec((D_MODEL, tn), lambda i, j: (0, j)),
        ],
        out_specs=pl.BlockSpec((tn // LANES, tm, LANES), lambda i, j: (j, i, 0)),
        out_shape=jax.ShapeDtypeStruct((IN_SLABS, t, LANES), BF16),
        scratch_shapes=[pltpu.VMEM((tm, D_MODEL), BF16)],
        compiler_params=pltpu.CompilerParams(
            dimension_semantics=("parallel", "arbitrary"),
            vmem_limit_bytes=VMEM_LIMIT),
        name="in_proj",
    )(x, norm_w.reshape(1, D_MODEL), w_bf16)


def _ret_kernel(q_ref, k_ref, v_ref, g_ref, cos_ref, sin_ref, lgf_ref, lgb_ref, gnw_ref,
                o_ref, lhs_ref, kt_ref, kvf_ref, rb_ref):
    seq = q_ref.shape[1]
    n_chunks = seq // CHUNK
    lgf = lgf_ref[0, 0:1, :]
    lgb = lgb_ref[0, 0:1, :]
    row = lax.broadcasted_iota(jnp.int32, (CHUNK, CHUNK), 0).astype(F32)
    col = lax.broadcasted_iota(jnp.int32, (CHUNK, CHUNK), 1).astype(F32)
    diff = row - col
    scale = HEAD_DIM ** -0.5
    dcomb = jnp.where(diff >= 0, jnp.exp(lgf * diff), jnp.exp(lgb * (-diff))) * scale
    xi_f = jnp.exp(lgf * (row + 1.0)) * scale
    xi_b = jnp.exp(lgb * (CHUNK - row)) * scale
    zeta_f = jnp.exp(lgf * (CHUNK - 1.0 - col))
    zeta_b = jnp.exp(lgb * col)
    decay_f = jnp.exp(lgf * CHUNK)
    decay_b = jnp.exp(lgb * CHUNK)

    def rows(c):
        return pl.ds(pl.multiple_of(c * CHUNK, CHUNK), CHUNK)

    def prep(i, state):
        c = n_chunks - 1 - i
        sl = rows(c)
        q = q_ref[0, sl, :].astype(F32)
        k = k_ref[0, sl, :].astype(F32)
        cos = cos_ref[sl, :]
        sin = sin_ref[sl, :]
        qr = q * cos + pltpu.roll(q, HEAD_DIM // 2, 1) * sin
        kr = k * cos + pltpu.roll(k, HEAD_DIM // 2, 1) * sin
        kt = kr.T
        lhs_ref[sl, :] = jnp.concatenate([qr, qr * xi_f, qr * xi_b], axis=1).astype(BF16)
        kt_ref[c] = kt.astype(BF16)
        ktz = jnp.concatenate([kt * zeta_f, kt * zeta_b], axis=0).astype(BF16)
        kv = jnp.dot(ktz, v_ref[0, sl, :], preferred_element_type=F32)
        kvf_ref[c] = kv[:HEAD_DIM]
        rb_ref[c] = state.astype(BF16)
        return state * decay_b + kv[HEAD_DIM:]

    lax.fori_loop(0, n_chunks, prep, jnp.zeros((HEAD_DIM, HEAD_DIM), F32), unroll=4)

    gnw = gnw_ref[0, 0:1, :]

    def fwd(c, state):
        sl = rows(c)
        s = jnp.dot(lhs_ref[sl, 0:HEAD_DIM], kt_ref[c], preferred_element_type=F32) * dcomb
        lhs = jnp.concatenate([s.astype(BF16), lhs_ref[sl, HEAD_DIM:]], axis=1)
        rhs = jnp.concatenate([v_ref[0, sl, :], state.astype(BF16), rb_ref[c]], axis=0)
        o = jnp.dot(lhs, rhs, preferred_element_type=F32)
        mu = jnp.mean(o, axis=-1, keepdims=True)
        d = o - mu
        var = jnp.mean(d * d, axis=-1, keepdims=True)
        on = d * lax.rsqrt(var + EPS) * gnw
        g = g_ref[0, sl, :].astype(F32)
        o_ref[0, sl, :] = (_silu(g) * on).astype(o_ref.dtype)
        return state * decay_f + kvf_ref[c]

    lax.fori_loop(0, n_chunks, fwd, jnp.zeros((HEAD_DIM, HEAD_DIM), F32), unroll=8)


def _retention(z, cos, sin, lgf, lgb, gnw, *, batch, seq):
    t = batch * seq
    n_chunks = seq // CHUNK

    def zspec(base):
        return pl.BlockSpec((1, seq, LANES), lambda b, h: (base + h, b, 0))

    table = pl.BlockSpec((seq, LANES), lambda b, h: (0, 0), pipeline_mode=pl.Buffered(1))
    per_head = pl.BlockSpec((1, 8, LANES), lambda b, h: (h, 0, 0))
    return pl.pallas_call(
        _ret_kernel,
        grid=(batch, HEADS),
        in_specs=[zspec(0), zspec(HEADS), zspec(2 * HEADS), zspec(3 * HEADS),
                  table, table, per_head, per_head, per_head],
        out_specs=pl.BlockSpec((1, seq, LANES), lambda b, h: (h, b, 0)),
        out_shape=jax.ShapeDtypeStruct((HEADS, t, LANES), BF16),
        scratch_shapes=[
            pltpu.VMEM((seq, 3 * HEAD_DIM), BF16),
            pltpu.VMEM((n_chunks, HEAD_DIM, CHUNK), BF16),
            pltpu.VMEM((n_chunks, HEAD_DIM, HEAD_DIM), F32),
            pltpu.VMEM((n_chunks, HEAD_DIM, HEAD_DIM), BF16),
        ],
        compiler_params=pltpu.CompilerParams(
            dimension_semantics=("parallel", "parallel"),
            vmem_limit_bytes=VMEM_LIMIT),
        name="retention",
    )(z, z, z, z, cos, sin, lgf, lgb, gnw)


def _sgu_kernel(zu_ref, zv_ref, nw_ref, ws_ref, b_ref, o_ref):
    rows = zu_ref.shape[1]
    n_chunks = rows // CHUNK
    v = _gelu_tanh(zv_ref[...].astype(F32))
    ms = jnp.sum(jnp.sum(v * v, axis=0), axis=-1, keepdims=True) * (1.0 / SGU_WIDTH)
    inv = lax.rsqrt(ms + EPS)
    for g in range(GROUPS):
        vn = (v[g] * inv * nw_ref[g, 0:1, :]).astype(BF16)
        rhs = jnp.concatenate(
            [vn[c * CHUNK:(c + 1) * CHUNK] for c in range(n_chunks)], axis=1)
        sp = jnp.dot(ws_ref[g], rhs, preferred_element_type=F32)
        u = _gelu_tanh(zu_ref[g].astype(F32))
        bias = b_ref[g]
        for c in range(n_chunks):
            sl = slice(c * CHUNK, (c + 1) * CHUNK)
            o_ref[g, sl, :] = (u[sl] * (sp[:, sl] + bias)).astype(o_ref.dtype)


def _sgu(z, norm_w, ws_bf16, bias, *, rows=512):
    t = z.shape[1]
    u_base = 4 * HEADS // GROUPS
    return pl.pallas_call(
        _sgu_kernel,
        grid=(t // rows,),
        in_specs=[
            pl.BlockSpec((GROUPS, rows, LANES), lambda i: (u_base, i, 0)),
            pl.BlockSpec((GROUPS, rows, LANES), lambda i: (u_base + 1, i, 0)),
            pl.BlockSpec((GROUPS, 8, LANES), lambda i: (0, 0, 0)),
            pl.BlockSpec((GROUPS, CHUNK, CHUNK), lambda i: (0, 0, 0)),
            pl.BlockSpec((GROUPS, CHUNK, LANES), lambda i: (0, 0, 0)),
        ],
        out_specs=pl.BlockSpec((GROUPS, rows, LANES), lambda i: (0, i, 0)),
        out_shape=jax.ShapeDtypeStruct((GROUPS, t, LANES), BF16),
        compiler_params=pltpu.CompilerParams(
            dimension_semantics=("parallel",),
            vmem_limit_bytes=VMEM_LIMIT),
        name="sgu",
    )(z, z, norm_w, ws_bf16, bias)


def _outproj_kernel(ret_ref, sgu_ref, x_ref, w_ref, nw_ref, x2_ref, h2_ref):
    mix = jnp.concatenate([ret_ref[s] for s in range(HEADS)]
                          + [sgu_ref[s] for s in range(GROUPS)], axis=1)
    x2 = x_ref[...] + jnp.dot(mix, w_ref[...], preferred_element_type=F32)
    x2_ref[...] = x2
    h2_ref[...] = _rms(x2, nw_ref[...]).astype(BF16)


def _out_proj(ret, sgu, x, w_bf16, norm_w, *, tm=512):
    t = x.shape[0]
    return pl.pallas_call(
        _outproj_kernel,
        grid=(t // tm,),
        in_specs=[
            pl.BlockSpec((HEADS, tm, LANES), lambda i: (0, i, 0)),
            pl.BlockSpec((GROUPS, tm, LANES), lambda i: (0, i, 0)),
            pl.BlockSpec((tm, D_MODEL), lambda i: (i, 0)),
            pl.BlockSpec((D_MODEL, D_MODEL), lambda i: (0, 0)),
            pl.BlockSpec((1, D_MODEL), lambda i: (0, 0)),
        ],
        out_specs=[pl.BlockSpec((tm, D_MODEL), lambda i: (i, 0)),
                   pl.BlockSpec((tm, D_MODEL), lambda i: (i, 0))],
        out_shape=[jax.ShapeDtypeStruct((t, D_MODEL), F32),
                   jax.ShapeDtypeStruct((t, D_MODEL), BF16)],
        compiler_params=pltpu.CompilerParams(
            dimension_semantics=("parallel",),
            vmem_limit_bytes=VMEM_LIMIT),
        name="out_proj",
    )(ret, sgu, x, w_bf16, norm_w.reshape(1, D_MODEL))


def _ffn_up_kernel(hp_ref, h_ref, hn_ref, wa_ref, wb_ref, cwa_ref, cwb_ref, cba_ref, cbb_ref,
                   o_ref, lhs_ref, u0_ref, u1_ref, *, n_j, tiles_per_seq, row_block, col_block):
    s = pl.program_id(0)
    last = pl.num_programs(0) - 2
    sm = jnp.minimum(s, last)
    i = sm // n_j
    j = sm % n_j
    tm = h_ref.shape[0]
    tf = wa_ref.shape[1]
    halo = hp_ref.shape[0]

    @pl.when(s == 0)
    def _():
        u1_ref[...] = jnp.zeros_like(u1_ref)

    @pl.when((j == 0) & (s <= last))
    def _():
        pos = i % tiles_per_seq
        hp = hp_ref[...]
        hn = hn_ref[...]
        lhs_ref[0:halo, :] = jnp.where(pos == 0, jnp.zeros_like(hp), hp)
        lhs_ref[halo:halo + tm, :] = h_ref[...]
        lhs_ref[halo + tm:, :] = jnp.where(pos == tiles_per_seq - 1, jnp.zeros_like(hn), hn)

    def step(uw_ref, ur_ref):
        def conv(r, c, base, cw_ref, cb_ref):
            n = row_block + 2 * SUBLANES
            u = ur_ref[halo - SUBLANES + r:halo + SUBLANES + r + row_block,
                       base + c:base + c + col_block]
            prev = pltpu.roll(u, 1, 0)[SUBLANES:SUBLANES + row_block]
            nxt = pltpu.roll(u, n - 1, 0)[SUBLANES:SUBLANES + row_block]
            cur = u[SUBLANES:SUBLANES + row_block]
            cols = slice(c, c + col_block)
            return (prev * cw_ref[0:1, cols] + cur * cw_ref[1:2, cols] + nxt * cw_ref[2:3, cols]
                    + cb_ref[:, cols])

        def finish_block(r, c):
            a = conv(r, c, 0, cwa_ref, cba_ref)
            b = conv(r, c, tf, cwb_ref, cbb_ref)
            o_ref[r:r + row_block, c:c + col_block] = (_silu(a) * b).astype(o_ref.dtype)

        for r in range(0, tm, row_block):
            for c in range(0, tf, col_block):
                finish_block(r, c)
        lhs = lhs_ref[...]
        uw_ref[:, 0:tf] = jnp.dot(lhs, wa_ref[...], preferred_element_type=F32)
        uw_ref[:, tf:] = jnp.dot(lhs, wb_ref[...], preferred_element_type=F32)

    @pl.when(s % 2 == 0)
    def _():
        step(u0_ref, u1_ref)

    @pl.when(s % 2 == 1)
    def _():
        step(u1_ref, u0_ref)


def _ffn_up(h2, w_up_bf16, conv_w, conv_b, *, seq, tm=FFN_ROWS, tf=FFN_TILE, row_block=16,
            col_block=128):
    t = h2.shape[0]
    halo = BF16_ROWS
    n_j = D_FF // tf
    n_mm = (t // tm) * n_j
    halo_blocks = tm // halo
    last_halo_block = t // halo - 1

    def mm(s):
        return jnp.minimum(s, n_mm - 1)

    def ep(s):
        return jnp.maximum(s - 1, 0)

    kernel = functools.partial(_ffn_up_kernel, n_j=n_j, tiles_per_seq=seq // tm,
                               row_block=row_block, col_block=col_block)
    return pl.pallas_call(
        kernel,
        grid=(n_mm + 1,),
        in_specs=[
            pl.BlockSpec((halo, D_MODEL),
                         lambda s: (jnp.maximum((mm(s) // n_j) * halo_blocks - 1, 0), 0)),
            pl.BlockSpec((tm, D_MODEL), lambda s: (mm(s) // n_j, 0)),
            pl.BlockSpec((halo, D_MODEL),
                         lambda s: (jnp.minimum((mm(s) // n_j + 1) * halo_blocks, last_halo_block), 0)),
            pl.BlockSpec((D_MODEL, tf), lambda s: (0, mm(s) % n_j)),
            pl.BlockSpec((D_MODEL, tf), lambda s: (0, n_j + mm(s) % n_j)),
            pl.BlockSpec((3, tf), lambda s: (0, ep(s) % n_j)),
            pl.BlockSpec((3, tf), lambda s: (0, n_j + ep(s) % n_j)),
            pl.BlockSpec((1, tf), lambda s: (0, ep(s) % n_j)),
            pl.BlockSpec((1, tf), lambda s: (0, n_j + ep(s) % n_j)),
        ],
        out_specs=pl.BlockSpec((None, tm, tf), lambda s: (ep(s) % n_j, ep(s) // n_j, 0)),
        out_shape=jax.ShapeDtypeStruct((n_j, t, tf), BF16),
        scratch_shapes=[pltpu.VMEM((tm + 2 * halo, D_MODEL), BF16),
                        pltpu.VMEM((tm + 2 * halo, 2 * tf), F32),
                        pltpu.VMEM((tm + 2 * halo, 2 * tf), F32)],
        compiler_params=pltpu.CompilerParams(
            dimension_semantics=("arbitrary",),
            vmem_limit_bytes=VMEM_LIMIT),
        name="ffn_up",
    )(h2, h2, h2, w_up_bf16, w_up_bf16, conv_w, conv_w,
      conv_b.reshape(1, 2 * D_FF), conv_b.reshape(1, 2 * D_FF))


def _ffn_up_direct_kernel(hp_ref, h_ref, hn_ref, wa_ref, wb_ref, cwa_ref, cwb_ref, cba_ref,
                          cbb_ref, o_ref, lhs_ref, *, tiles_per_seq):
    i = pl.program_id(0)
    j = pl.program_id(1)
    tm = h_ref.shape[0]
    halo = hp_ref.shape[0]
    n_rows = tm + 2 * halo

    @pl.when(j == 0)
    def _():
        pos = i % tiles_per_seq
        hp = hp_ref[...]
        hn = hn_ref[...]
        lhs_ref[0:halo, :] = jnp.where(pos == 0, jnp.zeros_like(hp), hp)
        lhs_ref[halo:halo + tm, :] = h_ref[...]
        lhs_ref[halo + tm:, :] = jnp.where(pos == tiles_per_seq - 1, jnp.zeros_like(hn), hn)

    lhs = lhs_ref[...]

    def conv_branch(w_ref, cw_ref, cb_ref):
        u = jnp.dot(lhs, w_ref[...], preferred_element_type=F32)
        prev = pltpu.roll(u, 1, 0)[halo:halo + tm]
        nxt = pltpu.roll(u, n_rows - 1, 0)[halo:halo + tm]
        cur = u[halo:halo + tm]
        return prev * cw_ref[0:1, :] + cur * cw_ref[1:2, :] + nxt * cw_ref[2:3, :] + cb_ref[...]

    a = conv_branch(wa_ref, cwa_ref, cba_ref)
    b = conv_branch(wb_ref, cwb_ref, cbb_ref)
    o_ref[...] = (_silu(a) * b).astype(o_ref.dtype)


def _ffn_up_direct(h2, w_up_bf16, conv_w, conv_b, *, seq, tm=FFN_ROWS, tf=FFN_TILE):
    t = h2.shape[0]
    halo = BF16_ROWS
    n_j = D_FF // tf
    halo_blocks = tm // halo
    last_halo_block = t // halo - 1
    kernel = functools.partial(_ffn_up_direct_kernel, tiles_per_seq=seq // tm)
    return pl.pallas_call(
        kernel,
        grid=(t // tm, n_j),
        in_specs=[
            pl.BlockSpec((halo, D_MODEL), lambda i, j: (jnp.maximum(i * halo_blocks - 1, 0), 0)),
            pl.BlockSpec((tm, D_MODEL), lambda i, j: (i, 0)),
            pl.BlockSpec((halo, D_MODEL),
                         lambda i, j: (jnp.minimum((i + 1) * halo_blocks, last_halo_block), 0)),
            pl.BlockSpec((D_MODEL, tf), lambda i, j: (0, j)),
            pl.BlockSpec((D_MODEL, tf), lambda i, j: (0, n_j + j)),
            pl.BlockSpec((3, tf), lambda i, j: (0, j)),
            pl.BlockSpec((3, tf), lambda i, j: (0, n_j + j)),
            pl.BlockSpec((1, tf), lambda i, j: (0, j)),
            pl.BlockSpec((1, tf), lambda i, j: (0, n_j + j)),
        ],
        out_specs=pl.BlockSpec((None, tm, tf), lambda i, j: (j, i, 0)),
        out_shape=jax.ShapeDtypeStruct((n_j, t, tf), BF16),
        scratch_shapes=[pltpu.VMEM((tm + 2 * halo, D_MODEL), BF16)],
        compiler_params=pltpu.CompilerParams(
            dimension_semantics=("parallel", "arbitrary"),
            vmem_limit_bytes=VMEM_LIMIT),
        name="ffn_up",
    )(h2, h2, h2, w_up_bf16, w_up_bf16, conv_w, conv_w,
      conv_b.reshape(1, 2 * D_FF), conv_b.reshape(1, 2 * D_FF))


def _ffn_down_kernel(act_ref, w_ref, x2_ref, fw_ref, o_ref):
    act = jnp.concatenate([act_ref[j] for j in range(act_ref.shape[0])], axis=1)
    y = x2_ref[...] + jnp.dot(act, w_ref[...], preferred_element_type=F32)
    o_ref[...] = _rms(y, fw_ref[...])


def _ffn_down(act, w_down_bf16, x2, final_w, *, tm=512):
    n_j, t, tf = act.shape
    return pl.pallas_call(
        _ffn_down_kernel,
        grid=(t // tm,),
        in_specs=[
            pl.BlockSpec((n_j, tm, tf), lambda i: (0, i, 0)),
            pl.BlockSpec((D_FF, D_MODEL), lambda i: (0, 0), pipeline_mode=pl.Buffered(1)),
            pl.BlockSpec((tm, D_MODEL), lambda i: (i, 0)),
            pl.BlockSpec((1, D_MODEL), lambda i: (0, 0)),
        ],
        out_specs=pl.BlockSpec((tm, D_MODEL), lambda i: (i, 0)),
        out_shape=jax.ShapeDtypeStruct((t, D_MODEL), F32),
        compiler_params=pltpu.CompilerParams(
            dimension_semantics=("parallel",),
            vmem_limit_bytes=VMEM_LIMIT),
        name="ffn_down",
    )(act, w_down_bf16, x2, final_w.reshape(1, D_MODEL))


def _ffn(h2, x2, w_up_bf16, conv_w, conv_b, w_down_bf16, final_w, *, seq):
    if seq == 4096:
        act = _ffn_up_direct(h2, w_up_bf16, conv_w, conv_b, seq=seq, tm=512)
    else:
        act = _ffn_up_direct(h2, w_up_bf16, conv_w, conv_b, seq=seq, tm=1024)
    return _ffn_down(act, w_down_bf16, x2, final_w)


def _rope_tables(seq):
    inv_freq = ROPE_BASE ** (-np.arange(0, HEAD_DIM, 2, dtype=np.float64) / HEAD_DIM)
    ang = np.arange(seq, dtype=np.float64)[:, None] * inv_freq[None, :]
    cos = np.cos(ang).astype(np.float32)
    sin = np.sin(ang).astype(np.float32)
    return (jnp.asarray(np.concatenate([cos, cos], axis=-1)),
            jnp.asarray(np.concatenate([-sin, sin], axis=-1)))


def _col_tiles(w, tile):
    k, n = w.shape
    return w.astype(BF16).reshape(k, n // tile, tile).transpose(1, 0, 2)


def _per_slab(vec, slabs):
    return jnp.broadcast_to(vec.reshape(slabs, 1, LANES), (slabs, 8, LANES))


def _trunk(x, p):
    batch, seq, _ = x.shape
    xt = x.reshape(batch * seq, D_MODEL)
    z = _in_proj(xt, p["norm1_w"], p["w_in"])
    ret = _retention(z, p["cos"], p["sin"], p["lgf"], p["lgb"], p["gnw"], batch=batch, seq=seq)
    sgu = _sgu(z, p["sgu_nw"], p["w_s"], p["sgu_b"])
    x2, h2 = _out_proj(ret, sgu, xt, p["w_out"], p["norm2_w"])
    y = _ffn(h2, x2, p["w_up"], p["conv_w"], p["conv_b"], p["w_down"], p["final_w"], seq=seq)
    return y.reshape(batch, seq, D_MODEL)


def kernel(x_prompt, x_sample, norm1_w, w_in, ret_log_decay_fwd, ret_log_decay_bwd, ret_gn_w, sgu_norm_w, sgu_w_s, sgu_b, w_out, norm2_w, w_up, conv_w, conv_b, w_down, final_norm_w):
    max_seq = max(x_prompt.shape[1], x_sample.shape[1])
    cos, sin = _rope_tables(max_seq)
    p = {
        "norm1_w": norm1_w[0],
        "w_in": w_in[0].astype(BF16),
        "cos": cos,
        "sin": sin,
        "lgf": jnp.broadcast_to(ret_log_decay_fwd[0].astype(F32)[:, None, None], (HEADS, 8, LANES)),
        "lgb": jnp.broadcast_to(ret_log_decay_bwd[0].astype(F32)[:, None, None], (HEADS, 8, LANES)),
        "gnw": _per_slab(ret_gn_w[0], HEADS),
        "sgu_nw": _per_slab(sgu_norm_w[0], GROUPS),
        "w_s": sgu_w_s[0].astype(BF16),
        "sgu_b": jnp.broadcast_to(sgu_b[0][:, :, None], (GROUPS, CHUNK, LANES)),
        "w_out": w_out[0].astype(BF16),
        "norm2_w": norm2_w[0],
        "w_up": w_up[0].astype(BF16),
        "conv_w": conv_w[0],
        "conv_b": conv_b[0],
        "w_down": w_down[0].astype(BF16),
        "final_w": final_norm_w,
    }
    return (_trunk(x_prompt, p), _trunk(x_sample, p))
```

```python
import functools
import math

import jax
import jax.numpy as jnp
import numpy as np
from jax import lax
from jax.experimental import pallas as pl
from jax.experimental.pallas import tpu as pltpu

D_MODEL = 2048
CHUNK = 128
HEADS = 8
HEAD_DIM = 128
RET_WIDTH = HEADS * HEAD_DIM
GROUPS = 8
SGU_WIDTH = GROUPS * CHUNK
IN_COLS = 4 * RET_WIDTH + 2 * SGU_WIDTH
IN_SLABS = IN_COLS // 128
D_FF = 5632
ROPE_BASE = 10000.0
EPS = 1e-6
LANES = 128
BF16_ROWS = 16
IN_PROJ_TILE = 2048
FFN_TILE = 512
FFN_ROWS = 1024
VMEM_LIMIT = 56 * 1024 * 1024

F32 = jnp.float32
BF16 = jnp.bfloat16


def _rms(x, w):
    ms = jnp.mean(x * x, axis=-1, keepdims=True)
    return x * lax.rsqrt(ms + EPS) * w


def _gelu_tanh(x):
    c = math.sqrt(2.0 / math.pi)
    h = 0.5 * x
    return h + h * jnp.tanh(x * (c + (c * 0.044715) * (x * x)))


def _silu(x):
    h = 0.5 * x
    return h + h * jnp.tanh(h)


def _inproj_kernel(x_ref, nw_ref, w_ref, z_ref, h_ref):
    @pl.when(pl.program_id(1) == 0)
    def _():
        h_ref[...] = _rms(x_ref[...], nw_ref[...]).astype(BF16)

    res = jnp.dot(h_ref[...], w_ref[...], preferred_element_type=F32)
    for s in range(z_ref.shape[0]):
        z_ref[s] = res[:, s * LANES:(s + 1) * LANES].astype(z_ref.dtype)


def _in_proj(x, norm_w, w_bf16, *, tm=1024, tn=IN_PROJ_TILE):
    t = x.shape[0]
    return pl.pallas_call(
        _inproj_kernel,
        grid=(t // tm, IN_COLS // tn),
        in_specs=[
            pl.BlockSpec((tm, D_MODEL), lambda i, j: (i, 0)),
            pl.BlockSpec((1, D_MODEL), lambda i, j: (0, 0)),
            pl.BlockSpec((D_MODEL, tn), lambda i, j: (0, j)),
        ],
        out_specs=pl.BlockSpec((tn // LANES, tm, LANES), lambda i, j: (j, i, 0)),
        out_shape=jax.ShapeDtypeStruct((IN_SLABS, t, LANES), BF16),
        scratch_shapes=[pltpu.VMEM((tm, D_MODEL), BF16)],
        compiler_params=pltpu.CompilerParams(
            dimension_semantics=("parallel", "arbitrary"),
            vmem_limit_bytes=VMEM_LIMIT),
        name="in_proj",
    )(x, norm_w.reshape(1, D_MODEL), w_bf16)


def _ret_kernel(q_ref, k_ref, v_ref, g_ref, cos_ref, sin_ref, lgf_ref, lgb_ref, gnw_ref,
                o_ref, lhs_ref, kt_ref, kvf_ref, rb_ref):
    seq = q_ref.shape[1]
    n_chunks = seq // CHUNK
    lgf = lgf_ref[0, 0:1, :]
    lgb = lgb_ref[0, 0:1, :]
    row = lax.broadcasted_iota(jnp.int32, (CHUNK, CHUNK), 0).astype(F32)
    col = lax.broadcasted_iota(jnp.int32, (CHUNK, CHUNK), 1).astype(F32)
    diff = row - col
    scale = HEAD_DIM ** -0.5
    dcomb = jnp.where(diff >= 0, jnp.exp(lgf * diff), jnp.exp(lgb * (-diff))) * scale
    xi_f = jnp.exp(lgf * (row + 1.0)) * scale
    xi_b = jnp.exp(lgb * (CHUNK - row)) * scale
    zeta_f = jnp.exp(lgf * (CHUNK - 1.0 - col))
    zeta_b = jnp.exp(lgb * col)
    decay_f = jnp.exp(lgf * CHUNK)
    decay_b = jnp.exp(lgb * CHUNK)

    def rows(c):
        return pl.ds(pl.multiple_of(c * CHUNK, CHUNK), CHUNK)

    def prep(i, state):
        c = n_chunks - 1 - i
        sl = rows(c)
        q = q_ref[0, sl, :].astype(F32)
        k = k_ref[0, sl, :].astype(F32)
        cos = cos_ref[sl, :]
        sin = sin_ref[sl, :]
        qr = q * cos + pltpu.roll(q, HEAD_DIM // 2, 1) * sin
        kr = k * cos + pltpu.roll(k, HEAD_DIM // 2, 1) * sin
        kt = kr.T
        lhs_ref[sl, :] = jnp.concatenate([qr, qr * xi_f, qr * xi_b], axis=1).astype(BF16)
        kt_ref[c] = kt.astype(BF16)
        ktz = jnp.concatenate([kt * zeta_f, kt * zeta_b], axis=0).astype(BF16)
        kv = jnp.dot(ktz, v_ref[0, sl, :], preferred_element_type=F32)
        kvf_ref[c] = kv[:HEAD_DIM]
        rb_ref[c] = state.astype(BF16)
        return state * decay_b + kv[HEAD_DIM:]

    lax.fori_loop(0, n_chunks, prep, jnp.zeros((HEAD_DIM, HEAD_DIM), F32), unroll=4)

    gnw = gnw_ref[0, 0:1, :]

    def fwd(c, state):
        sl = rows(c)
        s = jnp.dot(lhs_ref[sl, 0:HEAD_DIM], kt_ref[c], preferred_element_type=F32) * dcomb
        lhs = jnp.concatenate([s.astype(BF16), lhs_ref[sl, HEAD_DIM:]], axis=1)
        rhs = jnp.concatenate([v_ref[0, sl, :], state.astype(BF16), rb_ref[c]], axis=0)
        o = jnp.dot(lhs, rhs, preferred_element_type=F32)
        mu = jnp.mean(o, axis=-1, keepdims=True)
        d = o - mu
        var = jnp.mean(d * d, axis=-1, keepdims=True)
        on = d * lax.rsqrt(var + EPS) * gnw
        g = g_ref[0, sl, :].astype(F32)
        o_ref[0, sl, :] = (_silu(g) * on).astype(o_ref.dtype)
        return state * decay_f + kvf_ref[c]

    lax.fori_loop(0, n_chunks, fwd, jnp.zeros((HEAD_DIM, HEAD_DIM), F32), unroll=8)


def _retention(z, cos, sin, lgf, lgb, gnw, *, batch, seq):
    t = batch * seq
    n_chunks = seq // CHUNK

    def zspec(base):
        return pl.BlockSpec((1, seq, LANES), lambda b, h: (base + h, b, 0))

    table = pl.BlockSpec((seq, LANES), lambda b, h: (0, 0), pipeline_mode=pl.Buffered(1))
    per_head = pl.BlockSpec((1, 8, LANES), lambda b, h: (h, 0, 0))
    return pl.pallas_call(
        _ret_kernel,
        grid=(batch, HEADS),
        in_specs=[zspec(0), zspec(HEADS), zspec(2 * HEADS), zspec(3 * HEADS),
                  table, table, per_head, per_head, per_head],
        out_specs=pl.BlockSpec((1, seq, LANES), lambda b, h: (h, b, 0)),
        out_shape=jax.ShapeDtypeStruct((HEADS, t, LANES), BF16),
        scratch_shapes=[
            pltpu.VMEM((seq, 3 * HEAD_DIM), BF16),
            pltpu.VMEM((n_chunks, HEAD_DIM, CHUNK), BF16),
            pltpu.VMEM((n_chunks, HEAD_DIM, HEAD_DIM), F32),
            pltpu.VMEM((n_chunks, HEAD_DIM, HEAD_DIM), BF16),
        ],
        compiler_params=pltpu.CompilerParams(
            dimension_semantics=("parallel", "parallel"),
            vmem_limit_bytes=VMEM_LIMIT),
        name="retention",
    )(z, z, z, z, cos, sin, lgf, lgb, gnw)


def _sgu_kernel(zu_ref, zv_ref, nw_ref, ws_ref, b_ref, o_ref):
    rows = zu_ref.shape[1]
    n_chunks = rows // CHUNK
    v = _gelu_tanh(zv_ref[...].astype(F32))
    ms = jnp.sum(jnp.sum(v * v, axis=0), axis=-1, keepdims=True) * (1.0 / SGU_WIDTH)
    inv = lax.rsqrt(ms + EPS)
    for g in range(GROUPS):
        vn = (v[g] * inv * nw_ref[g, 0:1, :]).astype(BF16)
        rhs = jnp.concatenate(
            [vn[c * CHUNK:(c + 1) * CHUNK] for c in range(n_chunks)], axis=1)
        sp = jnp.dot(ws_ref[g], rhs, preferred_element_type=F32)
        u = _gelu_tanh(zu_ref[g].astype(F32))
        bias = b_ref[g]
        for c in range(n_chunks):
            sl = slice(c * CHUNK, (c + 1) * CHUNK)
            o_ref[g, sl, :] = (u[sl] * (sp[:, sl] + bias)).astype(o_ref.dtype)


def _sgu(z, norm_w, ws_bf16, bias, *, rows=512):
    t = z.shape[1]
    u_base = 4 * HEADS // GROUPS
    return pl.pallas_call(
        _sgu_kernel,
        grid=(t // rows,),
        in_specs=[
            pl.BlockSpec((GROUPS, rows, LANES), lambda i: (u_base, i, 0)),
            pl.BlockSpec((GROUPS, rows, LANES), lambda i: (u_base + 1, i, 0)),
            pl.BlockSpec((GROUPS, 8, LANES), lambda i: (0, 0, 0)),
            pl.BlockSpec((GROUPS, CHUNK, CHUNK), lambda i: (0, 0, 0)),
            pl.BlockSpec((GROUPS, CHUNK, LANES), lambda i: (0, 0, 0)),
        ],
        out_specs=pl.BlockSpec((GROUPS, rows, LANES), lambda i: (0, i, 0)),
        out_shape=jax.ShapeDtypeStruct((GROUPS, t, LANES), BF16),
        compiler_params=pltpu.CompilerParams(
            dimension_semantics=("parallel",),
            vmem_limit_bytes=VMEM_LIMIT),
        name="sgu",
    )(z, z, norm_w, ws_bf16, bias)


def _outproj_kernel(ret_ref, sgu_ref, x_ref, w_ref, nw_ref, x2_ref, h2_ref):
    mix = jnp.concatenate([ret_ref[s] for s in range(HEADS)]
                          + [sgu_ref[s] for s in range(GROUPS)], axis=1)
    x2 = x_ref[...] + jnp.dot(mix, w_ref[...], preferred_element_type=F32)
    x2_ref[...] = x2
    h2_ref[...] = _rms(x2, nw_ref[...]).astype(BF16)


def _out_proj(ret, sgu, x, w_bf16, norm_w, *, tm=512):
    t = x.shape[0]
    return pl.pallas_call(
        _outproj_kernel,
        grid=(t // tm,),
        in_specs=[
            pl.BlockSpec((HEADS, tm, LANES), lambda i: (0, i, 0)),
            pl.BlockSpec((GROUPS, tm, LANES), lambda i: (0, i, 0)),
            pl.BlockSpec((tm, D_MODEL), lambda i: (i, 0)),
            pl.BlockSpec((D_MODEL, D_MODEL), lambda i: (0, 0)),
            pl.BlockSpec((1, D_MODEL), lambda i: (0, 0)),
        ],
        out_specs=[pl.BlockSpec((tm, D_MODEL), lambda i: (i, 0)),
                   pl.BlockSpec((tm, D_MODEL), lambda i: (i, 0))],
        out_shape=[jax.ShapeDtypeStruct((t, D_MODEL), F32),
                   jax.ShapeDtypeStruct((t, D_MODEL), BF16)],
        compiler_params=pltpu.CompilerParams(
            dimension_semantics=("parallel",),
            vmem_limit_bytes=VMEM_LIMIT),
        name="out_proj",
    )(ret, sgu, x, w_bf16, norm_w.reshape(1, D_MODEL))


def _ffn_up_kernel(hp_ref, h_ref, hn_ref, wa_ref, wb_ref, cwa_ref, cwb_ref, cba_ref,
                   cbb_ref, o_ref, lhs_ref, *, tiles_per_seq):
    i = pl.program_id(0)
    j = pl.program_id(1)
    tm = h_ref.shape[0]
    halo = hp_ref.shape[0]

    @pl.when(j == 0)
    def _():
        pos = i % tiles_per_seq
        hp = hp_ref[...]
        hn = hn_ref[...]
        lhs_ref[0:halo, :] = jnp.where(pos == 0, jnp.zeros_like(hp), hp)
        lhs_ref[halo:halo + tm, :] = h_ref[...]
        lhs_ref[halo + tm:, :] = jnp.where(pos == tiles_per_seq - 1, jnp.zeros_like(hn), hn)

    lhs = lhs_ref[...]
    n_rows = tm + 2 * halo

    def conv_branch(w_ref, cw_ref, cb_ref):
        u = jnp.dot(lhs, w_ref[...], preferred_element_type=F32)
        prev = pltpu.roll(u, 1, 0)[halo:halo + tm]
        nxt = pltpu.roll(u, n_rows - 1, 0)[halo:halo + tm]
        cur = u[halo:halo + tm]
        return prev * cw_ref[0:1, :] + cur * cw_ref[1:2, :] + nxt * cw_ref[2:3, :] + cb_ref[...]

    a = conv_branch(wa_ref, cwa_ref, cba_ref)
    b = conv_branch(wb_ref, cwb_ref, cbb_ref)
    o_ref[...] = (_silu(a) * b).astype(o_ref.dtype)


def _ffn_up(h2, w_up_bf16, conv_w, conv_b, *, seq, tm=FFN_ROWS, tf=FFN_TILE):
    t = h2.shape[0]
    halo = BF16_ROWS
    n_j = D_FF // tf
    halo_blocks = tm // halo
    last_halo_block = t // halo - 1
    kernel = functools.partial(_ffn_up_kernel, tiles_per_seq=seq // tm)
    return pl.pallas_call(
        kernel,
        grid=(t // tm, n_j),
        in_specs=[
            pl.BlockSpec((halo, D_MODEL), lambda i, j: (jnp.maximum(i * halo_blocks - 1, 0), 0)),
            pl.BlockSpec((tm, D_MODEL), lambda i, j: (i, 0)),
            pl.BlockSpec((halo, D_MODEL),
                         lambda i, j: (jnp.minimum((i + 1) * halo_blocks, last_halo_block), 0)),
            pl.BlockSpec((D_MODEL, tf), lambda i, j: (0, j)),
            pl.BlockSpec((D_MODEL, tf), lambda i, j: (0, n_j + j)),
            pl.BlockSpec((3, tf), lambda i, j: (0, j)),
            pl.BlockSpec((3, tf), lambda i, j: (0, n_j + j)),
            pl.BlockSpec((1, tf), lambda i, j: (0, j)),
            pl.BlockSpec((1, tf), lambda i, j: (0, n_j + j)),
        ],
        out_specs=pl.BlockSpec((None, tm, tf), lambda i, j: (j, i, 0)),
        out_shape=jax.ShapeDtypeStruct((n_j, t, tf), BF16),
        scratch_shapes=[pltpu.VMEM((tm + 2 * halo, D_MODEL), BF16)],
        compiler_params=pltpu.CompilerParams(
            dimension_semantics=("parallel", "arbitrary"),
            vmem_limit_bytes=VMEM_LIMIT),
        name="ffn_up",
    )(h2, h2, h2, w_up_bf16, w_up_bf16, conv_w, conv_w,
      conv_b.reshape(1, 2 * D_FF), conv_b.reshape(1, 2 * D_FF))


def _ffn_down_kernel(act_ref, w_ref, x2_ref, fw_ref, o_ref):
    act = jnp.concatenate([act_ref[j] for j in range(act_ref.shape[0])], axis=1)
    y = x2_ref[...] + jnp.dot(act, w_ref[...], preferred_element_type=F32)
    o_ref[...] = _rms(y, fw_ref[...])


def _ffn_down(act, w_down_bf16, x2, final_w, *, tm=512):
    n_j, t, tf = act.shape
    return pl.pallas_call(
        _ffn_down_kernel,
        grid=(t // tm,),
        in_specs=[
            pl.BlockSpec((n_j, tm, tf), lambda i: (0, i, 0)),
            pl.BlockSpec((D_FF, D_MODEL), lambda i: (0, 0), pipeline_mode=pl.Buffered(1)),
            pl.BlockSpec((tm, D_MODEL), lambda i: (i, 0)),
            pl.BlockSpec((1, D_MODEL), lambda i: (0, 0)),
        ],
        out_specs=pl.BlockSpec((tm, D_MODEL), lambda i: (i, 0)),
        out_shape=jax.ShapeDtypeStruct((t, D_MODEL), F32),
        compiler_params=pltpu.CompilerParams(
            dimension_semantics=("parallel",),
            vmem_limit_bytes=VMEM_LIMIT),
        name="ffn_down",
    )(act, w_down_bf16, x2, final_w.reshape(1, D_MODEL))


def _ffn(h2, x2, w_up_bf16, conv_w, conv_b, w_down_bf16, final_w, *, seq):
    act = _ffn_up(h2, w_up_bf16, conv_w, conv_b, seq=seq)
    return _ffn_down(act, w_down_bf16, x2, final_w)


def _rope_tables(seq):
    inv_freq = ROPE_BASE ** (-np.arange(0, HEAD_DIM, 2, dtype=np.float64) / HEAD_DIM)
    ang = np.arange(seq, dtype=np.float64)[:, None] * inv_freq[None, :]
    cos = np.cos(ang).astype(np.float32)
    sin = np.sin(ang).astype(np.float32)
    return (jnp.asarray(np.concatenate([cos, cos], axis=-1)),
            jnp.asarray(np.concatenate([-sin, sin], axis=-1)))


def _per_slab(vec, slabs):
    return jnp.broadcast_to(vec.reshape(slabs, 1, LANES), (slabs, 8, LANES))


def _trunk(x, p):
    batch, seq, _ = x.shape
    xt = x.reshape(batch * seq, D_MODEL)
    z = _in_proj(xt, p["norm1_w"], p["w_in"])
    ret = _retention(z, p["cos"], p["sin"], p["lgf"], p["lgb"], p["gnw"], batch=batch, seq=seq)
    sgu = _sgu(z, p["sgu_nw"], p["w_s"], p["sgu_b"])
    x2, h2 = _out_proj(ret, sgu, xt, p["w_out"], p["norm2_w"])
    y = _ffn(h2, x2, p["w_up"], p["conv_w"], p["conv_b"], p["w_down"], p["final_w"], seq=seq)
    return y.reshape(batch, seq, D_MODEL)


def kernel(x_prompt, x_sample, norm1_w, w_in, ret_log_decay_fwd, ret_log_decay_bwd, ret_gn_w, sgu_norm_w, sgu_w_s, sgu_b, w_out, norm2_w, w_up, conv_w, conv_b, w_down, final_norm_w):
    max_seq = max(x_prompt.shape[1], x_sample.shape[1])
    cos, sin = _rope_tables(max_seq)
    p = {
        "norm1_w": norm1_w[0],
        "w_in": w_in[0].astype(BF16),
        "cos": cos,
        "sin": sin,
        "lgf": jnp.broadcast_to(ret_log_decay_fwd[0].astype(F32)[:, None, None], (HEADS, 8, LANES)),
        "lgb": jnp.broadcast_to(ret_log_decay_bwd[0].astype(F32)[:, None, None], (HEADS, 8, LANES)),
        "gnw": _per_slab(ret_gn_w[0], HEADS),
        "sgu_nw": _per_slab(sgu_norm_w[0], GROUPS),
        "w_s": sgu_w_s[0].astype(BF16),
        "sgu_b": jnp.broadcast_to(sgu_b[0][:, :, None], (GROUPS, CHUNK, LANES)),
        "w_out": w_out[0].astype(BF16),
        "norm2_w": norm2_w[0],
        "w_up": w_up[0].astype(BF16),
        "conv_w": conv_w[0],
        "conv_b": conv_b[0],
        "w_down": w_down[0].astype(BF16),
        "final_w": final_norm_w,
    }
    return (_trunk(x_prompt, p), _trunk(x_sample, p))
```

```python
import functools
import math

import jax
import jax.numpy as jnp
import numpy as np
from jax import lax
from jax.experimental import pallas as pl
from jax.experimental.pallas import tpu as pltpu

D_MODEL = 2048
CHUNK = 128
HEADS = 8
HEAD_DIM = 128
RET_WIDTH = HEADS * HEAD_DIM
GROUPS = 8
SGU_WIDTH = GROUPS * CHUNK
IN_COLS = 4 * RET_WIDTH + 2 * SGU_WIDTH
IN_SLABS = IN_COLS // 128
D_FF = 5632
ROPE_BASE = 10000.0
EPS = 1e-6
LANES = 128
BF16_ROWS = 16
IN_PROJ_TILE = 2048
FFN_TILE = 512
FFN_ROWS = 1024
VMEM_LIMIT = 56 * 1024 * 1024

F32 = jnp.float32
BF16 = jnp.bfloat16


def _rms(x, w):
    ms = jnp.mean(x * x, axis=-1, keepdims=True)
    return x * lax.rsqrt(ms + EPS) * w


def _gelu_tanh(x):
    c = math.sqrt(2.0 / math.pi)
    h = 0.5 * x
    return h + h * jnp.tanh(x * (c + (c * 0.044715) * (x * x)))


def _silu(x):
    h = 0.5 * x
    return h + h * jnp.tanh(h)


def _inproj_kernel(x_ref, nw_ref, w_ref, z_ref, h_ref):
    @pl.when(pl.program_id(1) == 0)
    def _():
        h_ref[...] = _rms(x_ref[...], nw_ref[...]).astype(BF16)

    res = jnp.dot(h_ref[...], w_ref[...], preferred_element_type=F32)
    for s in range(z_ref.shape[0]):
        z_ref[s] = res[:, s * LANES:(s + 1) * LANES].astype(z_ref.dtype)


def _in_proj(x, norm_w, w_bf16, *, tm=1024, tn=IN_PROJ_TILE):
    t = x.shape[0]
    return pl.pallas_call(
        _inproj_kernel,
        grid=(t // tm, IN_COLS // tn),
        in_specs=[
            pl.BlockSpec((tm, D_MODEL), lambda i, j: (i, 0)),
            pl.BlockSpec((1, D_MODEL), lambda i, j: (0, 0)),
            pl.BlockSpec((D_MODEL, tn), lambda i, j: (0, j)),
        ],
        out_specs=pl.BlockSpec((tn // LANES, tm, LANES), lambda i, j: (j, i, 0)),
        out_shape=jax.ShapeDtypeStruct((IN_SLABS, t, LANES), BF16),
        scratch_shapes=[pltpu.VMEM((tm, D_MODEL), BF16)],
        compiler_params=pltpu.CompilerParams(
            dimension_semantics=("parallel", "arbitrary"),
            vmem_limit_bytes=VMEM_LIMIT),
        name="in_proj",
    )(x, norm_w.reshape(1, D_MODEL), w_bf16)


def _ret_kernel(q_ref, k_ref, v_ref, g_ref, cos_ref, sin_ref, lgf_ref, lgb_ref, gnw_ref,
                o_ref, lhs_ref, kt_ref, kvf_ref, rb_ref):
    seq = q_ref.shape[1]
    n_chunks = seq // CHUNK
    lgf = lgf_ref[0, 0:1, :]
    lgb = lgb_ref[0, 0:1, :]
    row = lax.broadcasted_iota(jnp.int32, (CHUNK, CHUNK), 0).astype(F32)
    col = lax.broadcasted_iota(jnp.int32, (CHUNK, CHUNK), 1).astype(F32)
    diff = row - col
    scale = HEAD_DIM ** -0.5
    dcomb = jnp.where(diff >= 0, jnp.exp(lgf * diff), jnp.exp(lgb * (-diff))) * scale
    xi_f = jnp.exp(lgf * (row + 1.0)) * scale
    xi_b = jnp.exp(lgb * (CHUNK - row)) * scale
    zeta_f = jnp.exp(lgf * (CHUNK - 1.0 - col))
    zeta_b = jnp.exp(lgb * col)
    decay_f = jnp.exp(lgf * CHUNK)
    decay_b = jnp.exp(lgb * CHUNK)

    def rows(c):
        return pl.ds(pl.multiple_of(c * CHUNK, CHUNK), CHUNK)

    def prep(i, state):
        c = n_chunks - 1 - i
        sl = rows(c)
        q = q_ref[0, sl, :].astype(F32)
        k = k_ref[0, sl, :].astype(F32)
        cos = cos_ref[sl, :]
        sin = sin_ref[sl, :]
        qr = q * cos + pltpu.roll(q, HEAD_DIM // 2, 1) * sin
        kr = k * cos + pltpu.roll(k, HEAD_DIM // 2, 1) * sin
        kt = kr.T
        lhs_ref[sl, :] = jnp.concatenate([qr, qr * xi_f, qr * xi_b], axis=1).astype(BF16)
        kt_ref[c] = kt.astype(BF16)
        ktz = jnp.concatenate([kt * zeta_f, kt * zeta_b], axis=0).astype(BF16)
        kv = jnp.dot(ktz, v_ref[0, sl, :], preferred_element_type=F32)
        kvf_ref[c] = kv[:HEAD_DIM]
        rb_ref[c] = state.astype(BF16)
        return state * decay_b + kv[HEAD_DIM:]

    lax.fori_loop(0, n_chunks, prep, jnp.zeros((HEAD_DIM, HEAD_DIM), F32), unroll=4)

    gnw = gnw_ref[0, 0:1, :]

    def fwd(c, state):
        sl = rows(c)
        s = jnp.dot(lhs_ref[sl, 0:HEAD_DIM], kt_ref[c], preferred_element_type=F32) * dcomb
        lhs = jnp.concatenate([s.astype(BF16), lhs_ref[sl, HEAD_DIM:]], axis=1)
        rhs = jnp.concatenate([v_ref[0, sl, :], state.astype(BF16), rb_ref[c]], axis=0)
        o = jnp.dot(lhs, rhs, preferred_element_type=F32)
        mu = jnp.mean(o, axis=-1, keepdims=True)
        d = o - mu
        var = jnp.mean(d * d, axis=-1, keepdims=True)
        on = d * lax.rsqrt(var + EPS) * gnw
        g = g_ref[0, sl, :].astype(F32)
        o_ref[0, sl, :] = (_silu(g) * on).astype(o_ref.dtype)
        return state * decay_f + kvf_ref[c]

    lax.fori_loop(0, n_chunks, fwd, jnp.zeros((HEAD_DIM, HEAD_DIM), F32), unroll=8)


def _retention(z, cos, sin, lgf, lgb, gnw, *, batch, seq):
    t = batch * seq
    n_chunks = seq // CHUNK

    def zspec(base):
        return pl.BlockSpec((1, seq, LANES), lambda b, h: (base + h, b, 0))

    table = pl.BlockSpec((seq, LANES), lambda b, h: (0, 0), pipeline_mode=pl.Buffered(1))
    per_head = pl.BlockSpec((1, 8, LANES), lambda b, h: (h, 0, 0))
    return pl.pallas_call(
        _ret_kernel,
        grid=(batch, HEADS),
        in_specs=[zspec(0), zspec(HEADS), zspec(2 * HEADS), zspec(3 * HEADS),
                  table, table, per_head, per_head, per_head],
        out_specs=pl.BlockSpec((1, seq, LANES), lambda b, h: (h, b, 0)),
        out_shape=jax.ShapeDtypeStruct((HEADS, t, LANES), BF16),
        scratch_shapes=[
            pltpu.VMEM((seq, 3 * HEAD_DIM), BF16),
            pltpu.VMEM((n_chunks, HEAD_DIM, CHUNK), BF16),
            pltpu.VMEM((n_chunks, HEAD_DIM, HEAD_DIM), F32),
            pltpu.VMEM((n_chunks, HEAD_DIM, HEAD_DIM), BF16),
        ],
        compiler_params=pltpu.CompilerParams(
            dimension_semantics=("parallel", "parallel"),
            vmem_limit_bytes=VMEM_LIMIT),
        name="retention",
    )(z, z, z, z, cos, sin, lgf, lgb, gnw)


def _sgu_pointwise(zu_ref, zv_ref, nw_ref, u_ref, vn_ref):
    v = _gelu_tanh(zv_ref[...].astype(F32))
    ms = jnp.sum(jnp.sum(v * v, axis=0), axis=-1, keepdims=True) * (1.0 / SGU_WIDTH)
    inv = lax.rsqrt(ms + EPS)
    for g in range(GROUPS):
        vn_ref[g] = (v[g] * inv * nw_ref[g, 0:1, :]).astype(BF16)
        u_ref[g] = _gelu_tanh(zu_ref[g].astype(F32))


def _sgu_mix(u_ref, vn_ref, ws_ref, b_ref):
    n_chunks = vn_ref.shape[1] // CHUNK
    groups = []
    for g in range(GROUPS):
        rhs = jnp.concatenate(
            [vn_ref[g, c * CHUNK:(c + 1) * CHUNK, :] for c in range(n_chunks)], axis=1)
        sp = jnp.dot(ws_ref[g], rhs, preferred_element_type=F32)
        bias = b_ref[g]
        groups.append(jnp.concatenate(
            [u_ref[g, c * CHUNK:(c + 1) * CHUNK, :] * (sp[:, c * CHUNK:(c + 1) * CHUNK] + bias)
             for c in range(n_chunks)], axis=0).astype(BF16))
    return jnp.concatenate(groups, axis=1)


def _outproj_kernel(ret_ref, zu_ref, zv_ref, sgu_nw_ref, ws_ref, sb_ref, x_ref, w_ref, nw_ref,
                    x2_ref, h2_ref, u_ref, vn_ref):
    _sgu_pointwise(zu_ref, zv_ref, sgu_nw_ref, u_ref, vn_ref)
    ret = jnp.concatenate([ret_ref[s] for s in range(HEADS)], axis=1)
    x2_ref[...] = x_ref[...] + jnp.dot(ret, w_ref[0:RET_WIDTH, :], preferred_element_type=F32)
    sgu = _sgu_mix(u_ref, vn_ref, ws_ref, sb_ref)
    x2 = x2_ref[...] + jnp.dot(sgu, w_ref[RET_WIDTH:, :], preferred_element_type=F32)
    x2_ref[...] = x2
    h2_ref[...] = _rms(x2, nw_ref[...]).astype(BF16)


def _out_proj(ret, z, sgu_nw, ws_bf16, sgu_bias, x, w_bf16, norm_w, *, tm=512):
    t = x.shape[0]
    u_base = 4 * HEADS // GROUPS
    const3 = lambda i: (0, 0, 0)
    return pl.pallas_call(
        _outproj_kernel,
        grid=(t // tm,),
        in_specs=[
            pl.BlockSpec((HEADS, tm, LANES), lambda i: (0, i, 0)),
            pl.BlockSpec((GROUPS, tm, LANES), lambda i: (u_base, i, 0)),
            pl.BlockSpec((GROUPS, tm, LANES), lambda i: (u_base + 1, i, 0)),
            pl.BlockSpec((GROUPS, 8, LANES), const3),
            pl.BlockSpec((GROUPS, CHUNK, CHUNK), const3),
            pl.BlockSpec((GROUPS, CHUNK, LANES), const3),
            pl.BlockSpec((tm, D_MODEL), lambda i: (i, 0)),
            pl.BlockSpec((D_MODEL, D_MODEL), lambda i: (0, 0)),
            pl.BlockSpec((1, D_MODEL), lambda i: (0, 0)),
        ],
        out_specs=[pl.BlockSpec((tm, D_MODEL), lambda i: (i, 0)),
                   pl.BlockSpec((tm, D_MODEL), lambda i: (i, 0))],
        out_shape=[jax.ShapeDtypeStruct((t, D_MODEL), F32),
                   jax.ShapeDtypeStruct((t, D_MODEL), BF16)],
        scratch_shapes=[pltpu.VMEM((GROUPS, tm, LANES), F32),
                        pltpu.VMEM((GROUPS, tm, LANES), BF16)],
        compiler_params=pltpu.CompilerParams(
            dimension_semantics=("parallel",),
            vmem_limit_bytes=VMEM_LIMIT),
        name="out_proj",
    )(ret, z, z, sgu_nw, ws_bf16, sgu_bias, x, w_bf16, norm_w.reshape(1, D_MODEL))


def _ffn_up_kernel(hp_ref, h_ref, hn_ref, wa_ref, wb_ref, cwa_ref, cwb_ref, cba_ref,
                   cbb_ref, o_ref, lhs_ref, *, tiles_per_seq):
    i = pl.program_id(0)
    j = pl.program_id(1)
    tm = h_ref.shape[0]
    halo = hp_ref.shape[0]

    @pl.when(j == 0)
    def _():
        pos = i % tiles_per_seq
        hp = hp_ref[...]
        hn = hn_ref[...]
        lhs_ref[0:halo, :] = jnp.where(pos == 0, jnp.zeros_like(hp), hp)
        lhs_ref[halo:halo + tm, :] = h_ref[...]
        lhs_ref[halo + tm:, :] = jnp.where(pos == tiles_per_seq - 1, jnp.zeros_like(hn), hn)

    lhs = lhs_ref[...]
    n_rows = tm + 2 * halo

    def conv_branch(w_ref, cw_ref, cb_ref):
        u = jnp.dot(lhs, w_ref[...], preferred_element_type=F32)
        prev = pltpu.roll(u, 1, 0)[halo:halo + tm]
        nxt = pltpu.roll(u, n_rows - 1, 0)[halo:halo + tm]
        cur = u[halo:halo + tm]
        return prev * cw_ref[0:1, :] + cur * cw_ref[1:2, :] + nxt * cw_ref[2:3, :] + cb_ref[...]

    a = conv_branch(wa_ref, cwa_ref, cba_ref)
    b = conv_branch(wb_ref, cwb_ref, cbb_ref)
    o_ref[...] = (_silu(a) * b).astype(o_ref.dtype)


def _ffn_up(h2, w_up_bf16, conv_w, conv_b, *, seq, tm=FFN_ROWS, tf=FFN_TILE):
    t = h2.shape[0]
    halo = BF16_ROWS
    n_j = D_FF // tf
    halo_blocks = tm // halo
    last_halo_block = t // halo - 1
    kernel = functools.partial(_ffn_up_kernel, tiles_per_seq=seq // tm)
    return pl.pallas_call(
        kernel,
        grid=(t // tm, n_j),
        in_specs=[
            pl.BlockSpec((halo, D_MODEL), lambda i, j: (jnp.maximum(i * halo_blocks - 1, 0), 0)),
            pl.BlockSpec((tm, D_MODEL), lambda i, j: (i, 0)),
            pl.BlockSpec((halo, D_MODEL),
                         lambda i, j: (jnp.minimum((i + 1) * halo_blocks, last_halo_block), 0)),
            pl.BlockSpec((D_MODEL, tf), lambda i, j: (0, j)),
            pl.BlockSpec((D_MODEL, tf), lambda i, j: (0, n_j + j)),
            pl.BlockSpec((3, tf), lambda i, j: (0, j)),
            pl.BlockSpec((3, tf), lambda i, j: (0, n_j + j)),
            pl.BlockSpec((1, tf), lambda i, j: (0, j)),
            pl.BlockSpec((1, tf), lambda i, j: (0, n_j + j)),
        ],
        out_specs=pl.BlockSpec((None, tm, tf), lambda i, j: (j, i, 0)),
        out_shape=jax.ShapeDtypeStruct((n_j, t, tf), BF16),
        scratch_shapes=[pltpu.VMEM((tm + 2 * halo, D_MODEL), BF16)],
        compiler_params=pltpu.CompilerParams(
            dimension_semantics=("parallel", "arbitrary"),
            vmem_limit_bytes=VMEM_LIMIT),
        name="ffn_up",
    )(h2, h2, h2, w_up_bf16, w_up_bf16, conv_w, conv_w,
      conv_b.reshape(1, 2 * D_FF), conv_b.reshape(1, 2 * D_FF))


def _ffn_down_kernel(act_ref, w_ref, x2_ref, fw_ref, o_ref):
    act = jnp.concatenate([act_ref[j] for j in range(act_ref.shape[0])], axis=1)
    y = x2_ref[...] + jnp.dot(act, w_ref[...], preferred_element_type=F32)
    o_ref[...] = _rms(y, fw_ref[...])


def _ffn_down(act, w_down_bf16, x2, final_w, *, tm=512):
    n_j, t, tf = act.shape
    return pl.pallas_call(
        _ffn_down_kernel,
        grid=(t // tm,),
        in_specs=[
            pl.BlockSpec((n_j, tm, tf), lambda i: (0, i, 0)),
            pl.BlockSpec((D_FF, D_MODEL), lambda i: (0, 0), pipeline_mode=pl.Buffered(1)),
            pl.BlockSpec((tm, D_MODEL), lambda i: (i, 0)),
            pl.BlockSpec((1, D_MODEL), lambda i: (0, 0)),
        ],
        out_specs=pl.BlockSpec((tm, D_MODEL), lambda i: (i, 0)),
        out_shape=jax.ShapeDtypeStruct((t, D_MODEL), F32),
        compiler_params=pltpu.CompilerParams(
            dimension_semantics=("parallel",),
            vmem_limit_bytes=VMEM_LIMIT),
        name="ffn_down",
    )(act, w_down_bf16, x2, final_w.reshape(1, D_MODEL))


def _ffn(h2, x2, w_up_bf16, conv_w, conv_b, w_down_bf16, final_w, *, seq):
    act = _ffn_up(h2, w_up_bf16, conv_w, conv_b, seq=seq)
    return _ffn_down(act, w_down_bf16, x2, final_w)


def _rope_tables(seq):
    inv_freq = ROPE_BASE ** (-np.arange(0, HEAD_DIM, 2, dtype=np.float64) / HEAD_DIM)
    ang = np.arange(seq, dtype=np.float64)[:, None] * inv_freq[None, :]
    cos = np.cos(ang).astype(np.float32)
    sin = np.sin(ang).astype(np.float32)
    return (jnp.asarray(np.concatenate([cos, cos], axis=-1)),
            jnp.asarray(np.concatenate([-sin, sin], axis=-1)))


def _per_slab(vec, slabs):
    return jnp.broadcast_to(vec.reshape(slabs, 1, LANES), (slabs, 8, LANES))


def _trunk(x, p):
    batch, seq, _ = x.shape
    xt = x.reshape(batch * seq, D_MODEL)
    z = _in_proj(xt, p["norm1_w"], p["w_in"])
    ret = _retention(z, p["cos"], p["sin"], p["lgf"], p["lgb"], p["gnw"], batch=batch, seq=seq)
    x2, h2 = _out_proj(ret, z, p["sgu_nw"], p["w_s"], p["sgu_b"], xt, p["w_out"], p["norm2_w"])
    y = _ffn(h2, x2, p["w_up"], p["conv_w"], p["conv_b"], p["w_down"], p["final_w"], seq=seq)
    return y.reshape(batch, seq, D_MODEL)


def kernel(x_prompt, x_sample, norm1_w, w_in, ret_log_decay_fwd, ret_log_decay_bwd, ret_gn_w, sgu_norm_w, sgu_w_s, sgu_b, w_out, norm2_w, w_up, conv_w, conv_b, w_down, final_norm_w):
    max_seq = max(x_prompt.shape[1], x_sample.shape[1])
    cos, sin = _rope_tables(max_seq)
    p = {
        "norm1_w": norm1_w[0],
        "w_in": w_in[0].astype(BF16),
        "cos": cos,
        "sin": sin,
        "lgf": jnp.broadcast_to(ret_log_decay_fwd[0].astype(F32)[:, None, None], (HEADS, 8, LANES)),
        "lgb": jnp.broadcast_to(ret_log_decay_bwd[0].astype(F32)[:, None, None], (HEADS, 8, LANES)),
        "gnw": _per_slab(ret_gn_w[0], HEADS),
        "sgu_nw": _per_slab(sgu_norm_w[0], GROUPS),
        "w_s": sgu_w_s[0].astype(BF16),
        "sgu_b": jnp.broadcast_to(sgu_b[0][:, :, None], (GROUPS, CHUNK, LANES)),
        "w_out": w_out[0].astype(BF16),
        "norm2_w": norm2_w[0],
        "w_up": w_up[0].astype(BF16),
        "conv_w": conv_w[0],
        "conv_b": conv_b[0],
        "w_down": w_down[0].astype(BF16),
        "final_w": final_norm_w,
    }
    return (_trunk(x_prompt, p), _trunk(x_sample, p))
```

```python
import functools
import math

import jax
import jax.numpy as jnp
import numpy as np
from jax import lax
from jax.experimental import pallas as pl
from jax.experimental.pallas import tpu as pltpu

D_MODEL = 2048
CHUNK = 128
HEADS = 8
HEAD_DIM = 128
RET_WIDTH = HEADS * HEAD_DIM
GROUPS = 8
SGU_WIDTH = GROUPS * CHUNK
IN_COLS = 4 * RET_WIDTH + 2 * SGU_WIDTH
IN_SLABS = IN_COLS // 128
D_FF = 5632
ROPE_BASE = 10000.0
EPS = 1e-6
LANES = 128
BF16_ROWS = 16
IN_PROJ_TILE = 2048
FFN_TILE = 512
FFN_ROWS = 1024
FWD_GROUP = 8
VMEM_LIMIT = 56 * 1024 * 1024

F32 = jnp.float32
BF16 = jnp.bfloat16


def _rms(x, w):
    ms = jnp.mean(x * x, axis=-1, keepdims=True)
    return x * lax.rsqrt(ms + EPS) * w


def _gelu_tanh(x):
    c = math.sqrt(2.0 / math.pi)
    h = 0.5 * x
    return h + h * jnp.tanh(x * (c + (c * 0.044715) * (x * x)))


def _silu(x):
    h = 0.5 * x
    return h + h * jnp.tanh(h)


def _inproj_kernel(x_ref, nw_ref, w_ref, z_ref, h_ref):
    @pl.when(pl.program_id(1) == 0)
    def _():
        h_ref[...] = _rms(x_ref[...], nw_ref[...]).astype(BF16)

    res = jnp.dot(h_ref[...], w_ref[...], preferred_element_type=F32)
    for s in range(z_ref.shape[0]):
        z_ref[s] = res[:, s * LANES:(s + 1) * LANES].astype(z_ref.dtype)


def _in_proj(x, norm_w, w_bf16, *, tm=1024, tn=IN_PROJ_TILE):
    t = x.shape[0]
    return pl.pallas_call(
        _inproj_kernel,
        grid=(t // tm, IN_COLS // tn),
        in_specs=[
            pl.BlockSpec((tm, D_MODEL), lambda i, j: (i, 0)),
            pl.BlockSpec((1, D_MODEL), lambda i, j: (0, 0)),
            pl.BlockSpec((D_MODEL, tn), lambda i, j: (0, j)),
        ],
        out_specs=pl.BlockSpec((tn // LANES, tm, LANES), lambda i, j: (j, i, 0)),
        out_shape=jax.ShapeDtypeStruct((IN_SLABS, t, LANES), BF16),
        scratch_shapes=[pltpu.VMEM((tm, D_MODEL), BF16)],
        compiler_params=pltpu.CompilerParams(
            dimension_semantics=("parallel", "arbitrary"),
            vmem_limit_bytes=VMEM_LIMIT),
        name="in_proj",
    )(x, norm_w.reshape(1, D_MODEL), w_bf16)


def _ret_kernel(q_ref, k_ref, v_ref, g_ref, cos_ref, sin_ref, lgf_ref, lgb_ref, gnw_ref,
                o_ref, lhs_ref, kt_ref, kvf_ref, rb_ref):
    seq = q_ref.shape[1]
    n_chunks = seq // CHUNK
    lgf = lgf_ref[0, 0:1, :]
    lgb = lgb_ref[0, 0:1, :]
    row = lax.broadcasted_iota(jnp.int32, (CHUNK, CHUNK), 0).astype(F32)
    col = lax.broadcasted_iota(jnp.int32, (CHUNK, CHUNK), 1).astype(F32)
    diff = row - col
    scale = HEAD_DIM ** -0.5
    dcomb = jnp.where(diff >= 0, jnp.exp(lgf * diff), jnp.exp(lgb * (-diff))) * scale
    xi_f = jnp.exp(lgf * (row + 1.0)) * scale
    xi_b = jnp.exp(lgb * (CHUNK - row)) * scale
    zeta_f = jnp.exp(lgf * (CHUNK - 1.0 - col))
    zeta_b = jnp.exp(lgb * col)
    decay_f = jnp.exp(lgf * CHUNK)
    decay_b = jnp.exp(lgb * CHUNK)

    def rows(c):
        return pl.ds(pl.multiple_of(c * CHUNK, CHUNK), CHUNK)

    def prep(i, state):
        c = n_chunks - 1 - i
        sl = rows(c)
        q = q_ref[0, sl, :].astype(F32)
        k = k_ref[0, sl, :].astype(F32)
        cos = cos_ref[sl, :]
        sin = sin_ref[sl, :]
        qr = q * cos + pltpu.roll(q, HEAD_DIM // 2, 1) * sin
        kr = k * cos + pltpu.roll(k, HEAD_DIM // 2, 1) * sin
        kt = kr.T
        lhs_ref[sl, :] = jnp.concatenate([qr, qr * xi_f, qr * xi_b], axis=1).astype(BF16)
        kt_ref[c] = kt.astype(BF16)
        ktz = jnp.concatenate([kt * zeta_f, kt * zeta_b], axis=0).astype(BF16)
        kv = jnp.dot(ktz, v_ref[0, sl, :], preferred_element_type=F32)
        kvf_ref[c] = kv[:HEAD_DIM]
        rb_ref[c] = state.astype(BF16)
        return state * decay_b + kv[HEAD_DIM:]

    lax.fori_loop(0, n_chunks, prep, jnp.zeros((HEAD_DIM, HEAD_DIM), F32), unroll=4)

    gnw = gnw_ref[0, 0:1, :]

    def fwd(grp, state):
        cs = [grp * FWD_GROUP + k for k in range(FWD_GROUP)]
        sls = [rows(c) for c in cs]
        scores = [jnp.dot(lhs_ref[sl, 0:HEAD_DIM], kt_ref[c], preferred_element_type=F32)
                  for c, sl in zip(cs, sls)]
        states = []
        for c in cs:
            states.append(state.astype(BF16))
            state = state * decay_f + kvf_ref[c]
        outs = []
        for c, sl, s, st in zip(cs, sls, scores, states):
            lhs = jnp.concatenate([(s * dcomb).astype(BF16), lhs_ref[sl, HEAD_DIM:]], axis=1)
            rhs = jnp.concatenate([v_ref[0, sl, :], st, rb_ref[c]], axis=0)
            outs.append(jnp.dot(lhs, rhs, preferred_element_type=F32))
        for sl, o in zip(sls, outs):
            mu = jnp.mean(o, axis=-1, keepdims=True)
            d = o - mu
            var = jnp.mean(d * d, axis=-1, keepdims=True)
            on = d * lax.rsqrt(var + EPS) * gnw
            g = g_ref[0, sl, :].astype(F32)
            o_ref[0, sl, :] = (_silu(g) * on).astype(o_ref.dtype)
        return state

    lax.fori_loop(0, n_chunks // FWD_GROUP, fwd, jnp.zeros((HEAD_DIM, HEAD_DIM), F32))


def _retention(z, cos, sin, lgf, lgb, gnw, *, batch, seq):
    t = batch * seq
    n_chunks = seq // CHUNK

    def zspec(base):
        return pl.BlockSpec((1, seq, LANES), lambda b, h: (base + h, b, 0))

    table = pl.BlockSpec((seq, LANES), lambda b, h: (0, 0), pipeline_mode=pl.Buffered(1))
    per_head = pl.BlockSpec((1, 8, LANES), lambda b, h: (h, 0, 0))
    return pl.pallas_call(
        _ret_kernel,
        grid=(batch, HEADS),
        in_specs=[zspec(0), zspec(HEADS), zspec(2 * HEADS), zspec(3 * HEADS),
                  table, table, per_head, per_head, per_head],
        out_specs=pl.BlockSpec((1, seq, LANES), lambda b, h: (h, b, 0)),
        out_shape=jax.ShapeDtypeStruct((HEADS, t, LANES), BF16),
        scratch_shapes=[
            pltpu.VMEM((seq, 3 * HEAD_DIM), BF16),
            pltpu.VMEM((n_chunks, HEAD_DIM, CHUNK), BF16),
            pltpu.VMEM((n_chunks, HEAD_DIM, HEAD_DIM), F32),
            pltpu.VMEM((n_chunks, HEAD_DIM, HEAD_DIM), BF16),
        ],
        compiler_params=pltpu.CompilerParams(
            dimension_semantics=("parallel", "parallel"),
            vmem_limit_bytes=VMEM_LIMIT),
        name="retention",
    )(z, z, z, z, cos, sin, lgf, lgb, gnw)


def _sgu_pointwise(zu_ref, zv_ref, nw_ref, u_ref, vn_ref):
    v = _gelu_tanh(zv_ref[...].astype(F32))
    ms = jnp.sum(jnp.sum(v * v, axis=0), axis=-1, keepdims=True) * (1.0 / SGU_WIDTH)
    inv = lax.rsqrt(ms + EPS)
    for g in range(GROUPS):
        vn_ref[g] = (v[g] * inv * nw_ref[g, 0:1, :]).astype(BF16)
        u_ref[g] = _gelu_tanh(zu_ref[g].astype(F32))


def _sgu_mix(u_ref, vn_ref, ws_ref, b_ref):
    n_chunks = vn_ref.shape[1] // CHUNK
    groups = []
    for g in range(GROUPS):
        rhs = jnp.concatenate(
            [vn_ref[g, c * CHUNK:(c + 1) * CHUNK, :] for c in range(n_chunks)], axis=1)
        sp = jnp.dot(ws_ref[g], rhs, preferred_element_type=F32)
        bias = b_ref[g]
        groups.append(jnp.concatenate(
            [u_ref[g, c * CHUNK:(c + 1) * CHUNK, :] * (sp[:, c * CHUNK:(c + 1) * CHUNK] + bias)
             for c in range(n_chunks)], axis=0).astype(BF16))
    return jnp.concatenate(groups, axis=1)


def _outproj_kernel(ret_ref, zu_ref, zv_ref, sgu_nw_ref, ws_ref, sb_ref, x_ref, w_ref, nw_ref,
                    x2_ref, h2_ref, u_ref, vn_ref):
    _sgu_pointwise(zu_ref, zv_ref, sgu_nw_ref, u_ref, vn_ref)
    ret = jnp.concatenate([ret_ref[s] for s in range(HEADS)], axis=1)
    x2_ref[...] = x_ref[...] + jnp.dot(ret, w_ref[0:RET_WIDTH, :], preferred_element_type=F32)
    sgu = _sgu_mix(u_ref, vn_ref, ws_ref, sb_ref)
    x2 = x2_ref[...] + jnp.dot(sgu, w_ref[RET_WIDTH:, :], preferred_element_type=F32)
    x2_ref[...] = x2
    h2_ref[...] = _rms(x2, nw_ref[...]).astype(BF16)


def _out_proj(ret, z, sgu_nw, ws_bf16, sgu_bias, x, w_bf16, norm_w, *, tm=512):
    t = x.shape[0]
    u_base = 4 * HEADS // GROUPS
    const3 = lambda i: (0, 0, 0)
    return pl.pallas_call(
        _outproj_kernel,
        grid=(t // tm,),
        in_specs=[
            pl.BlockSpec((HEADS, tm, LANES), lambda i: (0, i, 0)),
            pl.BlockSpec((GROUPS, tm, LANES), lambda i: (u_base, i, 0)),
            pl.BlockSpec((GROUPS, tm, LANES), lambda i: (u_base + 1, i, 0)),
            pl.BlockSpec((GROUPS, 8, LANES), const3),
            pl.BlockSpec((GROUPS, CHUNK, CHUNK), const3),
            pl.BlockSpec((GROUPS, CHUNK, LANES), const3),
            pl.BlockSpec((tm, D_MODEL), lambda i: (i, 0)),
            pl.BlockSpec((D_MODEL, D_MODEL), lambda i: (0, 0)),
            pl.BlockSpec((1, D_MODEL), lambda i: (0, 0)),
        ],
        out_specs=[pl.BlockSpec((tm, D_MODEL), lambda i: (i, 0)),
                   pl.BlockSpec((tm, D_MODEL), lambda i: (i, 0))],
        out_shape=[jax.ShapeDtypeStruct((t, D_MODEL), F32),
                   jax.ShapeDtypeStruct((t, D_MODEL), BF16)],
        scratch_shapes=[pltpu.VMEM((GROUPS, tm, LANES), F32),
                        pltpu.VMEM((GROUPS, tm, LANES), BF16)],
        compiler_params=pltpu.CompilerParams(
            dimension_semantics=("parallel",),
            vmem_limit_bytes=VMEM_LIMIT),
        name="out_proj",
    )(ret, z, z, sgu_nw, ws_bf16, sgu_bias, x, w_bf16, norm_w.reshape(1, D_MODEL))


def _ffn_up_kernel(hp_ref, h_ref, hn_ref, wa_ref, wb_ref, cwa_ref, cwb_ref, cba_ref,
                   cbb_ref, o_ref, lhs_ref, *, tiles_per_seq):
    i = pl.program_id(0)
    j = pl.program_id(1)
    tm = h_ref.shape[0]
    halo = hp_ref.shape[0]

    @pl.when(j == 0)
    def _():
        pos = i % tiles_per_seq
        hp = hp_ref[...]
        hn = hn_ref[...]
        lhs_ref[0:halo, :] = jnp.where(pos == 0, jnp.zeros_like(hp), hp)
        lhs_ref[halo:halo + tm, :] = h_ref[...]
        lhs_ref[halo + tm:, :] = jnp.where(pos == tiles_per_seq - 1, jnp.zeros_like(hn), hn)

    lhs = lhs_ref[...]
    n_rows = tm + 2 * halo

    def conv_branch(w_ref, cw_ref, cb_ref):
        u = jnp.dot(lhs, w_ref[...], preferred_element_type=F32)
        prev = pltpu.roll(u, 1, 0)[halo:halo + tm]
        nxt = pltpu.roll(u, n_rows - 1, 0)[halo:halo + tm]
        cur = u[halo:halo + tm]
        return prev * cw_ref[0:1, :] + cur * cw_ref[1:2, :] + nxt * cw_ref[2:3, :] + cb_ref[...]

    a = conv_branch(wa_ref, cwa_ref, cba_ref)
    b = conv_branch(wb_ref, cwb_ref, cbb_ref)
    o_ref[...] = (_silu(a) * b).astype(o_ref.dtype)


def _ffn_up(h2, w_up_bf16, conv_w, conv_b, *, seq, tm=FFN_ROWS, tf=FFN_TILE):
    t = h2.shape[0]
    halo = BF16_ROWS
    n_j = D_FF // tf
    halo_blocks = tm // halo
    last_halo_block = t // halo - 1
    kernel = functools.partial(_ffn_up_kernel, tiles_per_seq=seq // tm)
    return pl.pallas_call(
        kernel,
        grid=(t // tm, n_j),
        in_specs=[
            pl.BlockSpec((halo, D_MODEL), lambda i, j: (jnp.maximum(i * halo_blocks - 1, 0), 0)),
            pl.BlockSpec((tm, D_MODEL), lambda i, j: (i, 0)),
            pl.BlockSpec((halo, D_MODEL),
                         lambda i, j: (jnp.minimum((i + 1) * halo_blocks, last_halo_block), 0)),
            pl.BlockSpec((D_MODEL, tf), lambda i, j: (0, j)),
            pl.BlockSpec((D_MODEL, tf), lambda i, j: (0, n_j + j)),
            pl.BlockSpec((3, tf), lambda i, j: (0, j)),
            pl.BlockSpec((3, tf), lambda i, j: (0, n_j + j)),
            pl.BlockSpec((1, tf), lambda i, j: (0, j)),
            pl.BlockSpec((1, tf), lambda i, j: (0, n_j + j)),
        ],
        out_specs=pl.BlockSpec((None, tm, tf), lambda i, j: (j, i, 0)),
        out_shape=jax.ShapeDtypeStruct((n_j, t, tf), BF16),
        scratch_shapes=[pltpu.VMEM((tm + 2 * halo, D_MODEL), BF16)],
        compiler_params=pltpu.CompilerParams(
            dimension_semantics=("parallel", "arbitrary"),
            vmem_limit_bytes=VMEM_LIMIT),
        name="ffn_up",
    )(h2, h2, h2, w_up_bf16, w_up_bf16, conv_w, conv_w,
      conv_b.reshape(1, 2 * D_FF), conv_b.reshape(1, 2 * D_FF))


def _ffn_down_kernel(act_ref, w_ref, x2_ref, fw_ref, o_ref):
    act = jnp.concatenate([act_ref[j] for j in range(act_ref.shape[0])], axis=1)
    y = x2_ref[...] + jnp.dot(act, w_ref[...], preferred_element_type=F32)
    o_ref[...] = _rms(y, fw_ref[...])


def _ffn_down(act, w_down_bf16, x2, final_w, *, tm=512):
    n_j, t, tf = act.shape
    return pl.pallas_call(
        _ffn_down_kernel,
        grid=(t // tm,),
        in_specs=[
            pl.BlockSpec((n_j, tm, tf), lambda i: (0, i, 0)),
            pl.BlockSpec((D_FF, D_MODEL), lambda i: (0, 0), pipeline_mode=pl.Buffered(1)),
            pl.BlockSpec((tm, D_MODEL), lambda i: (i, 0)),
            pl.BlockSpec((1, D_MODEL), lambda i: (0, 0)),
        ],
        out_specs=pl.BlockSpec((tm, D_MODEL), lambda i: (i, 0)),
        out_shape=jax.ShapeDtypeStruct((t, D_MODEL), F32),
        compiler_params=pltpu.CompilerParams(
            dimension_semantics=("parallel",),
            vmem_limit_bytes=VMEM_LIMIT),
        name="ffn_down",
    )(act, w_down_bf16, x2, final_w.reshape(1, D_MODEL))


def _ffn(h2, x2, w_up_bf16, conv_w, conv_b, w_down_bf16, final_w, *, seq):
    act = _ffn_up(h2, w_up_bf16, conv_w, conv_b, seq=seq)
    return _ffn_down(act, w_down_bf16, x2, final_w)


def _rope_tables(seq):
    inv_freq = ROPE_BASE ** (-np.arange(0, HEAD_DIM, 2, dtype=np.float64) / HEAD_DIM)
    ang = np.arange(seq, dtype=np.float64)[:, None] * inv_freq[None, :]
    cos = np.cos(ang).astype(np.float32)
    sin = np.sin(ang).astype(np.float32)
    return (jnp.asarray(np.concatenate([cos, cos], axis=-1)),
            jnp.asarray(np.concatenate([-sin, sin], axis=-1)))


def _per_slab(vec, slabs):
    return jnp.broadcast_to(vec.reshape(slabs, 1, LANES), (slabs, 8, LANES))


def _trunk(x, p):
    batch, seq, _ = x.shape
    xt = x.reshape(batch * seq, D_MODEL)
    z = _in_proj(xt, p["norm1_w"], p["w_in"])
    ret = _retention(z, p["cos"], p["sin"], p["lgf"], p["lgb"], p["gnw"], batch=batch, seq=seq)
    x2, h2 = _out_proj(ret, z, p["sgu_nw"], p["w_s"], p["sgu_b"], xt, p["w_out"], p["norm2_w"])
    y = _ffn(h2, x2, p["w_up"], p["conv_w"], p["conv_b"], p["w_down"], p["final_w"], seq=seq)
    return y.reshape(batch, seq, D_MODEL)


def kernel(x_prompt, x_sample, norm1_w, w_in, ret_log_decay_fwd, ret_log_decay_bwd, ret_gn_w, sgu_norm_w, sgu_w_s, sgu_b, w_out, norm2_w, w_up, conv_w, conv_b, w_down, final_norm_w):
    max_seq = max(x_prompt.shape[1], x_sample.shape[1])
    cos, sin = _rope_tables(max_seq)
    p = {
        "norm1_w": norm1_w[0],
        "w_in": w_in[0].astype(BF16),
        "cos": cos,
        "sin": sin,
        "lgf": jnp.broadcast_to(ret_log_decay_fwd[0].astype(F32)[:, None, None], (HEADS, 8, LANES)),
        "lgb": jnp.broadcast_to(ret_log_decay_bwd[0].astype(F32)[:, None, None], (HEADS, 8, LANES)),
        "gnw": _per_slab(ret_gn_w[0], HEADS),
        "sgu_nw": _per_slab(sgu_norm_w[0], GROUPS),
        "w_s": sgu_w_s[0].astype(BF16),
        "sgu_b": jnp.broadcast_to(sgu_b[0][:, :, None], (GROUPS, CHUNK, LANES)),
        "w_out": w_out[0].astype(BF16),
        "norm2_w": norm2_w[0],
        "w_up": w_up[0].astype(BF16),
        "conv_w": conv_w[0],
        "conv_b": conv_b[0],
        "w_down": w_down[0].astype(BF16),
        "final_w": final_norm_w,
    }
    return (_trunk(x_prompt, p), _trunk(x_sample, p))
```

```python
import functools
import math

import jax
import jax.numpy as jnp
import numpy as np
from jax import lax
from jax.experimental import pallas as pl
from jax.experimental.pallas import tpu as pltpu

D_MODEL = 2048
CHUNK = 128
HEADS = 8
HEAD_DIM = 128
RET_WIDTH = HEADS * HEAD_DIM
GROUPS = 8
SGU_WIDTH = GROUPS * CHUNK
IN_COLS = 4 * RET_WIDTH + 2 * SGU_WIDTH
IN_SLABS = IN_COLS // 128
D_FF = 5632
ROPE_BASE = 10000.0
EPS = 1e-6
LANES = 128
BF16_ROWS = 16
IN_PROJ_TILE = 2048
FFN_TILE = 512
FFN_ROWS = 1024
VMEM_LIMIT = 56 * 1024 * 1024

F32 = jnp.float32
BF16 = jnp.bfloat16


def _rms(x, w):
    ms = jnp.mean(x * x, axis=-1, keepdims=True)
    return x * lax.rsqrt(ms + EPS) * w


def _gelu_tanh(x):
    c = math.sqrt(2.0 / math.pi)
    h = 0.5 * x
    return h + h * jnp.tanh(x * (c + (c * 0.044715) * (x * x)))


def _silu(x):
    h = 0.5 * x
    return h + h * jnp.tanh(h)


def _slab_epilogue(slab):
    if slab < 2 * HEADS:
        return "rotary"
    if slab < 3 * HEADS:
        return "none"
    if slab < 4 * HEADS:
        return "silu"
    return "gelu"


def _inproj_kernel(x_ref, nw_ref, w_ref, cos_ref, sin_ref, z_ref, h_ref):
    j = pl.program_id(1)
    n_slabs = z_ref.shape[0]

    @pl.when(j == 0)
    def _():
        h_ref[...] = _rms(x_ref[...], nw_ref[...]).astype(BF16)

    def column_tile(jj):
        res = jnp.dot(h_ref[...], w_ref[...], preferred_element_type=F32)
        for s in range(n_slabs):
            y = res[:, s * LANES:(s + 1) * LANES]
            kind = _slab_epilogue(jj * n_slabs + s)
            if kind == "rotary":
                y = y * cos_ref[...] + pltpu.roll(y, HEAD_DIM // 2, 1) * sin_ref[...]
            elif kind == "silu":
                y = _silu(y)
            elif kind == "gelu":
                y = _gelu_tanh(y)
            z_ref[s] = y.astype(z_ref.dtype)

    for jj in range(IN_SLABS // n_slabs):
        pl.when(j == jj)(functools.partial(column_tile, jj))


def _in_proj(x, norm_w, w_bf16, cos, sin, *, seq, tm=1024, tn=IN_PROJ_TILE):
    t = x.shape[0]
    tiles_per_seq = seq // tm
    return pl.pallas_call(
        _inproj_kernel,
        grid=(t // tm, IN_COLS // tn),
        in_specs=[
            pl.BlockSpec((tm, D_MODEL), lambda i, j: (i, 0)),
            pl.BlockSpec((1, D_MODEL), lambda i, j: (0, 0)),
            pl.BlockSpec((D_MODEL, tn), lambda i, j: (0, j)),
            pl.BlockSpec((tm, LANES), lambda i, j: (i % tiles_per_seq, 0)),
            pl.BlockSpec((tm, LANES), lambda i, j: (i % tiles_per_seq, 0)),
        ],
        out_specs=pl.BlockSpec((tn // LANES, tm, LANES), lambda i, j: (j, i, 0)),
        out_shape=jax.ShapeDtypeStruct((IN_SLABS, t, LANES), BF16),
        scratch_shapes=[pltpu.VMEM((tm, D_MODEL), BF16)],
        compiler_params=pltpu.CompilerParams(
            dimension_semantics=("parallel", "arbitrary"),
            vmem_limit_bytes=VMEM_LIMIT),
        name="in_proj",
    )(x, norm_w.reshape(1, D_MODEL), w_bf16, cos, sin)


def _ret_kernel(q_ref, k_ref, v_ref, g_ref, lgf_ref, lgb_ref, gnw_ref,
                o_ref, qx_ref, kt_ref, kvf_ref, rb_ref):
    seq = q_ref.shape[1]
    n_chunks = seq // CHUNK
    lgf = lgf_ref[0, 0:1, :]
    lgb = lgb_ref[0, 0:1, :]
    row = lax.broadcasted_iota(jnp.int32, (CHUNK, CHUNK), 0).astype(F32)
    col = lax.broadcasted_iota(jnp.int32, (CHUNK, CHUNK), 1).astype(F32)
    diff = row - col
    scale = HEAD_DIM ** -0.5
    dcomb = jnp.where(diff >= 0, jnp.exp(lgf * diff), jnp.exp(lgb * (-diff))) * scale
    xi_f = jnp.exp(lgf * (row + 1.0)) * scale
    xi_b = jnp.exp(lgb * (CHUNK - row)) * scale
    zeta_f = jnp.exp(lgf * (CHUNK - 1.0 - col))
    zeta_b = jnp.exp(lgb * col)
    decay_f = jnp.exp(lgf * CHUNK)
    decay_b = jnp.exp(lgb * CHUNK)

    def rows(c):
        return pl.ds(pl.multiple_of(c * CHUNK, CHUNK), CHUNK)

    def prep(i, state):
        c = n_chunks - 1 - i
        sl = rows(c)
        q = q_ref[0, sl, :].astype(F32)
        kt = k_ref[0, sl, :].astype(F32).T
        qx_ref[sl, :] = jnp.concatenate([q * xi_f, q * xi_b], axis=1).astype(BF16)
        kt_ref[c] = kt.astype(BF16)
        ktz = jnp.concatenate([kt * zeta_f, kt * zeta_b], axis=0).astype(BF16)
        kv = jnp.dot(ktz, v_ref[0, sl, :], preferred_element_type=F32)
        kvf_ref[c] = kv[:HEAD_DIM]
        rb_ref[c] = state.astype(BF16)
        return state * decay_b + kv[HEAD_DIM:]

    lax.fori_loop(0, n_chunks, prep, jnp.zeros((HEAD_DIM, HEAD_DIM), F32), unroll=4)

    gnw = gnw_ref[0, 0:1, :]

    def fwd(c, state):
        sl = rows(c)
        s = jnp.dot(q_ref[0, sl, :], kt_ref[c], preferred_element_type=F32) * dcomb
        lhs = jnp.concatenate([s.astype(BF16), qx_ref[sl, :]], axis=1)
        rhs = jnp.concatenate([v_ref[0, sl, :], state.astype(BF16), rb_ref[c]], axis=0)
        o = jnp.dot(lhs, rhs, preferred_element_type=F32)
        mu = jnp.mean(o, axis=-1, keepdims=True)
        d = o - mu
        var = jnp.mean(d * d, axis=-1, keepdims=True)
        on = d * lax.rsqrt(var + EPS) * gnw
        o_ref[0, sl, :] = (g_ref[0, sl, :].astype(F32) * on).astype(o_ref.dtype)
        return state * decay_f + kvf_ref[c]

    lax.fori_loop(0, n_chunks, fwd, jnp.zeros((HEAD_DIM, HEAD_DIM), F32), unroll=8)


def _retention(z, lgf, lgb, gnw, *, batch, seq):
    t = batch * seq
    n_chunks = seq // CHUNK

    def zspec(base):
        return pl.BlockSpec((1, seq, LANES), lambda b, h: (base + h, b, 0))

    per_head = pl.BlockSpec((1, 8, LANES), lambda b, h: (h, 0, 0))
    return pl.pallas_call(
        _ret_kernel,
        grid=(batch, HEADS),
        in_specs=[zspec(0), zspec(HEADS), zspec(2 * HEADS), zspec(3 * HEADS),
                  per_head, per_head, per_head],
        out_specs=pl.BlockSpec((1, seq, LANES), lambda b, h: (h, b, 0)),
        out_shape=jax.ShapeDtypeStruct((HEADS, t, LANES), BF16),
        scratch_shapes=[
            pltpu.VMEM((seq, 2 * HEAD_DIM), BF16),
            pltpu.VMEM((n_chunks, HEAD_DIM, CHUNK), BF16),
            pltpu.VMEM((n_chunks, HEAD_DIM, HEAD_DIM), F32),
            pltpu.VMEM((n_chunks, HEAD_DIM, HEAD_DIM), BF16),
        ],
        compiler_params=pltpu.CompilerParams(
            dimension_semantics=("parallel", "parallel"),
            vmem_limit_bytes=VMEM_LIMIT),
        name="retention",
    )(z, z, z, z, lgf, lgb, gnw)


def _sgu_pointwise(zu_ref, zv_ref, nw_ref, u_ref, vn_ref):
    v = zv_ref[...].astype(F32)
    ms = jnp.sum(jnp.sum(v * v, axis=0), axis=-1, keepdims=True) * (1.0 / SGU_WIDTH)
    inv = lax.rsqrt(ms + EPS)
    for g in range(GROUPS):
        vn_ref[g] = (v[g] * inv * nw_ref[g, 0:1, :]).astype(BF16)
        u_ref[g] = zu_ref[g].astype(F32)


def _sgu_mix(u_ref, vn_ref, ws_ref, b_ref):
    n_chunks = vn_ref.shape[1] // CHUNK
    groups = []
    for g in range(GROUPS):
        rhs = jnp.concatenate(
            [vn_ref[g, c * CHUNK:(c + 1) * CHUNK, :] for c in range(n_chunks)], axis=1)
        sp = jnp.dot(ws_ref[g], rhs, preferred_element_type=F32)
        bias = b_ref[g]
        groups.append(jnp.concatenate(
            [u_ref[g, c * CHUNK:(c + 1) * CHUNK, :] * (sp[:, c * CHUNK:(c + 1) * CHUNK] + bias)
             for c in range(n_chunks)], axis=0).astype(BF16))
    return jnp.concatenate(groups, axis=1)


def _outproj_kernel(ret_ref, zu_ref, zv_ref, sgu_nw_ref, ws_ref, sb_ref, x_ref, w_ref, nw_ref,
                    x2_ref, h2_ref, u_ref, vn_ref):
    _sgu_pointwise(zu_ref, zv_ref, sgu_nw_ref, u_ref, vn_ref)
    ret = jnp.concatenate([ret_ref[s] for s in range(HEADS)], axis=1)
    x2_ref[...] = x_ref[...] + jnp.dot(ret, w_ref[0:RET_WIDTH, :], preferred_element_type=F32)
    sgu = _sgu_mix(u_ref, vn_ref, ws_ref, sb_ref)
    x2 = x2_ref[...] + jnp.dot(sgu, w_ref[RET_WIDTH:, :], preferred_element_type=F32)
    x2_ref[...] = x2
    h2_ref[...] = _rms(x2, nw_ref[...]).astype(BF16)


def _out_proj(ret, z, sgu_nw, ws_bf16, sgu_bias, x, w_bf16, norm_w, *, tm=512):
    t = x.shape[0]
    u_base = 4 * HEADS // GROUPS
    const3 = lambda i: (0, 0, 0)
    return pl.pallas_call(
        _outproj_kernel,
        grid=(t // tm,),
        in_specs=[
            pl.BlockSpec((HEADS, tm, LANES), lambda i: (0, i, 0)),
            pl.BlockSpec((GROUPS, tm, LANES), lambda i: (u_base, i, 0)),
            pl.BlockSpec((GROUPS, tm, LANES), lambda i: (u_base + 1, i, 0)),
            pl.BlockSpec((GROUPS, 8, LANES), const3),
            pl.BlockSpec((GROUPS, CHUNK, CHUNK), const3),
            pl.BlockSpec((GROUPS, CHUNK, LANES), const3),
            pl.BlockSpec((tm, D_MODEL), lambda i: (i, 0)),
            pl.BlockSpec((D_MODEL, D_MODEL), lambda i: (0, 0)),
            pl.BlockSpec((1, D_MODEL), lambda i: (0, 0)),
        ],
        out_specs=[pl.BlockSpec((tm, D_MODEL), lambda i: (i, 0)),
                   pl.BlockSpec((tm, D_MODEL), lambda i: (i, 0))],
        out_shape=[jax.ShapeDtypeStruct((t, D_MODEL), F32),
                   jax.ShapeDtypeStruct((t, D_MODEL), BF16)],
        scratch_shapes=[pltpu.VMEM((GROUPS, tm, LANES), F32),
                        pltpu.VMEM((GROUPS, tm, LANES), BF16)],
        compiler_params=pltpu.CompilerParams(
            dimension_semantics=("parallel",),
            vmem_limit_bytes=VMEM_LIMIT),
        name="out_proj",
    )(ret, z, z, sgu_nw, ws_bf16, sgu_bias, x, w_bf16, norm_w.reshape(1, D_MODEL))


def _ffn_up_kernel(hp_ref, h_ref, hn_ref, wa_ref, wb_ref, cwa_ref, cwb_ref, cba_ref,
                   cbb_ref, o_ref, lhs_ref, *, tiles_per_seq):
    i = pl.program_id(0)
    j = pl.program_id(1)
    tm = h_ref.shape[0]
    halo = hp_ref.shape[0]

    @pl.when(j == 0)
    def _():
        pos = i % tiles_per_seq
        hp = hp_ref[...]
        hn = hn_ref[...]
        lhs_ref[0:halo, :] = jnp.where(pos == 0, jnp.zeros_like(hp), hp)
        lhs_ref[halo:halo + tm, :] = h_ref[...]
        lhs_ref[halo + tm:, :] = jnp.where(pos == tiles_per_seq - 1, jnp.zeros_like(hn), hn)

    lhs = lhs_ref[...]
    n_rows = tm + 2 * halo

    def conv_branch(w_ref, cw_ref, cb_ref):
        u = jnp.dot(lhs, w_ref[...], preferred_element_type=F32)
        prev = pltpu.roll(u, 1, 0)[halo:halo + tm]
        nxt = pltpu.roll(u, n_rows - 1, 0)[halo:halo + tm]
        cur = u[halo:halo + tm]
        return prev * cw_ref[0:1, :] + cur * cw_ref[1:2, :] + nxt * cw_ref[2:3, :] + cb_ref[...]

    a = conv_branch(wa_ref, cwa_ref, cba_ref)
    b = conv_branch(wb_ref, cwb_ref, cbb_ref)
    o_ref[...] = (_silu(a) * b).astype(o_ref.dtype)


def _ffn_up(h2, w_up_bf16, conv_w, conv_b, *, seq, tm=FFN_ROWS, tf=FFN_TILE):
    t = h2.shape[0]
    halo = BF16_ROWS
    n_j = D_FF // tf
    halo_blocks = tm // halo
    last_halo_block = t // halo - 1
    kernel = functools.partial(_ffn_up_kernel, tiles_per_seq=seq // tm)
    return pl.pallas_call(
        kernel,
        grid=(t // tm, n_j),
        in_specs=[
            pl.BlockSpec((halo, D_MODEL), lambda i, j: (jnp.maximum(i * halo_blocks - 1, 0), 0)),
            pl.BlockSpec((tm, D_MODEL), lambda i, j: (i, 0)),
            pl.BlockSpec((halo, D_MODEL),
                         lambda i, j: (jnp.minimum((i + 1) * halo_blocks, last_halo_block), 0)),
            pl.BlockSpec((D_MODEL, tf), lambda i, j: (0, j)),
            pl.BlockSpec((D_MODEL, tf), lambda i, j: (0, n_j + j)),
            pl.BlockSpec((3, tf), lambda i, j: (0, j)),
            pl.BlockSpec((3, tf), lambda i, j: (0, n_j + j)),
            pl.BlockSpec((1, tf), lambda i, j: (0, j)),
            pl.BlockSpec((1, tf), lambda i, j: (0, n_j + j)),
        ],
        out_specs=pl.BlockSpec((None, tm, tf), lambda i, j: (j, i, 0)),
        out_shape=jax.ShapeDtypeStruct((n_j, t, tf), BF16),
        scratch_shapes=[pltpu.VMEM((tm + 2 * halo, D_MODEL), BF16)],
        compiler_params=pltpu.CompilerParams(
            dimension_semantics=("parallel", "arbitrary"),
            vmem_limit_bytes=VMEM_LIMIT),
        name="ffn_up",
    )(h2, h2, h2, w_up_bf16, w_up_bf16, conv_w, conv_w,
      conv_b.reshape(1, 2 * D_FF), conv_b.reshape(1, 2 * D_FF))


def _ffn_down_kernel(act_ref, w_ref, x2_ref, fw_ref, o_ref):
    act = jnp.concatenate([act_ref[j] for j in range(act_ref.shape[0])], axis=1)
    y = x2_ref[...] + jnp.dot(act, w_ref[...], preferred_element_type=F32)
    o_ref[...] = _rms(y, fw_ref[...])


def _ffn_down(act, w_down_bf16, x2, final_w, *, tm=512):
    n_j, t, tf = act.shape
    return pl.pallas_call(
        _ffn_down_kernel,
        grid=(t // tm,),
        in_specs=[
            pl.BlockSpec((n_j, tm, tf), lambda i: (0, i, 0)),
            pl.BlockSpec((D_FF, D_MODEL), lambda i: (0, 0), pipeline_mode=pl.Buffered(1)),
            pl.BlockSpec((tm, D_MODEL), lambda i: (i, 0)),
            pl.BlockSpec((1, D_MODEL), lambda i: (0, 0)),
        ],
        out_specs=pl.BlockSpec((tm, D_MODEL), lambda i: (i, 0)),
        out_shape=jax.ShapeDtypeStruct((t, D_MODEL), F32),
        compiler_params=pltpu.CompilerParams(
            dimension_semantics=("parallel",),
            vmem_limit_bytes=VMEM_LIMIT),
        name="ffn_down",
    )(act, w_down_bf16, x2, final_w.reshape(1, D_MODEL))


def _ffn(h2, x2, w_up_bf16, conv_w, conv_b, w_down_bf16, final_w, *, seq):
    act = _ffn_up(h2, w_up_bf16, conv_w, conv_b, seq=seq)
    return _ffn_down(act, w_down_bf16, x2, final_w)


def _rope_tables(seq):
    inv_freq = ROPE_BASE ** (-np.arange(0, HEAD_DIM, 2, dtype=np.float64) / HEAD_DIM)
    ang = np.arange(seq, dtype=np.float64)[:, None] * inv_freq[None, :]
    cos = np.cos(ang).astype(np.float32)
    sin = np.sin(ang).astype(np.float32)
    return (jnp.asarray(np.concatenate([cos, cos], axis=-1)),
            jnp.asarray(np.concatenate([-sin, sin], axis=-1)))


def _per_slab(vec, slabs):
    return jnp.broadcast_to(vec.reshape(slabs, 1, LANES), (slabs, 8, LANES))


def _trunk(x, p):
    batch, seq, _ = x.shape
    xt = x.reshape(batch * seq, D_MODEL)
    z = _in_proj(xt, p["norm1_w"], p["w_in"], p["cos"], p["sin"], seq=seq)
    ret = _retention(z, p["lgf"], p["lgb"], p["gnw"], batch=batch, seq=seq)
    x2, h2 = _out_proj(ret, z, p["sgu_nw"], p["w_s"], p["sgu_b"], xt, p["w_out"], p["norm2_w"])
    y = _ffn(h2, x2, p["w_up"], p["conv_w"], p["conv_b"], p["w_down"], p["final_w"], seq=seq)
    return y.reshape(batch, seq, D_MODEL)


def kernel(x_prompt, x_sample, norm1_w, w_in, ret_log_decay_fwd, ret_log_decay_bwd, ret_gn_w, sgu_norm_w, sgu_w_s, sgu_b, w_out, norm2_w, w_up, conv_w, conv_b, w_down, final_norm_w):
    max_seq = max(x_prompt.shape[1], x_sample.shape[1])
    cos, sin = _rope_tables(max_seq)
    p = {
        "norm1_w": norm1_w[0],
        "w_in": w_in[0].astype(BF16),
        "cos": cos,
        "sin": sin,
        "lgf": jnp.broadcast_to(ret_log_decay_fwd[0].astype(F32)[:, None, None], (HEADS, 8, LANES)),
        "lgb": jnp.broadcast_to(ret_log_decay_bwd[0].astype(F32)[:, None, None], (HEADS, 8, LANES)),
        "gnw": _per_slab(ret_gn_w[0], HEADS),
        "sgu_nw": _per_slab(sgu_norm_w[0], GROUPS),
        "w_s": sgu_w_s[0].astype(BF16),
        "sgu_b": jnp.broadcast_to(sgu_b[0][:, :, None], (GROUPS, CHUNK, LANES)),
        "w_out": w_out[0].astype(BF16),
        "norm2_w": norm2_w[0],
        "w_up": w_up[0].astype(BF16),
        "conv_w": conv_w[0],
        "conv_b": conv_b[0],
        "w_down": w_down[0].astype(BF16),
        "final_w": final_norm_w,
    }
    return (_trunk(x_prompt, p), _trunk(x_sample, p))
```

```python
import functools
import math

import jax
import jax.numpy as jnp
import numpy as np
from jax import lax
from jax.experimental import pallas as pl
from jax.experimental.pallas import tpu as pltpu

D_MODEL = 2048
CHUNK = 128
HEADS = 8
HEAD_DIM = 128
RET_WIDTH = HEADS * HEAD_DIM
GROUPS = 8
SGU_WIDTH = GROUPS * CHUNK
IN_COLS = 4 * RET_WIDTH + 2 * SGU_WIDTH
IN_SLABS = IN_COLS // 128
D_FF = 5632
ROPE_BASE = 10000.0
EPS = 1e-6
LANES = 128
BF16_ROWS = 16
IN_PROJ_TILE = 2048
FFN_TILE = 512
FFN_ROWS = 1024
VMEM_LIMIT = 56 * 1024 * 1024

F32 = jnp.float32
BF16 = jnp.bfloat16


def _rms(x, w):
    ms = jnp.mean(x * x, axis=-1, keepdims=True)
    return x * lax.rsqrt(ms + EPS) * w


def _gelu_tanh(x):
    c = math.sqrt(2.0 / math.pi)
    h = 0.5 * x
    return h + h * jnp.tanh(x * (c + (c * 0.044715) * (x * x)))


def _silu(x):
    h = 0.5 * x
    return h + h * jnp.tanh(h)


def _slab_epilogue(slab):
    if slab < 2 * HEADS:
        return "rotary"
    if slab < 3 * HEADS:
        return "none"
    if slab < 4 * HEADS:
        return "silu"
    return "gelu"


def _inproj_kernel(x_ref, nw_ref, w_ref, cos_ref, sin_ref, z_ref, h_ref):
    j = pl.program_id(1)
    n_slabs = z_ref.shape[0]

    @pl.when(j == 0)
    def _():
        h_ref[...] = _rms(x_ref[...], nw_ref[...]).astype(BF16)

    def column_tile(jj):
        res = jnp.dot(h_ref[...], w_ref[...], preferred_element_type=F32)
        for s in range(n_slabs):
            y = res[:, s * LANES:(s + 1) * LANES]
            kind = _slab_epilogue(jj * n_slabs + s)
            if kind == "rotary":
                y = y * cos_ref[...] + pltpu.roll(y, HEAD_DIM // 2, 1) * sin_ref[...]
            elif kind == "silu":
                y = _silu(y)
            elif kind == "gelu":
                y = _gelu_tanh(y)
            z_ref[s] = y.astype(z_ref.dtype)

    for jj in range(IN_SLABS // n_slabs):
        pl.when(j == jj)(functools.partial(column_tile, jj))


def _in_proj(x, norm_w, w_bf16, cos, sin, *, seq, tm=1024, tn=IN_PROJ_TILE):
    t = x.shape[0]
    tiles_per_seq = seq // tm
    return pl.pallas_call(
        _inproj_kernel,
        grid=(t // tm, IN_COLS // tn),
        in_specs=[
            pl.BlockSpec((tm, D_MODEL), lambda i, j: (i, 0)),
            pl.BlockSpec((1, D_MODEL), lambda i, j: (0, 0)),
            pl.BlockSpec((D_MODEL, tn), lambda i, j: (0, j)),
            pl.BlockSpec((tm, LANES), lambda i, j: (i % tiles_per_seq, 0)),
            pl.BlockSpec((tm, LANES), lambda i, j: (i % tiles_per_seq, 0)),
        ],
        out_specs=pl.BlockSpec((tn // LANES, tm, LANES), lambda i, j: (j, i, 0)),
        out_shape=jax.ShapeDtypeStruct((IN_SLABS, t, LANES), BF16),
        scratch_shapes=[pltpu.VMEM((tm, D_MODEL), BF16)],
        compiler_params=pltpu.CompilerParams(
            dimension_semantics=("parallel", "arbitrary"),
            vmem_limit_bytes=VMEM_LIMIT),
        name="in_proj",
    )(x, norm_w.reshape(1, D_MODEL), w_bf16, cos, sin)


def _ret_kernel(q_ref, k_ref, v_ref, g_ref, lgf_ref, lgb_ref, gnw_ref,
                o_ref, qx_ref, kt_ref, kvf_ref, rb_ref):
    seq = q_ref.shape[1]
    n_chunks = seq // CHUNK
    lgf = lgf_ref[0, 0:1, :]
    lgb = lgb_ref[0, 0:1, :]
    row = lax.broadcasted_iota(jnp.int32, (CHUNK, CHUNK), 0).astype(F32)
    col = lax.broadcasted_iota(jnp.int32, (CHUNK, CHUNK), 1).astype(F32)
    diff = row - col
    scale = HEAD_DIM ** -0.5
    dcomb = jnp.where(diff >= 0, jnp.exp(lgf * diff), jnp.exp(lgb * (-diff))) * scale
    xi_f = jnp.exp(lgf * (row + 1.0)) * scale
    xi_b = jnp.exp(lgb * (CHUNK - row)) * scale
    zeta_f = jnp.exp(lgf * (CHUNK - 1.0 - row))
    zeta_b = jnp.exp(lgb * row)
    decay_f = jnp.exp(lgf * CHUNK)
    decay_b = jnp.exp(lgb * CHUNK)

    def rows(c):
        return pl.ds(pl.multiple_of(c * CHUNK, CHUNK), CHUNK)

    def prep(i, state):
        c = n_chunks - 1 - i
        sl = rows(c)
        q = q_ref[0, sl, :].astype(F32)
        k = k_ref[0, sl, :].astype(F32)
        kt_ref[c] = k_ref[0, sl, :].T
        qx_ref[sl, :] = jnp.concatenate([q * xi_f, q * xi_b], axis=1).astype(BF16)
        kz = jnp.concatenate([k * zeta_f, k * zeta_b], axis=1).astype(BF16)
        kv = lax.dot_general(kz, v_ref[0, sl, :], (((0,), (0,)), ((), ())),
                             preferred_element_type=F32)
        kvf_ref[c] = kv[:HEAD_DIM]
        rb_ref[c] = state.astype(BF16)
        return state * decay_b + kv[HEAD_DIM:]

    lax.fori_loop(0, n_chunks, prep, jnp.zeros((HEAD_DIM, HEAD_DIM), F32), unroll=4)

    gnw = gnw_ref[0, 0:1, :]

    def fwd(c, state):
        sl = rows(c)
        s = jnp.dot(q_ref[0, sl, :], kt_ref[c], preferred_element_type=F32) * dcomb
        lhs = jnp.concatenate([s.astype(BF16), qx_ref[sl, :]], axis=1)
        rhs = jnp.concatenate([v_ref[0, sl, :], state.astype(BF16), rb_ref[c]], axis=0)
        o = jnp.dot(lhs, rhs, preferred_element_type=F32)
        mu = jnp.mean(o, axis=-1, keepdims=True)
        d = o - mu
        var = jnp.mean(d * d, axis=-1, keepdims=True)
        on = d * lax.rsqrt(var + EPS) * gnw
        o_ref[0, sl, :] = (g_ref[0, sl, :].astype(F32) * on).astype(o_ref.dtype)
        return state * decay_f + kvf_ref[c]

    lax.fori_loop(0, n_chunks, fwd, jnp.zeros((HEAD_DIM, HEAD_DIM), F32), unroll=8)


def _retention(z, lgf, lgb, gnw, *, batch, seq):
    t = batch * seq
    n_chunks = seq // CHUNK

    def zspec(base):
        return pl.BlockSpec((1, seq, LANES), lambda b, h: (base + h, b, 0))

    per_head = pl.BlockSpec((1, 8, LANES), lambda b, h: (h, 0, 0))
    return pl.pallas_call(
        _ret_kernel,
        grid=(batch, HEADS),
        in_specs=[zspec(0), zspec(HEADS), zspec(2 * HEADS), zspec(3 * HEADS),
                  per_head, per_head, per_head],
        out_specs=pl.BlockSpec((1, seq, LANES), lambda b, h: (h, b, 0)),
        out_shape=jax.ShapeDtypeStruct((HEADS, t, LANES), BF16),
        scratch_shapes=[
            pltpu.VMEM((seq, 2 * HEAD_DIM), BF16),
            pltpu.VMEM((n_chunks, HEAD_DIM, CHUNK), BF16),
            pltpu.VMEM((n_chunks, HEAD_DIM, HEAD_DIM), F32),
            pltpu.VMEM((n_chunks, HEAD_DIM, HEAD_DIM), BF16),
        ],
        compiler_params=pltpu.CompilerParams(
            dimension_semantics=("parallel", "parallel"),
            vmem_limit_bytes=VMEM_LIMIT),
        name="retention",
    )(z, z, z, z, lgf, lgb, gnw)


def _sgu_pointwise(zu_ref, zv_ref, nw_ref, u_ref, vn_ref):
    v = zv_ref[...].astype(F32)
    ms = jnp.sum(jnp.sum(v * v, axis=0), axis=-1, keepdims=True) * (1.0 / SGU_WIDTH)
    inv = lax.rsqrt(ms + EPS)
    for g in range(GROUPS):
        vn_ref[g] = (v[g] * inv * nw_ref[g, 0:1, :]).astype(BF16)
        u_ref[g] = zu_ref[g].astype(F32)


def _sgu_mix(u_ref, vn_ref, ws_ref, b_ref):
    n_chunks = vn_ref.shape[1] // CHUNK
    groups = []
    for g in range(GROUPS):
        rhs = jnp.concatenate(
            [vn_ref[g, c * CHUNK:(c + 1) * CHUNK, :] for c in range(n_chunks)], axis=1)
        sp = jnp.dot(ws_ref[g], rhs, preferred_element_type=F32)
        bias = b_ref[g]
        groups.append(jnp.concatenate(
            [u_ref[g, c * CHUNK:(c + 1) * CHUNK, :] * (sp[:, c * CHUNK:(c + 1) * CHUNK] + bias)
             for c in range(n_chunks)], axis=0).astype(BF16))
    return jnp.concatenate(groups, axis=1)


def _outproj_kernel(ret_ref, zu_ref, zv_ref, sgu_nw_ref, ws_ref, sb_ref, x_ref, w_ref, nw_ref,
                    x2_ref, h2_ref, u_ref, vn_ref):
    _sgu_pointwise(zu_ref, zv_ref, sgu_nw_ref, u_ref, vn_ref)
    ret = jnp.concatenate([ret_ref[s] for s in range(HEADS)], axis=1)
    x2_ref[...] = x_ref[...] + jnp.dot(ret, w_ref[0:RET_WIDTH, :], preferred_element_type=F32)
    sgu = _sgu_mix(u_ref, vn_ref, ws_ref, sb_ref)
    x2 = x2_ref[...] + jnp.dot(sgu, w_ref[RET_WIDTH:, :], preferred_element_type=F32)
    x2_ref[...] = x2
    h2_ref[...] = _rms(x2, nw_ref[...]).astype(BF16)


def _out_proj(ret, z, sgu_nw, ws_bf16, sgu_bias, x, w_bf16, norm_w, *, tm=512):
    t = x.shape[0]
    u_base = 4 * HEADS // GROUPS
    const3 = lambda i: (0, 0, 0)
    return pl.pallas_call(
        _outproj_kernel,
        grid=(t // tm,),
        in_specs=[
            pl.BlockSpec((HEADS, tm, LANES), lambda i: (0, i, 0)),
            pl.BlockSpec((GROUPS, tm, LANES), lambda i: (u_base, i, 0)),
            pl.BlockSpec((GROUPS, tm, LANES), lambda i: (u_base + 1, i, 0)),
            pl.BlockSpec((GROUPS, 8, LANES), const3),
            pl.BlockSpec((GROUPS, CHUNK, CHUNK), const3),
            pl.BlockSpec((GROUPS, CHUNK, LANES), const3),
            pl.BlockSpec((tm, D_MODEL), lambda i: (i, 0)),
            pl.BlockSpec((D_MODEL, D_MODEL), lambda i: (0, 0)),
            pl.BlockSpec((1, D_MODEL), lambda i: (0, 0)),
        ],
        out_specs=[pl.BlockSpec((tm, D_MODEL), lambda i: (i, 0)),
                   pl.BlockSpec((tm, D_MODEL), lambda i: (i, 0))],
        out_shape=[jax.ShapeDtypeStruct((t, D_MODEL), F32),
                   jax.ShapeDtypeStruct((t, D_MODEL), BF16)],
        scratch_shapes=[pltpu.VMEM((GROUPS, tm, LANES), F32),
                        pltpu.VMEM((GROUPS, tm, LANES), BF16)],
        compiler_params=pltpu.CompilerParams(
            dimension_semantics=("parallel",),
            vmem_limit_bytes=VMEM_LIMIT),
        name="out_proj",
    )(ret, z, z, sgu_nw, ws_bf16, sgu_bias, x, w_bf16, norm_w.reshape(1, D_MODEL))


def _ffn_up_kernel(hp_ref, h_ref, hn_ref, wa_ref, wb_ref, cwa_ref, cwb_ref, cba_ref,
                   cbb_ref, o_ref, lhs_ref, *, tiles_per_seq):
    i = pl.program_id(0)
    j = pl.program_id(1)
    tm = h_ref.shape[0]
    halo = hp_ref.shape[0]

    @pl.when(j == 0)
    def _():
        pos = i % tiles_per_seq
        hp = hp_ref[...]
        hn = hn_ref[...]
        lhs_ref[0:halo, :] = jnp.where(pos == 0, jnp.zeros_like(hp), hp)
        lhs_ref[halo:halo + tm, :] = h_ref[...]
        lhs_ref[halo + tm:, :] = jnp.where(pos == tiles_per_seq - 1, jnp.zeros_like(hn), hn)

    lhs = lhs_ref[...]
    n_rows = tm + 2 * halo

    def conv_branch(w_ref, cw_ref, cb_ref):
        u = jnp.dot(lhs, w_ref[...], preferred_element_type=F32)
        prev = pltpu.roll(u, 1, 0)[halo:halo + tm]
        nxt = pltpu.roll(u, n_rows - 1, 0)[halo:halo + tm]
        cur = u[halo:halo + tm]
        return prev * cw_ref[0:1, :] + cur * cw_ref[1:2, :] + nxt * cw_ref[2:3, :] + cb_ref[...]

    a = conv_branch(wa_ref, cwa_ref, cba_ref)
    b = conv_branch(wb_ref, cwb_ref, cbb_ref)
    o_ref[...] = (_silu(a) * b).astype(o_ref.dtype)


def _ffn_up(h2, w_up_bf16, conv_w, conv_b, *, seq, tm=FFN_ROWS, tf=FFN_TILE):
    t = h2.shape[0]
    halo = BF16_ROWS
    n_j = D_FF // tf
    halo_blocks = tm // halo
    last_halo_block = t // halo - 1
    kernel = functools.partial(_ffn_up_kernel, tiles_per_seq=seq // tm)
    return pl.pallas_call(
        kernel,
        grid=(t // tm, n_j),
        in_specs=[
            pl.BlockSpec((halo, D_MODEL), lambda i, j: (jnp.maximum(i * halo_blocks - 1, 0), 0)),
            pl.BlockSpec((tm, D_MODEL), lambda i, j: (i, 0)),
            pl.BlockSpec((halo, D_MODEL),
                         lambda i, j: (jnp.minimum((i + 1) * halo_blocks, last_halo_block), 0)),
            pl.BlockSpec((D_MODEL, tf), lambda i, j: (0, j)),
            pl.BlockSpec((D_MODEL, tf), lambda i, j: (0, n_j + j)),
            pl.BlockSpec((3, tf), lambda i, j: (0, j)),
            pl.BlockSpec((3, tf), lambda i, j: (0, n_j + j)),
            pl.BlockSpec((1, tf), lambda i, j: (0, j)),
            pl.BlockSpec((1, tf), lambda i, j: (0, n_j + j)),
        ],
        out_specs=pl.BlockSpec((None, tm, tf), lambda i, j: (j, i, 0)),
        out_shape=jax.ShapeDtypeStruct((n_j, t, tf), BF16),
        scratch_shapes=[pltpu.VMEM((tm + 2 * halo, D_MODEL), BF16)],
        compiler_params=pltpu.CompilerParams(
            dimension_semantics=("parallel", "arbitrary"),
            vmem_limit_bytes=VMEM_LIMIT),
        name="ffn_up",
    )(h2, h2, h2, w_up_bf16, w_up_bf16, conv_w, conv_w,
      conv_b.reshape(1, 2 * D_FF), conv_b.reshape(1, 2 * D_FF))


def _ffn_down_kernel(act_ref, w_ref, x2_ref, fw_ref, o_ref):
    act = jnp.concatenate([act_ref[j] for j in range(act_ref.shape[0])], axis=1)
    y = x2_ref[...] + jnp.dot(act, w_ref[...], preferred_element_type=F32)
    o_ref[...] = _rms(y, fw_ref[...])


def _ffn_down(act, w_down_bf16, x2, final_w, *, tm=512):
    n_j, t, tf = act.shape
    return pl.pallas_call(
        _ffn_down_kernel,
        grid=(t // tm,),
        in_specs=[
            pl.BlockSpec((n_j, tm, tf), lambda i: (0, i, 0)),
            pl.BlockSpec((D_FF, D_MODEL), lambda i: (0, 0), pipeline_mode=pl.Buffered(1)),
            pl.BlockSpec((tm, D_MODEL), lambda i: (i, 0)),
            pl.BlockSpec((1, D_MODEL), lambda i: (0, 0)),
        ],
        out_specs=pl.BlockSpec((tm, D_MODEL), lambda i: (i, 0)),
        out_shape=jax.ShapeDtypeStruct((t, D_MODEL), F32),
        compiler_params=pltpu.CompilerParams(
            dimension_semantics=("parallel",),
            vmem_limit_bytes=VMEM_LIMIT),
        name="ffn_down",
    )(act, w_down_bf16, x2, final_w.reshape(1, D_MODEL))


def _ffn(h2, x2, w_up_bf16, conv_w, conv_b, w_down_bf16, final_w, *, seq):
    act = _ffn_up(h2, w_up_bf16, conv_w, conv_b, seq=seq)
    return _ffn_down(act, w_down_bf16, x2, final_w)


def _rope_tables(seq):
    inv_freq = ROPE_BASE ** (-np.arange(0, HEAD_DIM, 2, dtype=np.float64) / HEAD_DIM)
    ang = np.arange(seq, dtype=np.float64)[:, None] * inv_freq[None, :]
    cos = np.cos(ang).astype(np.float32)
    sin = np.sin(ang).astype(np.float32)
    return (jnp.asarray(np.concatenate([cos, cos], axis=-1)),
            jnp.asarray(np.concatenate([-sin, sin], axis=-1)))


def _per_slab(vec, slabs):
    return jnp.broadcast_to(vec.reshape(slabs, 1, LANES), (slabs, 8, LANES))


def _trunk(x, p):
    batch, seq, _ = x.shape
    xt = x.reshape(batch * seq, D_MODEL)
    z = _in_proj(xt, p["norm1_w"], p["w_in"], p["cos"], p["sin"], seq=seq)
    ret = _retention(z, p["lgf"], p["lgb"], p["gnw"], batch=batch, seq=seq)
    x2, h2 = _out_proj(ret, z, p["sgu_nw"], p["w_s"], p["sgu_b"], xt, p["w_out"], p["norm2_w"])
    y = _ffn(h2, x2, p["w_up"], p["conv_w"], p["conv_b"], p["w_down"], p["final_w"], seq=seq)
    return y.reshape(batch, seq, D_MODEL)


def kernel(x_prompt, x_sample, norm1_w, w_in, ret_log_decay_fwd, ret_log_decay_bwd, ret_gn_w, sgu_norm_w, sgu_w_s, sgu_b, w_out, norm2_w, w_up, conv_w, conv_b, w_down, final_norm_w):
    max_seq = max(x_prompt.shape[1], x_sample.shape[1])
    cos, sin = _rope_tables(max_seq)
    p = {
        "norm1_w": norm1_w[0],
        "w_in": w_in[0].astype(BF16),
        "cos": cos,
        "sin": sin,
        "lgf": jnp.broadcast_to(ret_log_decay_fwd[0].astype(F32)[:, None, None], (HEADS, 8, LANES)),
        "lgb": jnp.broadcast_to(ret_log_decay_bwd[0].astype(F32)[:, None, None], (HEADS, 8, LANES)),
        "gnw": _per_slab(ret_gn_w[0], HEADS),
        "sgu_nw": _per_slab(sgu_norm_w[0], GROUPS),
        "w_s": sgu_w_s[0].astype(BF16),
        "sgu_b": jnp.broadcast_to(sgu_b[0][:, :, None], (GROUPS, CHUNK, LANES)),
        "w_out": w_out[0].astype(BF16),
        "norm2_w": norm2_w[0],
        "w_up": w_up[0].astype(BF16),
        "conv_w": conv_w[0],
        "conv_b": conv_b[0],
        "w_down": w_down[0].astype(BF16),
        "final_w": final_norm_w,
    }
    return (_trunk(x_prompt, p), _trunk(x_sample, p))
```

```python
import functools
import math

import jax
import jax.numpy as jnp
import numpy as np
from jax import lax
from jax.experimental import pallas as pl
from jax.experimental.pallas import tpu as pltpu

D_MODEL = 2048
CHUNK = 128
HEADS = 8
HEAD_DIM = 128
RET_WIDTH = HEADS * HEAD_DIM
GROUPS = 8
SGU_WIDTH = GROUPS * CHUNK
IN_COLS = 4 * RET_WIDTH + 2 * SGU_WIDTH
IN_SLABS = IN_COLS // 128
D_FF = 5632
ROPE_BASE = 10000.0
EPS = 1e-6
LANES = 128
BF16_ROWS = 16
IN_PROJ_TILE = 2048
FFN_TILE = 512
FFN_ROWS = 1024
VMEM_LIMIT = 56 * 1024 * 1024

F32 = jnp.float32
BF16 = jnp.bfloat16


def _rms(x, w):
    ms = jnp.mean(x * x, axis=-1, keepdims=True)
    return x * lax.rsqrt(ms + EPS) * w


def _gelu_tanh(x):
    c = math.sqrt(2.0 / math.pi)
    h = 0.5 * x
    return h + h * jnp.tanh(x * (c + (c * 0.044715) * (x * x)))


def _silu(x):
    h = 0.5 * x
    return h + h * jnp.tanh(h)


def _slab_epilogue(slab):
    if slab < 2 * HEADS:
        return "rotary"
    if slab < 3 * HEADS:
        return "none"
    if slab < 4 * HEADS:
        return "silu"
    return "gelu"


def _inproj_kernel(x_ref, nw_ref, w_ref, cos_ref, sin_ref, z_ref, h_ref):
    j = pl.program_id(1)
    n_slabs = z_ref.shape[0]

    @pl.when(j == 0)
    def _():
        h_ref[...] = _rms(x_ref[...], nw_ref[...]).astype(BF16)

    def column_tile(jj):
        res = jnp.dot(h_ref[...], w_ref[...], preferred_element_type=F32)
        for s in range(n_slabs):
            y = res[:, s * LANES:(s + 1) * LANES]
            kind = _slab_epilogue(jj * n_slabs + s)
            if kind == "rotary":
                y = y * cos_ref[...] + pltpu.roll(y, HEAD_DIM // 2, 1) * sin_ref[...]
            elif kind == "silu":
                y = _silu(y)
            elif kind == "gelu":
                y = _gelu_tanh(y)
            z_ref[s] = y.astype(z_ref.dtype)

    for jj in range(IN_SLABS // n_slabs):
        pl.when(j == jj)(functools.partial(column_tile, jj))


def _in_proj(x, norm_w, w_bf16, cos, sin, *, seq, tm=1024, tn=IN_PROJ_TILE):
    t = x.shape[0]
    tiles_per_seq = seq // tm
    return pl.pallas_call(
        _inproj_kernel,
        grid=(t // tm, IN_COLS // tn),
        in_specs=[
            pl.BlockSpec((tm, D_MODEL), lambda i, j: (i, 0)),
            pl.BlockSpec((1, D_MODEL), lambda i, j: (0, 0)),
            pl.BlockSpec((D_MODEL, tn), lambda i, j: (0, j)),
            pl.BlockSpec((tm, LANES), lambda i, j: (i % tiles_per_seq, 0)),
            pl.BlockSpec((tm, LANES), lambda i, j: (i % tiles_per_seq, 0)),
        ],
        out_specs=pl.BlockSpec((tn // LANES, tm, LANES), lambda i, j: (j, i, 0)),
        out_shape=jax.ShapeDtypeStruct((IN_SLABS, t, LANES), BF16),
        scratch_shapes=[pltpu.VMEM((tm, D_MODEL), BF16)],
        compiler_params=pltpu.CompilerParams(
            dimension_semantics=("parallel", "arbitrary"),
            vmem_limit_bytes=VMEM_LIMIT),
        name="in_proj",
    )(x, norm_w.reshape(1, D_MODEL), w_bf16, cos, sin)


def _ret_kernel(q_ref, k_ref, v_ref, g_ref, lgf_ref, lgb_ref, gnw_ref,
                o_ref, qx_ref, kt_ref, kvf_ref, rb_ref):
    seq = q_ref.shape[1]
    n_chunks = seq // CHUNK
    lgf = lgf_ref[0, 0:1, :]
    lgb = lgb_ref[0, 0:1, :]
    row = lax.broadcasted_iota(jnp.int32, (CHUNK, CHUNK), 0).astype(F32)
    col = lax.broadcasted_iota(jnp.int32, (CHUNK, CHUNK), 1).astype(F32)
    diff = row - col
    scale = HEAD_DIM ** -0.5
    dcomb = jnp.where(diff >= 0, jnp.exp(lgf * diff), jnp.exp(lgb * (-diff))) * scale
    xi_f = jnp.exp(lgf * (row + 1.0)) * scale
    xi_b = jnp.exp(lgb * (CHUNK - row)) * scale
    zeta_f = jnp.exp(lgf * (CHUNK - 1.0 - row))
    zeta_b = jnp.exp(lgb * row)
    decay_f = jnp.exp(lgf * CHUNK)
    decay_b = jnp.exp(lgb * CHUNK)

    def rows(c):
        return pl.ds(pl.multiple_of(c * CHUNK, CHUNK), CHUNK)

    def prep(i, state):
        c = n_chunks - 1 - i
        sl = rows(c)
        q = q_ref[0, sl, :].astype(F32)
        k = k_ref[0, sl, :].astype(F32)
        kt_ref[c] = k_ref[0, sl, :].T
        qx_ref[sl, :] = jnp.concatenate([q * xi_f, q * xi_b], axis=1).astype(BF16)
        kz = jnp.concatenate([k * zeta_f, k * zeta_b], axis=1).astype(BF16)
        kv = lax.dot_general(kz, v_ref[0, sl, :], (((0,), (0,)), ((), ())),
                             preferred_element_type=F32)
        kvf_ref[c] = kv[:HEAD_DIM]
        rb_ref[c] = state.astype(BF16)
        return state * decay_b + kv[HEAD_DIM:]

    lax.fori_loop(0, n_chunks, prep, jnp.zeros((HEAD_DIM, HEAD_DIM), F32), unroll=8)

    gnw = gnw_ref[0, 0:1, :]

    def fwd(c, state):
        sl = rows(c)
        s = jnp.dot(q_ref[0, sl, :], kt_ref[c], preferred_element_type=F32) * dcomb
        lhs = jnp.concatenate([s.astype(BF16), qx_ref[sl, :]], axis=1)
        rhs = jnp.concatenate([v_ref[0, sl, :], state.astype(BF16), rb_ref[c]], axis=0)
        o = jnp.dot(lhs, rhs, preferred_element_type=F32)
        mu = jnp.mean(o, axis=-1, keepdims=True)
        d = o - mu
        var = jnp.mean(d * d, axis=-1, keepdims=True)
        on = d * lax.rsqrt(var + EPS) * gnw
        o_ref[0, sl, :] = (g_ref[0, sl, :].astype(F32) * on).astype(o_ref.dtype)
        return state * decay_f + kvf_ref[c]

    lax.fori_loop(0, n_chunks, fwd, jnp.zeros((HEAD_DIM, HEAD_DIM), F32), unroll=8)


def _retention(z, lgf, lgb, gnw, *, batch, seq):
    t = batch * seq
    n_chunks = seq // CHUNK

    def zspec(base):
        return pl.BlockSpec((1, seq, LANES), lambda b, h: (base + h, b, 0))

    per_head = pl.BlockSpec((1, 8, LANES), lambda b, h: (h, 0, 0))
    return pl.pallas_call(
        _ret_kernel,
        grid=(batch, HEADS),
        in_specs=[zspec(0), zspec(HEADS), zspec(2 * HEADS), zspec(3 * HEADS),
                  per_head, per_head, per_head],
        out_specs=pl.BlockSpec((1, seq, LANES), lambda b, h: (h, b, 0)),
        out_shape=jax.ShapeDtypeStruct((HEADS, t, LANES), BF16),
        scratch_shapes=[
            pltpu.VMEM((seq, 2 * HEAD_DIM), BF16),
            pltpu.VMEM((n_chunks, HEAD_DIM, CHUNK), BF16),
            pltpu.VMEM((n_chunks, HEAD_DIM, HEAD_DIM), F32),
            pltpu.VMEM((n_chunks, HEAD_DIM, HEAD_DIM), BF16),
        ],
        compiler_params=pltpu.CompilerParams(
            dimension_semantics=("parallel", "parallel"),
            vmem_limit_bytes=VMEM_LIMIT),
        name="retention",
    )(z, z, z, z, lgf, lgb, gnw)


def _sgu_pointwise(zu_ref, zv_ref, nw_ref, u_ref, vn_ref):
    v = zv_ref[...].astype(F32)
    ms = jnp.sum(jnp.sum(v * v, axis=0), axis=-1, keepdims=True) * (1.0 / SGU_WIDTH)
    inv = lax.rsqrt(ms + EPS)
    for g in range(GROUPS):
        vn_ref[g] = (v[g] * inv * nw_ref[g, 0:1, :]).astype(BF16)
        u_ref[g] = zu_ref[g].astype(F32)


def _sgu_mix(u_ref, vn_ref, ws_ref, b_ref):
    n_chunks = vn_ref.shape[1] // CHUNK
    groups = []
    for g in range(GROUPS):
        rhs = jnp.concatenate(
            [vn_ref[g, c * CHUNK:(c + 1) * CHUNK, :] for c in range(n_chunks)], axis=1)
        sp = jnp.dot(ws_ref[g], rhs, preferred_element_type=F32)
        bias = b_ref[g]
        groups.append(jnp.concatenate(
            [u_ref[g, c * CHUNK:(c + 1) * CHUNK, :] * (sp[:, c * CHUNK:(c + 1) * CHUNK] + bias)
             for c in range(n_chunks)], axis=0).astype(BF16))
    return jnp.concatenate(groups, axis=1)


def _outproj_kernel(ret_ref, zu_ref, zv_ref, sgu_nw_ref, ws_ref, sb_ref, x_ref, w_ref, nw_ref,
                    x2_ref, h2_ref, u_ref, vn_ref):
    _sgu_pointwise(zu_ref, zv_ref, sgu_nw_ref, u_ref, vn_ref)
    ret = jnp.concatenate([ret_ref[s] for s in range(HEADS)], axis=1)
    x2_ref[...] = x_ref[...] + jnp.dot(ret, w_ref[0:RET_WIDTH, :], preferred_element_type=F32)
    sgu = _sgu_mix(u_ref, vn_ref, ws_ref, sb_ref)
    x2 = x2_ref[...] + jnp.dot(sgu, w_ref[RET_WIDTH:, :], preferred_element_type=F32)
    x2_ref[...] = x2
    h2_ref[...] = _rms(x2, nw_ref[...]).astype(BF16)


def _out_proj(ret, z, sgu_nw, ws_bf16, sgu_bias, x, w_bf16, norm_w, *, tm=512):
    t = x.shape[0]
    u_base = 4 * HEADS // GROUPS
    const3 = lambda i: (0, 0, 0)
    return pl.pallas_call(
        _outproj_kernel,
        grid=(t // tm,),
        in_specs=[
            pl.BlockSpec((HEADS, tm, LANES), lambda i: (0, i, 0)),
            pl.BlockSpec((GROUPS, tm, LANES), lambda i: (u_base, i, 0)),
            pl.BlockSpec((GROUPS, tm, LANES), lambda i: (u_base + 1, i, 0)),
            pl.BlockSpec((GROUPS, 8, LANES), const3),
            pl.BlockSpec((GROUPS, CHUNK, CHUNK), const3),
            pl.BlockSpec((GROUPS, CHUNK, LANES), const3),
            pl.BlockSpec((tm, D_MODEL), lambda i: (i, 0)),
            pl.BlockSpec((D_MODEL, D_MODEL), lambda i: (0, 0)),
            pl.BlockSpec((1, D_MODEL), lambda i: (0, 0)),
        ],
        out_specs=[pl.BlockSpec((tm, D_MODEL), lambda i: (i, 0)),
                   pl.BlockSpec((tm, D_MODEL), lambda i: (i, 0))],
        out_shape=[jax.ShapeDtypeStruct((t, D_MODEL), F32),
                   jax.ShapeDtypeStruct((t, D_MODEL), BF16)],
        scratch_shapes=[pltpu.VMEM((GROUPS, tm, LANES), F32),
                        pltpu.VMEM((GROUPS, tm, LANES), BF16)],
        compiler_params=pltpu.CompilerParams(
            dimension_semantics=("parallel",),
            vmem_limit_bytes=VMEM_LIMIT),
        name="out_proj",
    )(ret, z, z, sgu_nw, ws_bf16, sgu_bias, x, w_bf16, norm_w.reshape(1, D_MODEL))


def _ffn_up_kernel(hp_ref, h_ref, hn_ref, wa_ref, wb_ref, cwa_ref, cwb_ref, cba_ref,
                   cbb_ref, o_ref, lhs_ref, *, tiles_per_seq):
    i = pl.program_id(0)
    j = pl.program_id(1)
    tm = h_ref.shape[0]
    halo = hp_ref.shape[0]

    @pl.when(j == 0)
    def _():
        pos = i % tiles_per_seq
        hp = hp_ref[...]
        hn = hn_ref[...]
        lhs_ref[0:halo, :] = jnp.where(pos == 0, jnp.zeros_like(hp), hp)
        lhs_ref[halo:halo + tm, :] = h_ref[...]
        lhs_ref[halo + tm:, :] = jnp.where(pos == tiles_per_seq - 1, jnp.zeros_like(hn), hn)

    lhs = lhs_ref[...]
    n_rows = tm + 2 * halo

    def conv_branch(w_ref, cw_ref, cb_ref):
        u = jnp.dot(lhs, w_ref[...], preferred_element_type=F32)
        prev = pltpu.roll(u, 1, 0)[halo:halo + tm]
        nxt = pltpu.roll(u, n_rows - 1, 0)[halo:halo + tm]
        cur = u[halo:halo + tm]
        return prev * cw_ref[0:1, :] + cur * cw_ref[1:2, :] + nxt * cw_ref[2:3, :] + cb_ref[...]

    a = conv_branch(wa_ref, cwa_ref, cba_ref)
    b = conv_branch(wb_ref, cwb_ref, cbb_ref)
    o_ref[...] = (_silu(a) * b).astype(o_ref.dtype)


def _ffn_up(h2, w_up_bf16, conv_w, conv_b, *, seq, tm=FFN_ROWS, tf=FFN_TILE):
    t = h2.shape[0]
    halo = BF16_ROWS
    n_j = D_FF // tf
    halo_blocks = tm // halo
    last_halo_block = t // halo - 1
    kernel = functools.partial(_ffn_up_kernel, tiles_per_seq=seq // tm)
    return pl.pallas_call(
        kernel,
        grid=(t // tm, n_j),
        in_specs=[
            pl.BlockSpec((halo, D_MODEL), lambda i, j: (jnp.maximum(i * halo_blocks - 1, 0), 0)),
            pl.BlockSpec((tm, D_MODEL), lambda i, j: (i, 0)),
            pl.BlockSpec((halo, D_MODEL),
                         lambda i, j: (jnp.minimum((i + 1) * halo_blocks, last_halo_block), 0)),
            pl.BlockSpec((D_MODEL, tf), lambda i, j: (0, j)),
            pl.BlockSpec((D_MODEL, tf), lambda i, j: (0, n_j + j)),
            pl.BlockSpec((3, tf), lambda i, j: (0, j)),
            pl.BlockSpec((3, tf), lambda i, j: (0, n_j + j)),
            pl.BlockSpec((1, tf), lambda i, j: (0, j)),
            pl.BlockSpec((1, tf), lambda i, j: (0, n_j + j)),
        ],
        out_specs=pl.BlockSpec((None, tm, tf), lambda i, j: (j, i, 0)),
        out_shape=jax.ShapeDtypeStruct((n_j, t, tf), BF16),
        scratch_shapes=[pltpu.VMEM((tm + 2 * halo, D_MODEL), BF16)],
        compiler_params=pltpu.CompilerParams(
            dimension_semantics=("parallel", "arbitrary"),
            vmem_limit_bytes=VMEM_LIMIT),
        name="ffn_up",
    )(h2, h2, h2, w_up_bf16, w_up_bf16, conv_w, conv_w,
      conv_b.reshape(1, 2 * D_FF), conv_b.reshape(1, 2 * D_FF))


def _ffn_down_kernel(act_ref, w_ref, x2_ref, fw_ref, o_ref):
    act = jnp.concatenate([act_ref[j] for j in range(act_ref.shape[0])], axis=1)
    y = x2_ref[...] + jnp.dot(act, w_ref[...], preferred_element_type=F32)
    o_ref[...] = _rms(y, fw_ref[...])


def _ffn_down(act, w_down_bf16, x2, final_w, *, tm=512):
    n_j, t, tf = act.shape
    return pl.pallas_call(
        _ffn_down_kernel,
        grid=(t // tm,),
        in_specs=[
            pl.BlockSpec((n_j, tm, tf), lambda i: (0, i, 0)),
            pl.BlockSpec((D_FF, D_MODEL), lambda i: (0, 0), pipeline_mode=pl.Buffered(1)),
            pl.BlockSpec((tm, D_MODEL), lambda i: (i, 0)),
            pl.BlockSpec((1, D_MODEL), lambda i: (0, 0)),
        ],
        out_specs=pl.BlockSpec((tm, D_MODEL), lambda i: (i, 0)),
        out_shape=jax.ShapeDtypeStruct((t, D_MODEL), F32),
        compiler_params=pltpu.CompilerParams(
            dimension_semantics=("parallel",),
            vmem_limit_bytes=VMEM_LIMIT),
        name="ffn_down",
    )(act, w_down_bf16, x2, final_w.reshape(1, D_MODEL))


def _ffn(h2, x2, w_up_bf16, conv_w, conv_b, w_down_bf16, final_w, *, seq):
    act = _ffn_up(h2, w_up_bf16, conv_w, conv_b, seq=seq)
    return _ffn_down(act, w_down_bf16, x2, final_w)


def _rope_tables(seq):
    inv_freq = ROPE_BASE ** (-np.arange(0, HEAD_DIM, 2, dtype=np.float64) / HEAD_DIM)
    ang = np.arange(seq, dtype=np.float64)[:, None] * inv_freq[None, :]
    cos = np.cos(ang).astype(np.float32)
    sin = np.sin(ang).astype(np.float32)
    return (jnp.asarray(np.concatenate([cos, cos], axis=-1)),
            jnp.asarray(np.concatenate([-sin, sin], axis=-1)))


def _per_slab(vec, slabs):
    return jnp.broadcast_to(vec.reshape(slabs, 1, LANES), (slabs, 8, LANES))


def _trunk(x, p):
    batch, seq, _ = x.shape
    xt = x.reshape(batch * seq, D_MODEL)
    z = _in_proj(xt, p["norm1_w"], p["w_in"], p["cos"], p["sin"], seq=seq)
    ret = _retention(z, p["lgf"], p["lgb"], p["gnw"], batch=batch, seq=seq)
    x2, h2 = _out_proj(ret, z, p["sgu_nw"], p["w_s"], p["sgu_b"], xt, p["w_out"], p["norm2_w"])
    y = _ffn(h2, x2, p["w_up"], p["conv_w"], p["conv_b"], p["w_down"], p["final_w"], seq=seq)
    return y.reshape(batch, seq, D_MODEL)


def kernel(x_prompt, x_sample, norm1_w, w_in, ret_log_decay_fwd, ret_log_decay_bwd, ret_gn_w, sgu_norm_w, sgu_w_s, sgu_b, w_out, norm2_w, w_up, conv_w, conv_b, w_down, final_norm_w):
    max_seq = max(x_prompt.shape[1], x_sample.shape[1])
    cos, sin = _rope_tables(max_seq)
    p = {
        "norm1_w": norm1_w[0],
        "w_in": w_in[0].astype(BF16),
        "cos": cos,
        "sin": sin,
        "lgf": jnp.broadcast_to(ret_log_decay_fwd[0].astype(F32)[:, None, None], (HEADS, 8, LANES)),
        "lgb": jnp.broadcast_to(ret_log_decay_bwd[0].astype(F32)[:, None, None], (HEADS, 8, LANES)),
        "gnw": _per_slab(ret_gn_w[0], HEADS),
        "sgu_nw": _per_slab(sgu_norm_w[0], GROUPS),
        "w_s": sgu_w_s[0].astype(BF16),
        "sgu_b": jnp.broadcast_to(sgu_b[0][:, :, None], (GROUPS, CHUNK, LANES)),
        "w_out": w_out[0].astype(BF16),
        "norm2_w": norm2_w[0],
        "w_up": w_up[0].astype(BF16),
        "conv_w": conv_w[0],
        "conv_b": conv_b[0],
        "w_down": w_down[0].astype(BF16),
        "final_w": final_norm_w,
    }
    return (_trunk(x_prompt, p), _trunk(x_sample, p))
```

```python
import functools
import math

import jax
import jax.numpy as jnp
import numpy as np
from jax import lax
from jax.experimental import pallas as pl
from jax.experimental.pallas import tpu as pltpu

D_MODEL = 2048
CHUNK = 128
HEADS = 8
HEAD_DIM = 128
RET_WIDTH = HEADS * HEAD_DIM
GROUPS = 8
SGU_WIDTH = GROUPS * CHUNK
IN_COLS = 4 * RET_WIDTH + 2 * SGU_WIDTH
IN_SLABS = IN_COLS // 128
D_FF = 5632
ROPE_BASE = 10000.0
EPS = 1e-6
LANES = 128
BF16_ROWS = 16
IN_PROJ_TILE = 2048
FFN_TILE = 512
FFN_ROWS = 1024
VMEM_LIMIT = 56 * 1024 * 1024

F32 = jnp.float32
BF16 = jnp.bfloat16


def _rms(x, w):
    ms = jnp.mean(x * x, axis=-1, keepdims=True)
    return x * lax.rsqrt(ms + EPS) * w


def _gelu_tanh(x):
    c = math.sqrt(2.0 / math.pi)
    h = 0.5 * x
    return h + h * jnp.tanh(x * (c + (c * 0.044715) * (x * x)))


def _silu(x):
    h = 0.5 * x
    return h + h * jnp.tanh(h)


def _slab_epilogue(slab):
    if slab < 2 * HEADS:
        return "rotary"
    if slab < 3 * HEADS:
        return "none"
    if slab < 4 * HEADS:
        return "silu"
    return "gelu"


def _inproj_kernel(x_ref, nw_ref, w_ref, cos_ref, sin_ref, z_ref, h_ref):
    j = pl.program_id(1)
    n_slabs = z_ref.shape[0]

    @pl.when(j == 0)
    def _():
        h_ref[...] = _rms(x_ref[...], nw_ref[...]).astype(BF16)

    def column_tile(jj):
        res = jnp.dot(h_ref[...], w_ref[...], preferred_element_type=F32)
        for s in range(n_slabs):
            y = res[:, s * LANES:(s + 1) * LANES]
            kind = _slab_epilogue(jj * n_slabs + s)
            if kind == "rotary":
                y = y * cos_ref[...] + pltpu.roll(y, HEAD_DIM // 2, 1) * sin_ref[...]
            elif kind == "silu":
                y = _silu(y)
            elif kind == "gelu":
                y = _gelu_tanh(y)
            z_ref[s] = y.astype(z_ref.dtype)

    for jj in range(IN_SLABS // n_slabs):
        pl.when(j == jj)(functools.partial(column_tile, jj))


def _in_proj(x, norm_w, w_bf16, cos, sin, *, seq, tm=1024, tn=IN_PROJ_TILE):
    t = x.shape[0]
    tiles_per_seq = seq // tm
    return pl.pallas_call(
        _inproj_kernel,
        grid=(t // tm, IN_COLS // tn),
        in_specs=[
            pl.BlockSpec((tm, D_MODEL), lambda i, j: (i, 0)),
            pl.BlockSpec((1, D_MODEL), lambda i, j: (0, 0)),
            pl.BlockSpec((D_MODEL, tn), lambda i, j: (0, j)),
            pl.BlockSpec((tm, LANES), lambda i, j: (i % tiles_per_seq, 0)),
            pl.BlockSpec((tm, LANES), lambda i, j: (i % tiles_per_seq, 0)),
        ],
        out_specs=pl.BlockSpec((tn // LANES, tm, LANES), lambda i, j: (j, i, 0)),
        out_shape=jax.ShapeDtypeStruct((IN_SLABS, t, LANES), BF16),
        scratch_shapes=[pltpu.VMEM((tm, D_MODEL), BF16)],
        compiler_params=pltpu.CompilerParams(
            dimension_semantics=("parallel", "arbitrary"),
            vmem_limit_bytes=VMEM_LIMIT),
        name="in_proj",
    )(x, norm_w.reshape(1, D_MODEL), w_bf16, cos, sin)


def _ret_kernel(q_ref, k_ref, v_ref, g_ref, lgf_ref, lgb_ref, gnw_ref,
                o_ref, qx_ref, kt_ref, kvf_ref, rb_ref):
    seq = q_ref.shape[1]
    n_chunks = seq // CHUNK
    lgf = lgf_ref[0, 0:1, :]
    lgb = lgb_ref[0, 0:1, :]
    row = lax.broadcasted_iota(jnp.int32, (CHUNK, CHUNK), 0).astype(F32)
    col = lax.broadcasted_iota(jnp.int32, (CHUNK, CHUNK), 1).astype(F32)
    diff = row - col
    scale = HEAD_DIM ** -0.5
    dcomb = jnp.where(diff >= 0, jnp.exp(lgf * diff), jnp.exp(lgb * (-diff))) * scale
    xi_f = jnp.exp(lgf * (row + 1.0)) * scale
    xi_b = jnp.exp(lgb * (CHUNK - row)) * scale
    zeta_f = jnp.exp(lgf * (CHUNK - 1.0 - row))
    zeta_b = jnp.exp(lgb * row)
    decay_f = jnp.exp(lgf * CHUNK)
    decay_b = jnp.exp(lgb * CHUNK)

    def rows(c):
        return pl.ds(pl.multiple_of(c * CHUNK, CHUNK), CHUNK)

    def prep(i, state):
        c = n_chunks - 1 - i
        sl = rows(c)
        q = q_ref[0, sl, :].astype(F32)
        k = k_ref[0, sl, :].astype(F32)
        kt_ref[c] = k_ref[0, sl, :].T
        qx_ref[sl, :] = jnp.concatenate([q * xi_f, q * xi_b], axis=1).astype(BF16)
        kz = jnp.concatenate([k * zeta_f, k * zeta_b], axis=1).astype(BF16)
        kv = lax.dot_general(kz, v_ref[0, sl, :], (((0,), (0,)), ((), ())),
                             preferred_element_type=F32)
        kvf_ref[c] = kv[:HEAD_DIM]
        rb_ref[c] = state.astype(BF16)
        return state * decay_b + kv[HEAD_DIM:]

    lax.fori_loop(0, n_chunks, prep, jnp.zeros((HEAD_DIM, HEAD_DIM), F32), unroll=16)

    gnw = gnw_ref[0, 0:1, :]

    def fwd(c, state):
        sl = rows(c)
        s = jnp.dot(q_ref[0, sl, :], kt_ref[c], preferred_element_type=F32) * dcomb
        lhs = jnp.concatenate([s.astype(BF16), qx_ref[sl, :]], axis=1)
        rhs = jnp.concatenate([v_ref[0, sl, :], state.astype(BF16), rb_ref[c]], axis=0)
        o = jnp.dot(lhs, rhs, preferred_element_type=F32)
        mu = jnp.mean(o, axis=-1, keepdims=True)
        d = o - mu
        var = jnp.mean(d * d, axis=-1, keepdims=True)
        on = d * lax.rsqrt(var + EPS) * gnw
        o_ref[0, sl, :] = (g_ref[0, sl, :].astype(F32) * on).astype(o_ref.dtype)
        return state * decay_f + kvf_ref[c]

    lax.fori_loop(0, n_chunks, fwd, jnp.zeros((HEAD_DIM, HEAD_DIM), F32), unroll=16)


def _retention(z, lgf, lgb, gnw, *, batch, seq):
    t = batch * seq
    n_chunks = seq // CHUNK

    def zspec(base):
        return pl.BlockSpec((1, seq, LANES), lambda b, h: (base + h, b, 0))

    per_head = pl.BlockSpec((1, 8, LANES), lambda b, h: (h, 0, 0))
    return pl.pallas_call(
        _ret_kernel,
        grid=(batch, HEADS),
        in_specs=[zspec(0), zspec(HEADS), zspec(2 * HEADS), zspec(3 * HEADS),
                  per_head, per_head, per_head],
        out_specs=pl.BlockSpec((1, seq, LANES), lambda b, h: (h, b, 0)),
        out_shape=jax.ShapeDtypeStruct((HEADS, t, LANES), BF16),
        scratch_shapes=[
            pltpu.VMEM((seq, 2 * HEAD_DIM), BF16),
            pltpu.VMEM((n_chunks, HEAD_DIM, CHUNK), BF16),
            pltpu.VMEM((n_chunks, HEAD_DIM, HEAD_DIM), F32),
            pltpu.VMEM((n_chunks, HEAD_DIM, HEAD_DIM), BF16),
        ],
        compiler_params=pltpu.CompilerParams(
            dimension_semantics=("parallel", "parallel"),
            vmem_limit_bytes=VMEM_LIMIT),
        name="retention",
    )(z, z, z, z, lgf, lgb, gnw)


def _sgu_pointwise(zu_ref, zv_ref, nw_ref, u_ref, vn_ref):
    v = zv_ref[...].astype(F32)
    ms = jnp.sum(jnp.sum(v * v, axis=0), axis=-1, keepdims=True) * (1.0 / SGU_WIDTH)
    inv = lax.rsqrt(ms + EPS)
    for g in range(GROUPS):
        vn_ref[g] = (v[g] * inv * nw_ref[g, 0:1, :]).astype(BF16)
        u_ref[g] = zu_ref[g].astype(F32)


def _sgu_mix(u_ref, vn_ref, ws_ref, b_ref):
    n_chunks = vn_ref.shape[1] // CHUNK
    groups = []
    for g in range(GROUPS):
        rhs = jnp.concatenate(
            [vn_ref[g, c * CHUNK:(c + 1) * CHUNK, :] for c in range(n_chunks)], axis=1)
        sp = jnp.dot(ws_ref[g], rhs, preferred_element_type=F32)
        bias = b_ref[g]
        groups.append(jnp.concatenate(
            [u_ref[g, c * CHUNK:(c + 1) * CHUNK, :] * (sp[:, c * CHUNK:(c + 1) * CHUNK] + bias)
             for c in range(n_chunks)], axis=0).astype(BF16))
    return jnp.concatenate(groups, axis=1)


def _outproj_kernel(ret_ref, zu_ref, zv_ref, sgu_nw_ref, ws_ref, sb_ref, x_ref, w_ref, nw_ref,
                    x2_ref, h2_ref, u_ref, vn_ref):
    _sgu_pointwise(zu_ref, zv_ref, sgu_nw_ref, u_ref, vn_ref)
    ret = jnp.concatenate([ret_ref[s] for s in range(HEADS)], axis=1)
    x2_ref[...] = x_ref[...] + jnp.dot(ret, w_ref[0:RET_WIDTH, :], preferred_element_type=F32)
    sgu = _sgu_mix(u_ref, vn_ref, ws_ref, sb_ref)
    x2 = x2_ref[...] + jnp.dot(sgu, w_ref[RET_WIDTH:, :], preferred_element_type=F32)
    x2_ref[...] = x2
    h2_ref[...] = _rms(x2, nw_ref[...]).astype(BF16)


def _out_proj(ret, z, sgu_nw, ws_bf16, sgu_bias, x, w_bf16, norm_w, *, tm=512):
    t = x.shape[0]
    u_base = 4 * HEADS // GROUPS
    const3 = lambda i: (0, 0, 0)
    return pl.pallas_call(
        _outproj_kernel,
        grid=(t // tm,),
        in_specs=[
            pl.BlockSpec((HEADS, tm, LANES), lambda i: (0, i, 0)),
            pl.BlockSpec((GROUPS, tm, LANES), lambda i: (u_base, i, 0)),
            pl.BlockSpec((GROUPS, tm, LANES), lambda i: (u_base + 1, i, 0)),
            pl.BlockSpec((GROUPS, 8, LANES), const3),
            pl.BlockSpec((GROUPS, CHUNK, CHUNK), const3),
            pl.BlockSpec((GROUPS, CHUNK, LANES), const3),
            pl.BlockSpec((tm, D_MODEL), lambda i: (i, 0)),
            pl.BlockSpec((D_MODEL, D_MODEL), lambda i: (0, 0)),
            pl.BlockSpec((1, D_MODEL), lambda i: (0, 0)),
        ],
        out_specs=[pl.BlockSpec((tm, D_MODEL), lambda i: (i, 0)),
                   pl.BlockSpec((tm, D_MODEL), lambda i: (i, 0))],
        out_shape=[jax.ShapeDtypeStruct((t, D_MODEL), F32),
                   jax.ShapeDtypeStruct((t, D_MODEL), BF16)],
        scratch_shapes=[pltpu.VMEM((GROUPS, tm, LANES), F32),
                        pltpu.VMEM((GROUPS, tm, LANES), BF16)],
        compiler_params=pltpu.CompilerParams(
            dimension_semantics=("parallel",),
            vmem_limit_bytes=VMEM_LIMIT),
        name="out_proj",
    )(ret, z, z, sgu_nw, ws_bf16, sgu_bias, x, w_bf16, norm_w.reshape(1, D_MODEL))


def _ffn_up_kernel(hp_ref, h_ref, hn_ref, wa_ref, wb_ref, cwa_ref, cwb_ref, cba_ref,
                   cbb_ref, o_ref, lhs_ref, *, tiles_per_seq):
    i = pl.program_id(0)
    j = pl.program_id(1)
    tm = h_ref.shape[0]
    halo = hp_ref.shape[0]

    @pl.when(j == 0)
    def _():
        pos = i % tiles_per_seq
        hp = hp_ref[...]
        hn = hn_ref[...]
        lhs_ref[0:halo, :] = jnp.where(pos == 0, jnp.zeros_like(hp), hp)
        lhs_ref[halo:halo + tm, :] = h_ref[...]
        lhs_ref[halo + tm:, :] = jnp.where(pos == tiles_per_seq - 1, jnp.zeros_like(hn), hn)

    lhs = lhs_ref[...]
    n_rows = tm + 2 * halo

    def conv_branch(w_ref, cw_ref, cb_ref):
        u = jnp.dot(lhs, w_ref[...], preferred_element_type=F32)
        prev = pltpu.roll(u, 1, 0)[halo:halo + tm]
        nxt = pltpu.roll(u, n_rows - 1, 0)[halo:halo + tm]
        cur = u[halo:halo + tm]
        return prev * cw_ref[0:1, :] + cur * cw_ref[1:2, :] + nxt * cw_ref[2:3, :] + cb_ref[...]

    a = conv_branch(wa_ref, cwa_ref, cba_ref)
    b = conv_branch(wb_ref, cwb_ref, cbb_ref)
    o_ref[...] = (_silu(a) * b).astype(o_ref.dtype)


def _ffn_up(h2, w_up_bf16, conv_w, conv_b, *, seq, tm=FFN_ROWS, tf=FFN_TILE):
    t = h2.shape[0]
    halo = BF16_ROWS
    n_j = D_FF // tf
    halo_blocks = tm // halo
    last_halo_block = t // halo - 1
    kernel = functools.partial(_ffn_up_kernel, tiles_per_seq=seq // tm)
    return pl.pallas_call(
        kernel,
        grid=(t // tm, n_j),
        in_specs=[
            pl.BlockSpec((halo, D_MODEL), lambda i, j: (jnp.maximum(i * halo_blocks - 1, 0), 0)),
            pl.BlockSpec((tm, D_MODEL), lambda i, j: (i, 0)),
            pl.BlockSpec((halo, D_MODEL),
                         lambda i, j: (jnp.minimum((i + 1) * halo_blocks, last_halo_block), 0)),
            pl.BlockSpec((D_MODEL, tf), lambda i, j: (0, j)),
            pl.BlockSpec((D_MODEL, tf), lambda i, j: (0, n_j + j)),
            pl.BlockSpec((3, tf), lambda i, j: (0, j)),
            pl.BlockSpec((3, tf), lambda i, j: (0, n_j + j)),
            pl.BlockSpec((1, tf), lambda i, j: (0, j)),
            pl.BlockSpec((1, tf), lambda i, j: (0, n_j + j)),
        ],
        out_specs=pl.BlockSpec((None, tm, tf), lambda i, j: (j, i, 0)),
        out_shape=jax.ShapeDtypeStruct((n_j, t, tf), BF16),
        scratch_shapes=[pltpu.VMEM((tm + 2 * halo, D_MODEL), BF16)],
        compiler_params=pltpu.CompilerParams(
            dimension_semantics=("parallel", "arbitrary"),
            vmem_limit_bytes=VMEM_LIMIT),
        name="ffn_up",
    )(h2, h2, h2, w_up_bf16, w_up_bf16, conv_w, conv_w,
      conv_b.reshape(1, 2 * D_FF), conv_b.reshape(1, 2 * D_FF))


def _ffn_down_kernel(act_ref, w_ref, x2_ref, fw_ref, o_ref):
    act = jnp.concatenate([act_ref[j] for j in range(act_ref.shape[0])], axis=1)
    y = x2_ref[...] + jnp.dot(act, w_ref[...], preferred_element_type=F32)
    o_ref[...] = _rms(y, fw_ref[...])


def _ffn_down(act, w_down_bf16, x2, final_w, *, tm=512):
    n_j, t, tf = act.shape
    return pl.pallas_call(
        _ffn_down_kernel,
        grid=(t // tm,),
        in_specs=[
            pl.BlockSpec((n_j, tm, tf), lambda i: (0, i, 0)),
            pl.BlockSpec((D_FF, D_MODEL), lambda i: (0, 0), pipeline_mode=pl.Buffered(1)),
            pl.BlockSpec((tm, D_MODEL), lambda i: (i, 0)),
            pl.BlockSpec((1, D_MODEL), lambda i: (0, 0)),
        ],
        out_specs=pl.BlockSpec((tm, D_MODEL), lambda i: (i, 0)),
        out_shape=jax.ShapeDtypeStruct((t, D_MODEL), F32),
        compiler_params=pltpu.CompilerParams(
            dimension_semantics=("parallel",),
            vmem_limit_bytes=VMEM_LIMIT),
        name="ffn_down",
    )(act, w_down_bf16, x2, final_w.reshape(1, D_MODEL))


def _ffn(h2, x2, w_up_bf16, conv_w, conv_b, w_down_bf16, final_w, *, seq):
    act = _ffn_up(h2, w_up_bf16, conv_w, conv_b, seq=seq)
    return _ffn_down(act, w_down_bf16, x2, final_w)


def _rope_tables(seq):
    inv_freq = ROPE_BASE ** (-np.arange(0, HEAD_DIM, 2, dtype=np.float64) / HEAD_DIM)
    ang = np.arange(seq, dtype=np.float64)[:, None] * inv_freq[None, :]
    cos = np.cos(ang).astype(np.float32)
    sin = np.sin(ang).astype(np.float32)
    return (jnp.asarray(np.concatenate([cos, cos], axis=-1)),
            jnp.asarray(np.concatenate([-sin, sin], axis=-1)))


def _per_slab(vec, slabs):
    return jnp.broadcast_to(vec.reshape(slabs, 1, LANES), (slabs, 8, LANES))


def _trunk(x, p):
    batch, seq, _ = x.shape
    xt = x.reshape(batch * seq, D_MODEL)
    z = _in_proj(xt, p["norm1_w"], p["w_in"], p["cos"], p["sin"], seq=seq)
    ret = _retention(z, p["lgf"], p["lgb"], p["gnw"], batch=batch, seq=seq)
    x2, h2 = _out_proj(ret, z, p["sgu_nw"], p["w_s"], p["sgu_b"], xt, p["w_out"], p["norm2_w"])
    y = _ffn(h2, x2, p["w_up"], p["conv_w"], p["conv_b"], p["w_down"], p["final_w"], seq=seq)
    return y.reshape(batch, seq, D_MODEL)


def kernel(x_prompt, x_sample, norm1_w, w_in, ret_log_decay_fwd, ret_log_decay_bwd, ret_gn_w, sgu_norm_w, sgu_w_s, sgu_b, w_out, norm2_w, w_up, conv_w, conv_b, w_down, final_norm_w):
    max_seq = max(x_prompt.shape[1], x_sample.shape[1])
    cos, sin = _rope_tables(max_seq)
    p = {
        "norm1_w": norm1_w[0],
        "w_in": w_in[0].astype(BF16),
        "cos": cos,
        "sin": sin,
        "lgf": jnp.broadcast_to(ret_log_decay_fwd[0].astype(F32)[:, None, None], (HEADS, 8, LANES)),
        "lgb": jnp.broadcast_to(ret_log_decay_bwd[0].astype(F32)[:, None, None], (HEADS, 8, LANES)),
        "gnw": _per_slab(ret_gn_w[0], HEADS),
        "sgu_nw": _per_slab(sgu_norm_w[0], GROUPS),
        "w_s": sgu_w_s[0].astype(BF16),
        "sgu_b": jnp.broadcast_to(sgu_b[0][:, :, None], (GROUPS, CHUNK, LANES)),
        "w_out": w_out[0].astype(BF16),
        "norm2_w": norm2_w[0],
        "w_up": w_up[0].astype(BF16),
        "conv_w": conv_w[0],
        "conv_b": conv_b[0],
        "w_down": w_down[0].astype(BF16),
        "final_w": final_norm_w,
    }
    return (_trunk(x_prompt, p), _trunk(x_sample, p))
```

```python
import functools
import math

import jax
import jax.numpy as jnp
import numpy as np
from jax import lax
from jax.experimental import pallas as pl
from jax.experimental.pallas import tpu as pltpu

D_MODEL = 2048
CHUNK = 128
HEADS = 8
HEAD_DIM = 128
RET_WIDTH = HEADS * HEAD_DIM
GROUPS = 8
SGU_WIDTH = GROUPS * CHUNK
IN_COLS = 4 * RET_WIDTH + 2 * SGU_WIDTH
IN_SLABS = IN_COLS // 128
D_FF = 5632
ROPE_BASE = 10000.0
EPS = 1e-6
LANES = 128
BF16_ROWS = 16
IN_PROJ_TILE = 2048
FFN_TILE = 512
FFN_ROWS = 1024
VMEM_LIMIT = 56 * 1024 * 1024

F32 = jnp.float32
BF16 = jnp.bfloat16


def _rms(x, w):
    ms = jnp.mean(x * x, axis=-1, keepdims=True)
    return x * lax.rsqrt(ms + EPS) * w


def _gelu_tanh(x):
    c = math.sqrt(2.0 / math.pi)
    h = 0.5 * x
    return h + h * jnp.tanh(x * (c + (c * 0.044715) * (x * x)))


def _silu(x):
    h = 0.5 * x
    return h + h * jnp.tanh(h)


def _slab_epilogue(slab):
    if slab < 2 * HEADS:
        return "rotary"
    if slab < 3 * HEADS:
        return "none"
    if slab < 4 * HEADS:
        return "silu"
    return "gelu"


def _inproj_kernel(x_ref, nw_ref, w_ref, cos_ref, sin_ref, z_ref, h_ref):
    j = pl.program_id(1)
    n_slabs = z_ref.shape[0]

    @pl.when(j == 0)
    def _():
        h_ref[...] = _rms(x_ref[...], nw_ref[...]).astype(BF16)

    def column_tile(jj):
        res = jnp.dot(h_ref[...], w_ref[...], preferred_element_type=F32)
        for s in range(n_slabs):
            y = res[:, s * LANES:(s + 1) * LANES]
            kind = _slab_epilogue(jj * n_slabs + s)
            if kind == "rotary":
                y = y * cos_ref[...] + pltpu.roll(y, HEAD_DIM // 2, 1) * sin_ref[...]
            elif kind == "silu":
                y = _silu(y)
            elif kind == "gelu":
                y = _gelu_tanh(y)
            z_ref[s] = y.astype(z_ref.dtype)

    for jj in range(IN_SLABS // n_slabs):
        pl.when(j == jj)(functools.partial(column_tile, jj))


def _in_proj(x, norm_w, w_bf16, cos, sin, *, seq, tm=1024, tn=IN_PROJ_TILE):
    t = x.shape[0]
    tiles_per_seq = seq // tm
    return pl.pallas_call(
        _inproj_kernel,
        grid=(t // tm, IN_COLS // tn),
        in_specs=[
            pl.BlockSpec((tm, D_MODEL), lambda i, j: (i, 0)),
            pl.BlockSpec((1, D_MODEL), lambda i, j: (0, 0)),
            pl.BlockSpec((D_MODEL, tn), lambda i, j: (0, j)),
            pl.BlockSpec((tm, LANES), lambda i, j: (i % tiles_per_seq, 0)),
            pl.BlockSpec((tm, LANES), lambda i, j: (i % tiles_per_seq, 0)),
        ],
        out_specs=pl.BlockSpec((tn // LANES, tm, LANES), lambda i, j: (j, i, 0)),
        out_shape=jax.ShapeDtypeStruct((IN_SLABS, t, LANES), BF16),
        scratch_shapes=[pltpu.VMEM((tm, D_MODEL), BF16)],
        compiler_params=pltpu.CompilerParams(
            dimension_semantics=("parallel", "arbitrary"),
            vmem_limit_bytes=VMEM_LIMIT),
        name="in_proj",
    )(x, norm_w.reshape(1, D_MODEL), w_bf16, cos, sin)


def _ret_kernel(q_ref, k_ref, v_ref, g_ref, lgf_ref, lgb_ref, gnw_ref,
                o_ref, qx_ref, kt_ref, kvf_ref, rb_ref):
    seq = q_ref.shape[1]
    n_chunks = seq // CHUNK
    lgf = lgf_ref[0, 0:1, :]
    lgb = lgb_ref[0, 0:1, :]
    row = lax.broadcasted_iota(jnp.int32, (CHUNK, CHUNK), 0).astype(F32)
    col = lax.broadcasted_iota(jnp.int32, (CHUNK, CHUNK), 1).astype(F32)
    diff = row - col
    scale = HEAD_DIM ** -0.5
    dcomb = jnp.where(diff >= 0, jnp.exp(lgf * diff), jnp.exp(lgb * (-diff))) * scale
    xi_f = jnp.exp(lgf * (row + 1.0)) * scale
    xi_b = jnp.exp(lgb * (CHUNK - row)) * scale
    zeta_f = jnp.exp(lgf * (CHUNK - 1.0 - row))
    zeta_b = jnp.exp(lgb * row)
    decay_f = jnp.exp(lgf * CHUNK)
    decay_b = jnp.exp(lgb * CHUNK)

    def rows(c):
        return pl.ds(pl.multiple_of(c * CHUNK, CHUNK), CHUNK)

    def prep(i, state):
        c = n_chunks - 1 - i
        sl = rows(c)
        q = q_ref[0, sl, :].astype(F32)
        k = k_ref[0, sl, :].astype(F32)
        kt_ref[c] = k_ref[0, sl, :].T
        qx_ref[sl, :] = jnp.concatenate([q * xi_f, q * xi_b], axis=1).astype(BF16)
        kz = jnp.concatenate([k * zeta_f, k * zeta_b], axis=1).astype(BF16)
        kv = lax.dot_general(kz, v_ref[0, sl, :], (((0,), (0,)), ((), ())),
                             preferred_element_type=F32)
        kvf_ref[c] = kv[:HEAD_DIM]
        rb_ref[c] = state.astype(BF16)
        return state * decay_b + kv[HEAD_DIM:]

    lax.fori_loop(0, n_chunks, prep, jnp.zeros((HEAD_DIM, HEAD_DIM), F32), unroll=32)

    gnw = gnw_ref[0, 0:1, :]

    def fwd(c, state):
        sl = rows(c)
        s = jnp.dot(q_ref[0, sl, :], kt_ref[c], preferred_element_type=F32) * dcomb
        lhs = jnp.concatenate([s.astype(BF16), qx_ref[sl, :]], axis=1)
        rhs = jnp.concatenate([v_ref[0, sl, :], state.astype(BF16), rb_ref[c]], axis=0)
        o = jnp.dot(lhs, rhs, preferred_element_type=F32)
        mu = jnp.mean(o, axis=-1, keepdims=True)
        d = o - mu
        var = jnp.mean(d * d, axis=-1, keepdims=True)
        on = d * lax.rsqrt(var + EPS) * gnw
        o_ref[0, sl, :] = (g_ref[0, sl, :].astype(F32) * on).astype(o_ref.dtype)
        return state * decay_f + kvf_ref[c]

    lax.fori_loop(0, n_chunks, fwd, jnp.zeros((HEAD_DIM, HEAD_DIM), F32), unroll=32)


def _retention(z, lgf, lgb, gnw, *, batch, seq):
    t = batch * seq
    n_chunks = seq // CHUNK

    def zspec(base):
        return pl.BlockSpec((1, seq, LANES), lambda b, h: (base + h, b, 0))

    per_head = pl.BlockSpec((1, 8, LANES), lambda b, h: (h, 0, 0))
    return pl.pallas_call(
        _ret_kernel,
        grid=(batch, HEADS),
        in_specs=[zspec(0), zspec(HEADS), zspec(2 * HEADS), zspec(3 * HEADS),
                  per_head, per_head, per_head],
        out_specs=pl.BlockSpec((1, seq, LANES), lambda b, h: (h, b, 0)),
        out_shape=jax.ShapeDtypeStruct((HEADS, t, LANES), BF16),
        scratch_shapes=[
            pltpu.VMEM((seq, 2 * HEAD_DIM), BF16),
            pltpu.VMEM((n_chunks, HEAD_DIM, CHUNK), BF16),
            pltpu.VMEM((n_chunks, HEAD_DIM, HEAD_DIM), F32),
            pltpu.VMEM((n_chunks, HEAD_DIM, HEAD_DIM), BF16),
        ],
        compiler_params=pltpu.CompilerParams(
            dimension_semantics=("parallel", "parallel"),
            vmem_limit_bytes=VMEM_LIMIT),
        name="retention",
    )(z, z, z, z, lgf, lgb, gnw)


def _sgu_pointwise(zu_ref, zv_ref, nw_ref, u_ref, vn_ref):
    v = zv_ref[...].astype(F32)
    ms = jnp.sum(jnp.sum(v * v, axis=0), axis=-1, keepdims=True) * (1.0 / SGU_WIDTH)
    inv = lax.rsqrt(ms + EPS)
    for g in range(GROUPS):
        vn_ref[g] = (v[g] * inv * nw_ref[g, 0:1, :]).astype(BF16)
        u_ref[g] = zu_ref[g].astype(F32)


def _sgu_mix(u_ref, vn_ref, ws_ref, b_ref):
    n_chunks = vn_ref.shape[1] // CHUNK
    groups = []
    for g in range(GROUPS):
        rhs = jnp.concatenate(
            [vn_ref[g, c * CHUNK:(c + 1) * CHUNK, :] for c in range(n_chunks)], axis=1)
        sp = jnp.dot(ws_ref[g], rhs, preferred_element_type=F32)
        bias = b_ref[g]
        groups.append(jnp.concatenate(
            [u_ref[g, c * CHUNK:(c + 1) * CHUNK, :] * (sp[:, c * CHUNK:(c + 1) * CHUNK] + bias)
             for c in range(n_chunks)], axis=0).astype(BF16))
    return jnp.concatenate(groups, axis=1)


def _outproj_kernel(ret_ref, zu_ref, zv_ref, sgu_nw_ref, ws_ref, sb_ref, x_ref, w_ref, nw_ref,
                    x2_ref, h2_ref, u_ref, vn_ref):
    _sgu_pointwise(zu_ref, zv_ref, sgu_nw_ref, u_ref, vn_ref)
    ret = jnp.concatenate([ret_ref[s] for s in range(HEADS)], axis=1)
    x2_ref[...] = x_ref[...] + jnp.dot(ret, w_ref[0:RET_WIDTH, :], preferred_element_type=F32)
    sgu = _sgu_mix(u_ref, vn_ref, ws_ref, sb_ref)
    x2 = x2_ref[...] + jnp.dot(sgu, w_ref[RET_WIDTH:, :], preferred_element_type=F32)
    x2_ref[...] = x2
    h2_ref[...] = _rms(x2, nw_ref[...]).astype(BF16)


def _out_proj(ret, z, sgu_nw, ws_bf16, sgu_bias, x, w_bf16, norm_w, *, tm=512):
    t = x.shape[0]
    u_base = 4 * HEADS // GROUPS
    const3 = lambda i: (0, 0, 0)
    return pl.pallas_call(
        _outproj_kernel,
        grid=(t // tm,),
        in_specs=[
            pl.BlockSpec((HEADS, tm, LANES), lambda i: (0, i, 0)),
            pl.BlockSpec((GROUPS, tm, LANES), lambda i: (u_base, i, 0)),
            pl.BlockSpec((GROUPS, tm, LANES), lambda i: (u_base + 1, i, 0)),
            pl.BlockSpec((GROUPS, 8, LANES), const3),
            pl.BlockSpec((GROUPS, CHUNK, CHUNK), const3),
            pl.BlockSpec((GROUPS, CHUNK, LANES), const3),
            pl.BlockSpec((tm, D_MODEL), lambda i: (i, 0)),
            pl.BlockSpec((D_MODEL, D_MODEL), lambda i: (0, 0)),
            pl.BlockSpec((1, D_MODEL), lambda i: (0, 0)),
        ],
        out_specs=[pl.BlockSpec((tm, D_MODEL), lambda i: (i, 0)),
                   pl.BlockSpec((tm, D_MODEL), lambda i: (i, 0))],
        out_shape=[jax.ShapeDtypeStruct((t, D_MODEL), F32),
                   jax.ShapeDtypeStruct((t, D_MODEL), BF16)],
        scratch_shapes=[pltpu.VMEM((GROUPS, tm, LANES), F32),
                        pltpu.VMEM((GROUPS, tm, LANES), BF16)],
        compiler_params=pltpu.CompilerParams(
            dimension_semantics=("parallel",),
            vmem_limit_bytes=VMEM_LIMIT),
        name="out_proj",
    )(ret, z, z, sgu_nw, ws_bf16, sgu_bias, x, w_bf16, norm_w.reshape(1, D_MODEL))


def _ffn_up_kernel(hp_ref, h_ref, hn_ref, wa_ref, wb_ref, cwa_ref, cwb_ref, cba_ref,
                   cbb_ref, o_ref, lhs_ref, *, tiles_per_seq):
    i = pl.program_id(0)
    j = pl.program_id(1)
    tm = h_ref.shape[0]
    halo = hp_ref.shape[0]

    @pl.when(j == 0)
    def _():
        pos = i % tiles_per_seq
        hp = hp_ref[...]
        hn = hn_ref[...]
        lhs_ref[0:halo, :] = jnp.where(pos == 0, jnp.zeros_like(hp), hp)
        lhs_ref[halo:halo + tm, :] = h_ref[...]
        lhs_ref[halo + tm:, :] = jnp.where(pos == tiles_per_seq - 1, jnp.zeros_like(hn), hn)

    lhs = lhs_ref[...]
    n_rows = tm + 2 * halo

    def conv_branch(w_ref, cw_ref, cb_ref):
        u = jnp.dot(lhs, w_ref[...], preferred_element_type=F32)
        prev = pltpu.roll(u, 1, 0)[halo:halo + tm]
        nxt = pltpu.roll(u, n_rows - 1, 0)[halo:halo + tm]
        cur = u[halo:halo + tm]
        return prev * cw_ref[0:1, :] + cur * cw_ref[1:2, :] + nxt * cw_ref[2:3, :] + cb_ref[...]

    a = conv_branch(wa_ref, cwa_ref, cba_ref)
    b = conv_branch(wb_ref, cwb_ref, cbb_ref)
    o_ref[...] = (_silu(a) * b).astype(o_ref.dtype)


def _ffn_up(h2, w_up_bf16, conv_w, conv_b, *, seq, tm=FFN_ROWS, tf=FFN_TILE):
    t = h2.shape[0]
    halo = BF16_ROWS
    n_j = D_FF // tf
    halo_blocks = tm // halo
    last_halo_block = t // halo - 1
    kernel = functools.partial(_ffn_up_kernel, tiles_per_seq=seq // tm)
    return pl.pallas_call(
        kernel,
        grid=(t // tm, n_j),
        in_specs=[
            pl.BlockSpec((halo, D_MODEL), lambda i, j: (jnp.maximum(i * halo_blocks - 1, 0), 0)),
            pl.BlockSpec((tm, D_MODEL), lambda i, j: (i, 0)),
            pl.BlockSpec((halo, D_MODEL),
                         lambda i, j: (jnp.minimum((i + 1) * halo_blocks, last_halo_block), 0)),
            pl.BlockSpec((D_MODEL, tf), lambda i, j: (0, j)),
            pl.BlockSpec((D_MODEL, tf), lambda i, j: (0, n_j + j)),
            pl.BlockSpec((3, tf), lambda i, j: (0, j)),
            pl.BlockSpec((3, tf), lambda i, j: (0, n_j + j)),
            pl.BlockSpec((1, tf), lambda i, j: (0, j)),
            pl.BlockSpec((1, tf), lambda i, j: (0, n_j + j)),
        ],
        out_specs=pl.BlockSpec((None, tm, tf), lambda i, j: (j, i, 0)),
        out_shape=jax.ShapeDtypeStruct((n_j, t, tf), BF16),
        scratch_shapes=[pltpu.VMEM((tm + 2 * halo, D_MODEL), BF16)],
        compiler_params=pltpu.CompilerParams(
            dimension_semantics=("parallel", "arbitrary"),
            vmem_limit_bytes=VMEM_LIMIT),
        name="ffn_up",
    )(h2, h2, h2, w_up_bf16, w_up_bf16, conv_w, conv_w,
      conv_b.reshape(1, 2 * D_FF), conv_b.reshape(1, 2 * D_FF))


def _ffn_down_kernel(act_ref, w_ref, x2_ref, fw_ref, o_ref):
    act = jnp.concatenate([act_ref[j] for j in range(act_ref.shape[0])], axis=1)
    y = x2_ref[...] + jnp.dot(act, w_ref[...], preferred_element_type=F32)
    o_ref[...] = _rms(y, fw_ref[...])


def _ffn_down(act, w_down_bf16, x2, final_w, *, tm=512):
    n_j, t, tf = act.shape
    return pl.pallas_call(
        _ffn_down_kernel,
        grid=(t // tm,),
        in_specs=[
            pl.BlockSpec((n_j, tm, tf), lambda i: (0, i, 0)),
            pl.BlockSpec((D_FF, D_MODEL), lambda i: (0, 0), pipeline_mode=pl.Buffered(1)),
            pl.BlockSpec((tm, D_MODEL), lambda i: (i, 0)),
            pl.BlockSpec((1, D_MODEL), lambda i: (0, 0)),
        ],
        out_specs=pl.BlockSpec((tm, D_MODEL), lambda i: (i, 0)),
        out_shape=jax.ShapeDtypeStruct((t, D_MODEL), F32),
        compiler_params=pltpu.CompilerParams(
            dimension_semantics=("parallel",),
            vmem_limit_bytes=VMEM_LIMIT),
        name="ffn_down",
    )(act, w_down_bf16, x2, final_w.reshape(1, D_MODEL))


def _ffn(h2, x2, w_up_bf16, conv_w, conv_b, w_down_bf16, final_w, *, seq):
    act = _ffn_up(h2, w_up_bf16, conv_w, conv_b, seq=seq)
    return _ffn_down(act, w_down_bf16, x2, final_w)


def _rope_tables(seq):
    inv_freq = ROPE_BASE ** (-np.arange(0, HEAD_DIM, 2, dtype=np.float64) / HEAD_DIM)
    ang = np.arange(seq, dtype=np.float64)[:, None] * inv_freq[None, :]
    cos = np.cos(ang).astype(np.float32)
    sin = np.sin(ang).astype(np.float32)
    return (jnp.asarray(np.concatenate([cos, cos], axis=-1)),
            jnp.asarray(np.concatenate([-sin, sin], axis=-1)))


def _per_slab(vec, slabs):
    return jnp.broadcast_to(vec.reshape(slabs, 1, LANES), (slabs, 8, LANES))


def _trunk(x, p):
    batch, seq, _ = x.shape
    xt = x.reshape(batch * seq, D_MODEL)
    z = _in_proj(xt, p["norm1_w"], p["w_in"], p["cos"], p["sin"], seq=seq)
    ret = _retention(z, p["lgf"], p["lgb"], p["gnw"], batch=batch, seq=seq)
    x2, h2 = _out_proj(ret, z, p["sgu_nw"], p["w_s"], p["sgu_b"], xt, p["w_out"], p["norm2_w"])
    y = _ffn(h2, x2, p["w_up"], p["conv_w"], p["conv_b"], p["w_down"], p["final_w"], seq=seq)
    return y.reshape(batch, seq, D_MODEL)


def kernel(x_prompt, x_sample, norm1_w, w_in, ret_log_decay_fwd, ret_log_decay_bwd, ret_gn_w, sgu_norm_w, sgu_w_s, sgu_b, w_out, norm2_w, w_up, conv_w, conv_b, w_down, final_norm_w):
    max_seq = max(x_prompt.shape[1], x_sample.shape[1])
    cos, sin = _rope_tables(max_seq)
    p = {
        "norm1_w": norm1_w[0],
        "w_in": w_in[0].astype(BF16),
        "cos": cos,
        "sin": sin,
        "lgf": jnp.broadcast_to(ret_log_decay_fwd[0].astype(F32)[:, None, None], (HEADS, 8, LANES)),
        "lgb": jnp.broadcast_to(ret_log_decay_bwd[0].astype(F32)[:, None, None], (HEADS, 8, LANES)),
        "gnw": _per_slab(ret_gn_w[0], HEADS),
        "sgu_nw": _per_slab(sgu_norm_w[0], GROUPS),
        "w_s": sgu_w_s[0].astype(BF16),
        "sgu_b": jnp.broadcast_to(sgu_b[0][:, :, None], (GROUPS, CHUNK, LANES)),
        "w_out": w_out[0].astype(BF16),
        "norm2_w": norm2_w[0],
        "w_up": w_up[0].astype(BF16),
        "conv_w": conv_w[0],
        "conv_b": conv_b[0],
        "w_down": w_down[0].astype(BF16),
        "final_w": final_norm_w,
    }
    return (_trunk(x_prompt, p), _trunk(x_sample, p))
```

```python
import functools
import math

import jax
import jax.numpy as jnp
import numpy as np
from jax import lax
from jax.experimental import pallas as pl
from jax.experimental.pallas import tpu as pltpu

D_MODEL = 2048
CHUNK = 128
HEADS = 8
HEAD_DIM = 128
RET_WIDTH = HEADS * HEAD_DIM
GROUPS = 8
SGU_WIDTH = GROUPS * CHUNK
IN_COLS = 4 * RET_WIDTH + 2 * SGU_WIDTH
IN_SLABS = IN_COLS // 128
D_FF = 5632
ROPE_BASE = 10000.0
EPS = 1e-6
LANES = 128
BF16_ROWS = 16
IN_PROJ_TILE = 2048
FFN_TILE = 512
FFN_ROWS = 1024
RET_UNROLL = 32
VMEM_LIMIT = 56 * 1024 * 1024

F32 = jnp.float32
BF16 = jnp.bfloat16


def _rms(x, w):
    ms = jnp.mean(x * x, axis=-1, keepdims=True)
    return x * lax.rsqrt(ms + EPS) * w


def _gelu_tanh(x):
    c = math.sqrt(2.0 / math.pi)
    h = 0.5 * x
    return h + h * jnp.tanh(x * (c + (c * 0.044715) * (x * x)))


def _silu(x):
    h = 0.5 * x
    return h + h * jnp.tanh(h)


def _slab_epilogue(slab):
    if slab < 2 * HEADS:
        return "rotary"
    if slab < 3 * HEADS:
        return "none"
    if slab < 4 * HEADS:
        return "silu"
    return "gelu"


def _inproj_kernel(x_ref, nw_ref, w_ref, cos_ref, sin_ref, z_ref, h_ref):
    j = pl.program_id(1)
    n_slabs = z_ref.shape[0]

    @pl.when(j == 0)
    def _():
        h_ref[...] = _rms(x_ref[...], nw_ref[...]).astype(BF16)

    def column_tile(jj):
        res = jnp.dot(h_ref[...], w_ref[...], preferred_element_type=F32)
        for s in range(n_slabs):
            y = res[:, s * LANES:(s + 1) * LANES]
            kind = _slab_epilogue(jj * n_slabs + s)
            if kind == "rotary":
                y = y * cos_ref[...] + pltpu.roll(y, HEAD_DIM // 2, 1) * sin_ref[...]
            elif kind == "silu":
                y = _silu(y)
            elif kind == "gelu":
                y = _gelu_tanh(y)
            z_ref[s] = y.astype(z_ref.dtype)

    for jj in range(IN_SLABS // n_slabs):
        pl.when(j == jj)(functools.partial(column_tile, jj))


def _in_proj(x, norm_w, w_bf16, cos, sin, *, seq, tm=1024, tn=IN_PROJ_TILE):
    t = x.shape[0]
    tiles_per_seq = seq // tm
    return pl.pallas_call(
        _inproj_kernel,
        grid=(t // tm, IN_COLS // tn),
        in_specs=[
            pl.BlockSpec((tm, D_MODEL), lambda i, j: (i, 0)),
            pl.BlockSpec((1, D_MODEL), lambda i, j: (0, 0)),
            pl.BlockSpec((D_MODEL, tn), lambda i, j: (0, j)),
            pl.BlockSpec((tm, LANES), lambda i, j: (i % tiles_per_seq, 0)),
            pl.BlockSpec((tm, LANES), lambda i, j: (i % tiles_per_seq, 0)),
        ],
        out_specs=pl.BlockSpec((tn // LANES, tm, LANES), lambda i, j: (j, i, 0)),
        out_shape=jax.ShapeDtypeStruct((IN_SLABS, t, LANES), BF16),
        scratch_shapes=[pltpu.VMEM((tm, D_MODEL), BF16)],
        compiler_params=pltpu.CompilerParams(
            dimension_semantics=("parallel", "arbitrary"),
            vmem_limit_bytes=VMEM_LIMIT),
        name="in_proj",
    )(x, norm_w.reshape(1, D_MODEL), w_bf16, cos, sin)


def _ret_kernel(q_ref, k_ref, v_ref, g_ref, lgf_ref, lgb_ref, gnw_ref,
                o_ref, qx_ref, kt_ref, kvf_ref, rb_ref):
    seq = q_ref.shape[1]
    n_chunks = seq // CHUNK
    lgf = lgf_ref[0, 0:1, :]
    lgb = lgb_ref[0, 0:1, :]
    row = lax.broadcasted_iota(jnp.int32, (CHUNK, CHUNK), 0).astype(F32)
    col = lax.broadcasted_iota(jnp.int32, (CHUNK, CHUNK), 1).astype(F32)
    diff = row - col
    scale = HEAD_DIM ** -0.5
    dcomb = jnp.where(diff >= 0, jnp.exp(lgf * diff), jnp.exp(lgb * (-diff))) * scale
    xi_f = jnp.exp(lgf * (row + 1.0)) * scale
    xi_b = jnp.exp(lgb * (CHUNK - row)) * scale
    zeta_f = jnp.exp(lgf * (CHUNK - 1.0 - row))
    zeta_b = jnp.exp(lgb * row)
    decay_f = jnp.exp(lgf * CHUNK)
    decay_b = jnp.exp(lgb * CHUNK)

    def rows(c):
        return pl.ds(pl.multiple_of(c * CHUNK, CHUNK), CHUNK)

    def prep(i, state):
        c = n_chunks - 1 - i
        sl = rows(c)
        q = q_ref[0, sl, :].astype(F32)
        k = k_ref[0, sl, :].astype(F32)
        kt_ref[c] = k_ref[0, sl, :].T
        qx_ref[sl, :] = jnp.concatenate([q * xi_f, q * xi_b], axis=1).astype(BF16)
        kz = jnp.concatenate([k * zeta_f, k * zeta_b], axis=1).astype(BF16)
        kv = lax.dot_general(kz, v_ref[0, sl, :], (((0,), (0,)), ((), ())),
                             preferred_element_type=F32)
        kvf_ref[c] = kv[:HEAD_DIM]
        rb_ref[c] = state.astype(BF16)
        return state * decay_b + kv[HEAD_DIM:]

    lax.fori_loop(0, n_chunks, prep, jnp.zeros((HEAD_DIM, HEAD_DIM), F32), unroll=RET_UNROLL)

    gnw = gnw_ref[0, 0:1, :]

    def fwd(c, state):
        sl = rows(c)
        s = jnp.dot(q_ref[0, sl, :], kt_ref[c], preferred_element_type=F32) * dcomb
        lhs = jnp.concatenate([s.astype(BF16), qx_ref[sl, :]], axis=1)
        rhs = jnp.concatenate([v_ref[0, sl, :], state.astype(BF16), rb_ref[c]], axis=0)
        o = jnp.dot(lhs, rhs, preferred_element_type=F32)
        mu = jnp.mean(o, axis=-1, keepdims=True)
        d = o - mu
        var = jnp.mean(d * d, axis=-1, keepdims=True)
        on = d * lax.rsqrt(var + EPS) * gnw
        o_ref[0, sl, :] = (g_ref[0, sl, :].astype(F32) * on).astype(o_ref.dtype)
        return state * decay_f + kvf_ref[c]

    lax.fori_loop(0, n_chunks, fwd, jnp.zeros((HEAD_DIM, HEAD_DIM), F32), unroll=RET_UNROLL)


def _retention(z, lgf, lgb, gnw, *, batch, seq):
    t = batch * seq
    n_chunks = seq // CHUNK

    def zspec(base):
        return pl.BlockSpec((1, seq, LANES), lambda b, h: (base + h, b, 0))

    per_head = pl.BlockSpec((1, 8, LANES), lambda b, h: (h, 0, 0))
    return pl.pallas_call(
        _ret_kernel,
        grid=(batch, HEADS),
        in_specs=[zspec(0), zspec(HEADS), zspec(2 * HEADS), zspec(3 * HEADS),
                  per_head, per_head, per_head],
        out_specs=pl.BlockSpec((1, seq, LANES), lambda b, h: (h, b, 0)),
        out_shape=jax.ShapeDtypeStruct((HEADS, t, LANES), BF16),
        scratch_shapes=[
            pltpu.VMEM((seq, 2 * HEAD_DIM), BF16),
            pltpu.VMEM((n_chunks, HEAD_DIM, CHUNK), BF16),
            pltpu.VMEM((n_chunks, HEAD_DIM, HEAD_DIM), F32),
            pltpu.VMEM((n_chunks, HEAD_DIM, HEAD_DIM), BF16),
        ],
        compiler_params=pltpu.CompilerParams(
            dimension_semantics=("parallel", "parallel"),
            vmem_limit_bytes=VMEM_LIMIT),
        name="retention",
    )(z, z, z, z, lgf, lgb, gnw)


def _sgu_pointwise(zu_ref, zv_ref, nw_ref, u_ref, vn_ref):
    v = zv_ref[...].astype(F32)
    ms = jnp.sum(jnp.sum(v * v, axis=0), axis=-1, keepdims=True) * (1.0 / SGU_WIDTH)
    inv = lax.rsqrt(ms + EPS)
    for g in range(GROUPS):
        vn_ref[g] = (v[g] * inv * nw_ref[g, 0:1, :]).astype(BF16)
        u_ref[g] = zu_ref[g].astype(F32)


def _sgu_mix(u_ref, vn_ref, ws_ref, b_ref):
    n_chunks = vn_ref.shape[1] // CHUNK
    groups = []
    for g in range(GROUPS):
        rhs = jnp.concatenate(
            [vn_ref[g, c * CHUNK:(c + 1) * CHUNK, :] for c in range(n_chunks)], axis=1)
        sp = jnp.dot(ws_ref[g], rhs, preferred_element_type=F32)
        bias = b_ref[g]
        groups.append(jnp.concatenate(
            [u_ref[g, c * CHUNK:(c + 1) * CHUNK, :] * (sp[:, c * CHUNK:(c + 1) * CHUNK] + bias)
             for c in range(n_chunks)], axis=0).astype(BF16))
    return jnp.concatenate(groups, axis=1)


def _outproj_kernel(ret_ref, zu_ref, zv_ref, sgu_nw_ref, ws_ref, sb_ref, x_ref, w_ref, nw_ref,
                    x2_ref, h2_ref, u_ref, vn_ref):
    _sgu_pointwise(zu_ref, zv_ref, sgu_nw_ref, u_ref, vn_ref)
    ret = jnp.concatenate([ret_ref[s] for s in range(HEADS)], axis=1)
    x2_ref[...] = x_ref[...] + jnp.dot(ret, w_ref[0:RET_WIDTH, :], preferred_element_type=F32)
    sgu = _sgu_mix(u_ref, vn_ref, ws_ref, sb_ref)
    x2 = x2_ref[...] + jnp.dot(sgu, w_ref[RET_WIDTH:, :], preferred_element_type=F32)
    x2_ref[...] = x2
    h2_ref[...] = _rms(x2, nw_ref[...]).astype(BF16)


def _out_proj(ret, z, sgu_nw, ws_bf16, sgu_bias, x, w_bf16, norm_w, *, tm=512):
    t = x.shape[0]
    u_base = 4 * HEADS // GROUPS
    const3 = lambda i: (0, 0, 0)
    return pl.pallas_call(
        _outproj_kernel,
        grid=(t // tm,),
        in_specs=[
            pl.BlockSpec((HEADS, tm, LANES), lambda i: (0, i, 0)),
            pl.BlockSpec((GROUPS, tm, LANES), lambda i: (u_base, i, 0)),
            pl.BlockSpec((GROUPS, tm, LANES), lambda i: (u_base + 1, i, 0)),
            pl.BlockSpec((GROUPS, 8, LANES), const3),
            pl.BlockSpec((GROUPS, CHUNK, CHUNK), const3),
            pl.BlockSpec((GROUPS, CHUNK, LANES), const3),
            pl.BlockSpec((tm, D_MODEL), lambda i: (i, 0)),
            pl.BlockSpec((D_MODEL, D_MODEL), lambda i: (0, 0)),
            pl.BlockSpec((1, D_MODEL), lambda i: (0, 0)),
        ],
        out_specs=[pl.BlockSpec((tm, D_MODEL), lambda i: (i, 0)),
                   pl.BlockSpec((tm, D_MODEL), lambda i: (i, 0))],
        out_shape=[jax.ShapeDtypeStruct((t, D_MODEL), F32),
                   jax.ShapeDtypeStruct((t, D_MODEL), BF16)],
        scratch_shapes=[pltpu.VMEM((GROUPS, tm, LANES), F32),
                        pltpu.VMEM((GROUPS, tm, LANES), BF16)],
        compiler_params=pltpu.CompilerParams(
            dimension_semantics=("parallel",),
            vmem_limit_bytes=VMEM_LIMIT),
        name="out_proj",
    )(ret, z, z, sgu_nw, ws_bf16, sgu_bias, x, w_bf16, norm_w.reshape(1, D_MODEL))


def _ffn_up_kernel(hp_ref, h_ref, hn_ref, wa_ref, wb_ref, cwa_ref, cwb_ref, cba_ref,
                   cbb_ref, o_ref, lhs_ref, *, tiles_per_seq):
    i = pl.program_id(0)
    j = pl.program_id(1)
    tm = h_ref.shape[0]
    halo = hp_ref.shape[0]

    @pl.when(j == 0)
    def _():
        pos = i % tiles_per_seq
        hp = hp_ref[...]
        hn = hn_ref[...]
        lhs_ref[0:halo, :] = jnp.where(pos == 0, jnp.zeros_like(hp), hp)
        lhs_ref[halo:halo + tm, :] = h_ref[...]
        lhs_ref[halo + tm:, :] = jnp.where(pos == tiles_per_seq - 1, jnp.zeros_like(hn), hn)

    lhs = lhs_ref[...]
    n_rows = tm + 2 * halo

    def conv_branch(w_ref, cw_ref, cb_ref):
        u = jnp.dot(lhs, w_ref[...], preferred_element_type=F32)
        prev = pltpu.roll(u, 1, 0)[halo:halo + tm]
        nxt = pltpu.roll(u, n_rows - 1, 0)[halo:halo + tm]
        cur = u[halo:halo + tm]
        return prev * cw_ref[0:1, :] + cur * cw_ref[1:2, :] + nxt * cw_ref[2:3, :] + cb_ref[...]

    a = conv_branch(wa_ref, cwa_ref, cba_ref)
    b = conv_branch(wb_ref, cwb_ref, cbb_ref)
    o_ref[...] = (_silu(a) * b).astype(o_ref.dtype)


def _ffn_up(h2, w_up_bf16, conv_w, conv_b, *, seq, tm=FFN_ROWS, tf=FFN_TILE):
    t = h2.shape[0]
    halo = BF16_ROWS
    n_j = D_FF // tf
    halo_blocks = tm // halo
    last_halo_block = t // halo - 1
    kernel = functools.partial(_ffn_up_kernel, tiles_per_seq=seq // tm)
    return pl.pallas_call(
        kernel,
        grid=(t // tm, n_j),
        in_specs=[
            pl.BlockSpec((halo, D_MODEL), lambda i, j: (jnp.maximum(i * halo_blocks - 1, 0), 0)),
            pl.BlockSpec((tm, D_MODEL), lambda i, j: (i, 0)),
            pl.BlockSpec((halo, D_MODEL),
                         lambda i, j: (jnp.minimum((i + 1) * halo_blocks, last_halo_block), 0)),
            pl.BlockSpec((D_MODEL, tf), lambda i, j: (0, j)),
            pl.BlockSpec((D_MODEL, tf), lambda i, j: (0, n_j + j)),
            pl.BlockSpec((3, tf), lambda i, j: (0, j)),
            pl.BlockSpec((3, tf), lambda i, j: (0, n_j + j)),
            pl.BlockSpec((1, tf), lambda i, j: (0, j)),
            pl.BlockSpec((1, tf), lambda i, j: (0, n_j + j)),
        ],
        out_specs=pl.BlockSpec((None, tm, tf), lambda i, j: (j, i, 0)),
        out_shape=jax.ShapeDtypeStruct((n_j, t, tf), BF16),
        scratch_shapes=[pltpu.VMEM((tm + 2 * halo, D_MODEL), BF16)],
        compiler_params=pltpu.CompilerParams(
            dimension_semantics=("parallel", "arbitrary"),
            vmem_limit_bytes=VMEM_LIMIT),
        name="ffn_up",
    )(h2, h2, h2, w_up_bf16, w_up_bf16, conv_w, conv_w,
      conv_b.reshape(1, 2 * D_FF), conv_b.reshape(1, 2 * D_FF))


def _ffn_down_kernel(act_ref, w_ref, x2_ref, fw_ref, o_ref):
    act = jnp.concatenate([act_ref[j] for j in range(act_ref.shape[0])], axis=1)
    y = x2_ref[...] + jnp.dot(act, w_ref[...], preferred_element_type=F32)
    o_ref[...] = _rms(y, fw_ref[...])


def _ffn_down(act, w_down_bf16, x2, final_w, *, tm=512):
    n_j, t, tf = act.shape
    return pl.pallas_call(
        _ffn_down_kernel,
        grid=(t // tm,),
        in_specs=[
            pl.BlockSpec((n_j, tm, tf), lambda i: (0, i, 0)),
            pl.BlockSpec((D_FF, D_MODEL), lambda i: (0, 0), pipeline_mode=pl.Buffered(1)),
            pl.BlockSpec((tm, D_MODEL), lambda i: (i, 0)),
            pl.BlockSpec((1, D_MODEL), lambda i: (0, 0)),
        ],
        out_specs=pl.BlockSpec((tm, D_MODEL), lambda i: (i, 0)),
        out_shape=jax.ShapeDtypeStruct((t, D_MODEL), F32),
        compiler_params=pltpu.CompilerParams(
            dimension_semantics=("parallel",),
            vmem_limit_bytes=VMEM_LIMIT),
        name="ffn_down",
    )(act, w_down_bf16, x2, final_w.reshape(1, D_MODEL))


def _ffn(h2, x2, w_up_bf16, conv_w, conv_b, w_down_bf16, final_w, *, seq):
    act = _ffn_up(h2, w_up_bf16, conv_w, conv_b, seq=seq)
    return _ffn_down(act, w_down_bf16, x2, final_w)


def _rope_tables(seq):
    inv_freq = ROPE_BASE ** (-np.arange(0, HEAD_DIM, 2, dtype=np.float64) / HEAD_DIM)
    ang = np.arange(seq, dtype=np.float64)[:, None] * inv_freq[None, :]
    cos = np.cos(ang).astype(np.float32)
    sin = np.sin(ang).astype(np.float32)
    return (jnp.asarray(np.concatenate([cos, cos], axis=-1)),
            jnp.asarray(np.concatenate([-sin, sin], axis=-1)))


def _per_slab(vec, slabs):
    return jnp.broadcast_to(vec.reshape(slabs, 1, LANES), (slabs, 8, LANES))


def _trunk(x, p):
    batch, seq, _ = x.shape
    xt = x.reshape(batch * seq, D_MODEL)
    z = _in_proj(xt, p["norm1_w"], p["w_in"], p["cos"], p["sin"], seq=seq)
    ret = _retention(z, p["lgf"], p["lgb"], p["gnw"], batch=batch, seq=seq)
    x2, h2 = _out_proj(ret, z, p["sgu_nw"], p["w_s"], p["sgu_b"], xt, p["w_out"], p["norm2_w"])
    y = _ffn(h2, x2, p["w_up"], p["conv_w"], p["conv_b"], p["w_down"], p["final_w"], seq=seq)
    return y.reshape(batch, seq, D_MODEL)


def kernel(x_prompt, x_sample, norm1_w, w_in, ret_log_decay_fwd, ret_log_decay_bwd, ret_gn_w, sgu_norm_w, sgu_w_s, sgu_b, w_out, norm2_w, w_up, conv_w, conv_b, w_down, final_norm_w):
    max_seq = max(x_prompt.shape[1], x_sample.shape[1])
    cos, sin = _rope_tables(max_seq)
    p = {
        "norm1_w": norm1_w[0],
        "w_in": w_in[0].astype(BF16),
        "cos": cos,
        "sin": sin,
        "lgf": jnp.broadcast_to(ret_log_decay_fwd[0].astype(F32)[:, None, None], (HEADS, 8, LANES)),
        "lgb": jnp.broadcast_to(ret_log_decay_bwd[0].astype(F32)[:, None, None], (HEADS, 8, LANES)),
        "gnw": _per_slab(ret_gn_w[0], HEADS),
        "sgu_nw": _per_slab(sgu_norm_w[0], GROUPS),
        "w_s": sgu_w_s[0].astype(BF16),
        "sgu_b": jnp.broadcast_to(sgu_b[0][:, :, None], (GROUPS, CHUNK, LANES)),
        "w_out": w_out[0].astype(BF16),
        "norm2_w": norm2_w[0],
        "w_up": w_up[0].astype(BF16),
        "conv_w": conv_w[0],
        "conv_b": conv_b[0],
        "w_down": w_down[0].astype(BF16),
        "final_w": final_norm_w,
    }
    return (_trunk(x_prompt, p), _trunk(x_sample, p))
```

```python
import functools
import math

import jax
import jax.numpy as jnp
import numpy as np
from jax import lax
from jax.experimental import pallas as pl
from jax.experimental.pallas import tpu as pltpu

D_MODEL = 2048
CHUNK = 128
HEADS = 8
HEAD_DIM = 128
RET_WIDTH = HEADS * HEAD_DIM
GROUPS = 8
SGU_WIDTH = GROUPS * CHUNK
IN_COLS = 4 * RET_WIDTH + 2 * SGU_WIDTH
IN_SLABS = IN_COLS // 128
D_FF = 5632
ROPE_BASE = 10000.0
EPS = 1e-6
LANES = 128
BF16_ROWS = 16
IN_PROJ_TILE = 2048
FFN_TILE = 512
FFN_ROWS = 1024
RET_UNROLL = 32
VMEM_LIMIT = 56 * 1024 * 1024

F32 = jnp.float32
BF16 = jnp.bfloat16


def _rms(x, w):
    ms = jnp.mean(x * x, axis=-1, keepdims=True)
    return x * lax.rsqrt(ms + EPS) * w


def _gelu_tanh(x):
    c = math.sqrt(2.0 / math.pi)
    h = 0.5 * x
    return h + h * jnp.tanh(x * (c + (c * 0.044715) * (x * x)))


def _silu(x):
    h = 0.5 * x
    return h + h * jnp.tanh(h)


def _slab_epilogue(slab):
    if slab < 2 * HEADS:
        return "rotary"
    if slab < 3 * HEADS:
        return "none"
    if slab < 4 * HEADS:
        return "silu"
    return "gelu"


def _inproj_kernel(x_ref, nw_ref, w_ref, cos_ref, sin_ref, z_ref, h_ref):
    j = pl.program_id(1)
    n_slabs = z_ref.shape[0]

    @pl.when(j == 0)
    def _():
        h_ref[...] = _rms(x_ref[...], nw_ref[...]).astype(BF16)

    def column_tile(jj):
        res = jnp.dot(h_ref[...], w_ref[...], preferred_element_type=F32)
        for s in range(n_slabs):
            y = res[:, s * LANES:(s + 1) * LANES]
            kind = _slab_epilogue(jj * n_slabs + s)
            if kind == "rotary":
                y = y * cos_ref[...] + pltpu.roll(y, HEAD_DIM // 2, 1) * sin_ref[...]
            elif kind == "silu":
                y = _silu(y)
            elif kind == "gelu":
                y = _gelu_tanh(y)
            z_ref[s] = y.astype(z_ref.dtype)

    for jj in range(IN_SLABS // n_slabs):
        pl.when(j == jj)(functools.partial(column_tile, jj))


def _in_proj(x, norm_w, w_bf16, cos, sin, *, seq, tm=1024, tn=IN_PROJ_TILE):
    t = x.shape[0]
    tiles_per_seq = seq // tm
    return pl.pallas_call(
        _inproj_kernel,
        grid=(t // tm, IN_COLS // tn),
        in_specs=[
            pl.BlockSpec((tm, D_MODEL), lambda i, j: (i, 0)),
            pl.BlockSpec((1, D_MODEL), lambda i, j: (0, 0)),
            pl.BlockSpec((D_MODEL, tn), lambda i, j: (0, j)),
            pl.BlockSpec((tm, LANES), lambda i, j: (i % tiles_per_seq, 0)),
            pl.BlockSpec((tm, LANES), lambda i, j: (i % tiles_per_seq, 0)),
        ],
        out_specs=pl.BlockSpec((tn // LANES, tm, LANES), lambda i, j: (j, i, 0)),
        out_shape=jax.ShapeDtypeStruct((IN_SLABS, t, LANES), BF16),
        scratch_shapes=[pltpu.VMEM((tm, D_MODEL), BF16)],
        compiler_params=pltpu.CompilerParams(
            dimension_semantics=("parallel", "arbitrary"),
            vmem_limit_bytes=VMEM_LIMIT),
        name="in_proj",
    )(x, norm_w.reshape(1, D_MODEL), w_bf16, cos, sin)


def _ret_kernel(q_ref, k_ref, v_ref, g_ref, lgf_ref, lgb_ref, gnw_ref,
                o_ref, qx_ref, kt_ref, kvf_ref, rb_ref):
    seq = q_ref.shape[1]
    n_chunks = seq // CHUNK
    lgf = lgf_ref[0, 0:1, :]
    lgb = lgb_ref[0, 0:1, :]
    row = lax.broadcasted_iota(jnp.int32, (CHUNK, CHUNK), 0).astype(F32)
    col = lax.broadcasted_iota(jnp.int32, (CHUNK, CHUNK), 1).astype(F32)
    diff = row - col
    scale = HEAD_DIM ** -0.5
    dcomb = jnp.where(diff >= 0, jnp.exp(lgf * diff), jnp.exp(lgb * (-diff))) * scale
    xi_f = jnp.exp(lgf * (row + 1.0)) * scale
    xi_b = jnp.exp(lgb * (CHUNK - row)) * scale
    zeta_f = jnp.exp(lgf * (CHUNK - 1.0 - row))
    zeta_b = jnp.exp(lgb * row)
    decay_f = jnp.exp(lgf * CHUNK)
    decay_b = jnp.exp(lgb * CHUNK)

    def rows(c):
        return pl.ds(pl.multiple_of(c * CHUNK, CHUNK), CHUNK)

    def prep(i, state):
        c = n_chunks - 1 - i
        sl = rows(c)
        q = q_ref[0, sl, :].astype(F32)
        k = k_ref[0, sl, :].astype(F32)
        kt_ref[c] = k_ref[0, sl, :].T
        qx_ref[sl, :] = jnp.concatenate([q * xi_f, q * xi_b], axis=1).astype(BF16)
        kz = jnp.concatenate([k * zeta_f, k * zeta_b], axis=1).astype(BF16)
        kv = lax.dot_general(kz, v_ref[0, sl, :], (((0,), (0,)), ((), ())),
                             preferred_element_type=F32)
        kvf_ref[c] = kv[:HEAD_DIM]
        rb_ref[c] = state.astype(BF16)
        return state * decay_b + kv[HEAD_DIM:]

    lax.fori_loop(0, n_chunks, prep, jnp.zeros((HEAD_DIM, HEAD_DIM), F32), unroll=RET_UNROLL)

    gnw = gnw_ref[0, 0:1, :]

    def fwd(c, state):
        sl = rows(c)
        s = jnp.dot(q_ref[0, sl, :], kt_ref[c], preferred_element_type=F32) * dcomb
        lhs = jnp.concatenate([s.astype(BF16), qx_ref[sl, :]], axis=1)
        rhs = jnp.concatenate([v_ref[0, sl, :], state.astype(BF16), rb_ref[c]], axis=0)
        o = jnp.dot(lhs, rhs, preferred_element_type=F32)
        mu = jnp.mean(o, axis=-1, keepdims=True)
        d = o - mu
        var = jnp.mean(d * d, axis=-1, keepdims=True)
        on = d * lax.rsqrt(var + EPS) * gnw
        o_ref[0, sl, :] = (g_ref[0, sl, :].astype(F32) * on).astype(o_ref.dtype)
        return state * decay_f + kvf_ref[c]

    lax.fori_loop(0, n_chunks, fwd, jnp.zeros((HEAD_DIM, HEAD_DIM), F32), unroll=RET_UNROLL)


def _retention(z, lgf, lgb, gnw, *, batch, seq):
    t = batch * seq
    n_chunks = seq // CHUNK

    def zspec(base):
        return pl.BlockSpec((1, seq, LANES), lambda b, h: (base + h, b, 0))

    per_head = pl.BlockSpec((1, 8, LANES), lambda b, h: (h, 0, 0))
    return pl.pallas_call(
        _ret_kernel,
        grid=(batch, HEADS),
        in_specs=[zspec(0), zspec(HEADS), zspec(2 * HEADS), zspec(3 * HEADS),
                  per_head, per_head, per_head],
        out_specs=pl.BlockSpec((1, seq, LANES), lambda b, h: (h, b, 0)),
        out_shape=jax.ShapeDtypeStruct((HEADS, t, LANES), BF16),
        scratch_shapes=[
            pltpu.VMEM((seq, 2 * HEAD_DIM), BF16),
            pltpu.VMEM((n_chunks, HEAD_DIM, CHUNK), BF16),
            pltpu.VMEM((n_chunks, HEAD_DIM, HEAD_DIM), F32),
            pltpu.VMEM((n_chunks, HEAD_DIM, HEAD_DIM), BF16),
        ],
        compiler_params=pltpu.CompilerParams(
            dimension_semantics=("parallel", "parallel"),
            vmem_limit_bytes=VMEM_LIMIT),
        name="retention",
    )(z, z, z, z, lgf, lgb, gnw)


def _sgu_norm(zv_ref, nw_ref):
    v = zv_ref[...].astype(F32)
    ms = jnp.sum(jnp.sum(v * v, axis=0), axis=-1, keepdims=True) * (1.0 / SGU_WIDTH)
    inv = lax.rsqrt(ms + EPS)
    return [(v[g] * inv * nw_ref[g, 0:1, :]).astype(BF16) for g in range(GROUPS)]


def _sgu_mix(zu_ref, vn, ws_ref, b_ref):
    n_chunks = zu_ref.shape[1] // CHUNK
    groups = []
    for g in range(GROUPS):
        rhs = jnp.concatenate(
            [vn[g][c * CHUNK:(c + 1) * CHUNK] for c in range(n_chunks)], axis=1)
        sp = jnp.dot(ws_ref[g], rhs, preferred_element_type=F32)
        bias = b_ref[g]
        groups.append(jnp.concatenate(
            [zu_ref[g, c * CHUNK:(c + 1) * CHUNK, :].astype(F32)
             * (sp[:, c * CHUNK:(c + 1) * CHUNK] + bias)
             for c in range(n_chunks)], axis=0).astype(BF16))
    return jnp.concatenate(groups, axis=1)


def _outproj_kernel(ret_ref, zu_ref, zv_ref, sgu_nw_ref, ws_ref, sb_ref, x_ref, w_ref, nw_ref,
                    x2_ref, h2_ref):
    vn = _sgu_norm(zv_ref, sgu_nw_ref)
    ret = jnp.concatenate([ret_ref[s] for s in range(HEADS)], axis=1)
    x2_ref[...] = x_ref[...] + jnp.dot(ret, w_ref[0:RET_WIDTH, :], preferred_element_type=F32)
    sgu = _sgu_mix(zu_ref, vn, ws_ref, sb_ref)
    x2 = x2_ref[...] + jnp.dot(sgu, w_ref[RET_WIDTH:, :], preferred_element_type=F32)
    x2_ref[...] = x2
    h2_ref[...] = _rms(x2, nw_ref[...]).astype(BF16)


def _out_proj(ret, z, sgu_nw, ws_bf16, sgu_bias, x, w_bf16, norm_w, *, tm=512):
    t = x.shape[0]
    u_base = 4 * HEADS // GROUPS
    const3 = lambda i: (0, 0, 0)
    return pl.pallas_call(
        _outproj_kernel,
        grid=(t // tm,),
        in_specs=[
            pl.BlockSpec((HEADS, tm, LANES), lambda i: (0, i, 0)),
            pl.BlockSpec((GROUPS, tm, LANES), lambda i: (u_base, i, 0)),
            pl.BlockSpec((GROUPS, tm, LANES), lambda i: (u_base + 1, i, 0)),
            pl.BlockSpec((GROUPS, 8, LANES), const3),
            pl.BlockSpec((GROUPS, CHUNK, CHUNK), const3),
            pl.BlockSpec((GROUPS, CHUNK, LANES), const3),
            pl.BlockSpec((tm, D_MODEL), lambda i: (i, 0)),
            pl.BlockSpec((D_MODEL, D_MODEL), lambda i: (0, 0)),
            pl.BlockSpec((1, D_MODEL), lambda i: (0, 0)),
        ],
        out_specs=[pl.BlockSpec((tm, D_MODEL), lambda i: (i, 0)),
                   pl.BlockSpec((tm, D_MODEL), lambda i: (i, 0))],
        out_shape=[jax.ShapeDtypeStruct((t, D_MODEL), F32),
                   jax.ShapeDtypeStruct((t, D_MODEL), BF16)],
        compiler_params=pltpu.CompilerParams(
            dimension_semantics=("parallel",),
            vmem_limit_bytes=VMEM_LIMIT),
        name="out_proj",
    )(ret, z, z, sgu_nw, ws_bf16, sgu_bias, x, w_bf16, norm_w.reshape(1, D_MODEL))


def _ffn_up_kernel(hp_ref, h_ref, hn_ref, wa_ref, wb_ref, cwa_ref, cwb_ref, cba_ref,
                   cbb_ref, o_ref, lhs_ref, *, tiles_per_seq):
    i = pl.program_id(0)
    j = pl.program_id(1)
    tm = h_ref.shape[0]
    halo = hp_ref.shape[0]

    @pl.when(j == 0)
    def _():
        pos = i % tiles_per_seq
        hp = hp_ref[...]
        hn = hn_ref[...]
        lhs_ref[0:halo, :] = jnp.where(pos == 0, jnp.zeros_like(hp), hp)
        lhs_ref[halo:halo + tm, :] = h_ref[...]
        lhs_ref[halo + tm:, :] = jnp.where(pos == tiles_per_seq - 1, jnp.zeros_like(hn), hn)

    lhs = lhs_ref[...]
    n_rows = tm + 2 * halo

    def conv_branch(w_ref, cw_ref, cb_ref):
        u = jnp.dot(lhs, w_ref[...], preferred_element_type=F32)
        prev = pltpu.roll(u, 1, 0)[halo:halo + tm]
        nxt = pltpu.roll(u, n_rows - 1, 0)[halo:halo + tm]
        cur = u[halo:halo + tm]
        return prev * cw_ref[0:1, :] + cur * cw_ref[1:2, :] + nxt * cw_ref[2:3, :] + cb_ref[...]

    a = conv_branch(wa_ref, cwa_ref, cba_ref)
    b = conv_branch(wb_ref, cwb_ref, cbb_ref)
    o_ref[...] = (_silu(a) * b).astype(o_ref.dtype)


def _ffn_up(h2, w_up_bf16, conv_w, conv_b, *, seq, tm=FFN_ROWS, tf=FFN_TILE):
    t = h2.shape[0]
    halo = BF16_ROWS
    n_j = D_FF // tf
    halo_blocks = tm // halo
    last_halo_block = t // halo - 1
    kernel = functools.partial(_ffn_up_kernel, tiles_per_seq=seq // tm)
    return pl.pallas_call(
        kernel,
        grid=(t // tm, n_j),
        in_specs=[
            pl.BlockSpec((halo, D_MODEL), lambda i, j: (jnp.maximum(i * halo_blocks - 1, 0), 0)),
            pl.BlockSpec((tm, D_MODEL), lambda i, j: (i, 0)),
            pl.BlockSpec((halo, D_MODEL),
                         lambda i, j: (jnp.minimum((i + 1) * halo_blocks, last_halo_block), 0)),
            pl.BlockSpec((D_MODEL, tf), lambda i, j: (0, j)),
            pl.BlockSpec((D_MODEL, tf), lambda i, j: (0, n_j + j)),
            pl.BlockSpec((3, tf), lambda i, j: (0, j)),
            pl.BlockSpec((3, tf), lambda i, j: (0, n_j + j)),
            pl.BlockSpec((1, tf), lambda i, j: (0, j)),
            pl.BlockSpec((1, tf), lambda i, j: (0, n_j + j)),
        ],
        out_specs=pl.BlockSpec((None, tm, tf), lambda i, j: (j, i, 0)),
        out_shape=jax.ShapeDtypeStruct((n_j, t, tf), BF16),
        scratch_shapes=[pltpu.VMEM((tm + 2 * halo, D_MODEL), BF16)],
        compiler_params=pltpu.CompilerParams(
            dimension_semantics=("parallel", "arbitrary"),
            vmem_limit_bytes=VMEM_LIMIT),
        name="ffn_up",
    )(h2, h2, h2, w_up_bf16, w_up_bf16, conv_w, conv_w,
      conv_b.reshape(1, 2 * D_FF), conv_b.reshape(1, 2 * D_FF))


def _ffn_down_kernel(act_ref, w_ref, x2_ref, fw_ref, o_ref):
    act = jnp.concatenate([act_ref[j] for j in range(act_ref.shape[0])], axis=1)
    y = x2_ref[...] + jnp.dot(act, w_ref[...], preferred_element_type=F32)
    o_ref[...] = _rms(y, fw_ref[...])


def _ffn_down(act, w_down_bf16, x2, final_w, *, tm=512):
    n_j, t, tf = act.shape
    return pl.pallas_call(
        _ffn_down_kernel,
        grid=(t // tm,),
        in_specs=[
            pl.BlockSpec((n_j, tm, tf), lambda i: (0, i, 0)),
            pl.BlockSpec((D_FF, D_MODEL), lambda i: (0, 0), pipeline_mode=pl.Buffered(1)),
            pl.BlockSpec((tm, D_MODEL), lambda i: (i, 0)),
            pl.BlockSpec((1, D_MODEL), lambda i: (0, 0)),
        ],
        out_specs=pl.BlockSpec((tm, D_MODEL), lambda i: (i, 0)),
        out_shape=jax.ShapeDtypeStruct((t, D_MODEL), F32),
        compiler_params=pltpu.CompilerParams(
            dimension_semantics=("parallel",),
            vmem_limit_bytes=VMEM_LIMIT),
        name="ffn_down",
    )(act, w_down_bf16, x2, final_w.reshape(1, D_MODEL))


def _ffn(h2, x2, w_up_bf16, conv_w, conv_b, w_down_bf16, final_w, *, seq):
    act = _ffn_up(h2, w_up_bf16, conv_w, conv_b, seq=seq)
    return _ffn_down(act, w_down_bf16, x2, final_w)


def _rope_tables(seq):
    inv_freq = ROPE_BASE ** (-np.arange(0, HEAD_DIM, 2, dtype=np.float64) / HEAD_DIM)
    ang = np.arange(seq, dtype=np.float64)[:, None] * inv_freq[None, :]
    cos = np.cos(ang).astype(np.float32)
    sin = np.sin(ang).astype(np.float32)
    return (jnp.asarray(np.concatenate([cos, cos], axis=-1)),
            jnp.asarray(np.concatenate([-sin, sin], axis=-1)))


def _per_slab(vec, slabs):
    return jnp.broadcast_to(vec.reshape(slabs, 1, LANES), (slabs, 8, LANES))


def _trunk(x, p):
    batch, seq, _ = x.shape
    xt = x.reshape(batch * seq, D_MODEL)
    z = _in_proj(xt, p["norm1_w"], p["w_in"], p["cos"], p["sin"], seq=seq)
    ret = _retention(z, p["lgf"], p["lgb"], p["gnw"], batch=batch, seq=seq)
    x2, h2 = _out_proj(ret, z, p["sgu_nw"], p["w_s"], p["sgu_b"], xt, p["w_out"], p["norm2_w"])
    y = _ffn(h2, x2, p["w_up"], p["conv_w"], p["conv_b"], p["w_down"], p["final_w"], seq=seq)
    return y.reshape(batch, seq, D_MODEL)


def kernel(x_prompt, x_sample, norm1_w, w_in, ret_log_decay_fwd, ret_log_decay_bwd, ret_gn_w, sgu_norm_w, sgu_w_s, sgu_b, w_out, norm2_w, w_up, conv_w, conv_b, w_down, final_norm_w):
    max_seq = max(x_prompt.shape[1], x_sample.shape[1])
    cos, sin = _rope_tables(max_seq)
    p = {
        "norm1_w": norm1_w[0],
        "w_in": w_in[0].astype(BF16),
        "cos": cos,
        "sin": sin,
        "lgf": jnp.broadcast_to(ret_log_decay_fwd[0].astype(F32)[:, None, None], (HEADS, 8, LANES)),
        "lgb": jnp.broadcast_to(ret_log_decay_bwd[0].astype(F32)[:, None, None], (HEADS, 8, LANES)),
        "gnw": _per_slab(ret_gn_w[0], HEADS),
        "sgu_nw": _per_slab(sgu_norm_w[0], GROUPS),
        "w_s": sgu_w_s[0].astype(BF16),
        "sgu_b": jnp.broadcast_to(sgu_b[0][:, :, None], (GROUPS, CHUNK, LANES)),
        "w_out": w_out[0].astype(BF16),
        "norm2_w": norm2_w[0],
        "w_up": w_up[0].astype(BF16),
        "conv_w": conv_w[0],
        "conv_b": conv_b[0],
        "w_down": w_down[0].astype(BF16),
        "final_w": final_norm_w,
    }
    return (_trunk(x_prompt, p), _trunk(x_sample, p))
```

```python
import functools
import math

import jax
import jax.numpy as jnp
import numpy as np
from jax import lax
from jax.experimental import pallas as pl
from jax.experimental.pallas import tpu as pltpu

D_MODEL = 2048
CHUNK = 128
HEADS = 8
HEAD_DIM = 128
RET_WIDTH = HEADS * HEAD_DIM
GROUPS = 8
SGU_WIDTH = GROUPS * CHUNK
IN_COLS = 4 * RET_WIDTH + 2 * SGU_WIDTH
D_FF = 5632
ROPE_BASE = 10000.0
EPS = 1e-6
LANES = 128
SUBLANES = 8
BF16_ROWS = 2 * SUBLANES
IN_SLABS = IN_COLS // LANES
IN_PROJ_TILE = 2048
FFN_TILE = 512
FFN_ROWS = 1024
RET_UNROLL = 32
VMEM_LIMIT = 56 * 1024 * 1024

F32 = jnp.float32
BF16 = jnp.bfloat16


def _rms(x, w):
    ms = jnp.mean(x * x, axis=-1, keepdims=True)
    return x * lax.rsqrt(ms + EPS) * w


def _gelu_tanh(x):
    c = math.sqrt(2.0 / math.pi)
    h = 0.5 * x
    return h + h * jnp.tanh(x * (c + (c * 0.044715) * (x * x)))


def _silu(x):
    h = 0.5 * x
    return h + h * jnp.tanh(h)


def _slab_epilogue(slab):
    if slab < 2 * HEADS:
        return "rotary"
    if slab < 3 * HEADS:
        return "none"
    if slab < 4 * HEADS:
        return "silu"
    return "gelu"


def _inproj_kernel(x_ref, nw_ref, w_ref, cos_ref, sin_ref, z_ref, h_ref):
    j = pl.program_id(1)
    n_slabs = z_ref.shape[0]

    @pl.when(j == 0)
    def _():
        h_ref[...] = _rms(x_ref[...], nw_ref[...]).astype(BF16)

    def column_tile(jj):
        res = jnp.dot(h_ref[...], w_ref[...], preferred_element_type=F32)
        for s in range(n_slabs):
            y = res[:, s * LANES:(s + 1) * LANES]
            kind = _slab_epilogue(jj * n_slabs + s)
            if kind == "rotary":
                y = y * cos_ref[...] + pltpu.roll(y, HEAD_DIM // 2, 1) * sin_ref[...]
            elif kind == "silu":
                y = _silu(y)
            elif kind == "gelu":
                y = _gelu_tanh(y)
            z_ref[s] = y.astype(z_ref.dtype)

    for jj in range(IN_SLABS // n_slabs):
        pl.when(j == jj)(functools.partial(column_tile, jj))


def _in_proj(x, norm_w, w_bf16, cos, sin, *, seq, tm=1024, tn=IN_PROJ_TILE):
    t = x.shape[0]
    tiles_per_seq = seq // tm
    return pl.pallas_call(
        _inproj_kernel,
        grid=(t // tm, IN_COLS // tn),
        in_specs=[
            pl.BlockSpec((tm, D_MODEL), lambda i, j: (i, 0)),
            pl.BlockSpec((1, D_MODEL), lambda i, j: (0, 0)),
            pl.BlockSpec((D_MODEL, tn), lambda i, j: (0, j)),
            pl.BlockSpec((tm, LANES), lambda i, j: (i % tiles_per_seq, 0)),
            pl.BlockSpec((tm, LANES), lambda i, j: (i % tiles_per_seq, 0)),
        ],
        out_specs=pl.BlockSpec((tn // LANES, tm, LANES), lambda i, j: (j, i, 0)),
        out_shape=jax.ShapeDtypeStruct((IN_SLABS, t, LANES), BF16),
        scratch_shapes=[pltpu.VMEM((tm, D_MODEL), BF16)],
        compiler_params=pltpu.CompilerParams(
            dimension_semantics=("parallel", "arbitrary"),
            vmem_limit_bytes=VMEM_LIMIT),
        name="in_proj",
    )(x, norm_w.reshape(1, D_MODEL), w_bf16, cos, sin)


def _ret_kernel(q_ref, k_ref, v_ref, g_ref, lgf_ref, lgb_ref, gnw_ref,
                o_ref, qx_ref, kt_ref, kvf_ref, rb_ref):
    seq = q_ref.shape[1]
    n_chunks = seq // CHUNK
    lgf = lgf_ref[0, 0:1, :]
    lgb = lgb_ref[0, 0:1, :]
    row = lax.broadcasted_iota(jnp.int32, (CHUNK, CHUNK), 0).astype(F32)
    col = lax.broadcasted_iota(jnp.int32, (CHUNK, CHUNK), 1).astype(F32)
    diff = row - col
    scale = HEAD_DIM ** -0.5
    dcomb = jnp.where(diff >= 0, jnp.exp(lgf * diff), jnp.exp(lgb * (-diff))) * scale
    xi_f = jnp.exp(lgf * (row + 1.0)) * scale
    xi_b = jnp.exp(lgb * (CHUNK - row)) * scale
    zeta_f = jnp.exp(lgf * (CHUNK - 1.0 - row))
    zeta_b = jnp.exp(lgb * row)
    decay_f = jnp.exp(lgf * CHUNK)
    decay_b = jnp.exp(lgb * CHUNK)

    def rows(c):
        return pl.ds(pl.multiple_of(c * CHUNK, CHUNK), CHUNK)

    def prep(i, state):
        c = n_chunks - 1 - i
        sl = rows(c)
        q = q_ref[0, sl, :].astype(F32)
        k = k_ref[0, sl, :].astype(F32)
        kt_ref[c] = k_ref[0, sl, :].T
        qx_ref[sl, :] = jnp.concatenate([q * xi_f, q * xi_b], axis=1).astype(BF16)
        kz = jnp.concatenate([k * zeta_f, k * zeta_b], axis=1).astype(BF16)
        kv = lax.dot_general(kz, v_ref[0, sl, :], (((0,), (0,)), ((), ())),
                             preferred_element_type=F32)
        kvf_ref[c] = kv[:HEAD_DIM]
        rb_ref[c] = state.astype(BF16)
        return state * decay_b + kv[HEAD_DIM:]

    lax.fori_loop(0, n_chunks, prep, jnp.zeros((HEAD_DIM, HEAD_DIM), F32), unroll=RET_UNROLL)

    gnw = gnw_ref[0, 0:1, :]

    def fwd(c, state):
        sl = rows(c)
        s = jnp.dot(q_ref[0, sl, :], kt_ref[c], preferred_element_type=F32) * dcomb
        lhs = jnp.concatenate([s.astype(BF16), qx_ref[sl, :]], axis=1)
        rhs = jnp.concatenate([v_ref[0, sl, :], state.astype(BF16), rb_ref[c]], axis=0)
        o = jnp.dot(lhs, rhs, preferred_element_type=F32)
        mu = jnp.mean(o, axis=-1, keepdims=True)
        d = o - mu
        var = jnp.mean(d * d, axis=-1, keepdims=True)
        on = d * lax.rsqrt(var + EPS) * gnw
        o_ref[0, sl, :] = (g_ref[0, sl, :].astype(F32) * on).astype(o_ref.dtype)
        return state * decay_f + kvf_ref[c]

    lax.fori_loop(0, n_chunks, fwd, jnp.zeros((HEAD_DIM, HEAD_DIM), F32), unroll=RET_UNROLL)


def _retention(z, lgf, lgb, gnw, *, batch, seq):
    t = batch * seq
    n_chunks = seq // CHUNK

    def zspec(base):
        return pl.BlockSpec((1, seq, LANES), lambda b, h: (base + h, b, 0))

    per_head = pl.BlockSpec((1, SUBLANES, LANES), lambda b, h: (h, 0, 0))
    return pl.pallas_call(
        _ret_kernel,
        grid=(batch, HEADS),
        in_specs=[zspec(0), zspec(HEADS), zspec(2 * HEADS), zspec(3 * HEADS),
                  per_head, per_head, per_head],
        out_specs=pl.BlockSpec((1, seq, LANES), lambda b, h: (h, b, 0)),
        out_shape=jax.ShapeDtypeStruct((HEADS, t, LANES), BF16),
        scratch_shapes=[
            pltpu.VMEM((seq, 2 * HEAD_DIM), BF16),
            pltpu.VMEM((n_chunks, HEAD_DIM, CHUNK), BF16),
            pltpu.VMEM((n_chunks, HEAD_DIM, HEAD_DIM), F32),
            pltpu.VMEM((n_chunks, HEAD_DIM, HEAD_DIM), BF16),
        ],
        compiler_params=pltpu.CompilerParams(
            dimension_semantics=("parallel", "parallel"),
            vmem_limit_bytes=VMEM_LIMIT),
        name="retention",
    )(z, z, z, z, lgf, lgb, gnw)


def _sgu_norm(zv_ref, nw_ref):
    v = zv_ref[...].astype(F32)
    ms = jnp.sum(jnp.sum(v * v, axis=0), axis=-1, keepdims=True) * (1.0 / SGU_WIDTH)
    inv = lax.rsqrt(ms + EPS)
    return [(v[g] * inv * nw_ref[g, 0:1, :]).astype(BF16) for g in range(GROUPS)]


def _sgu_mix(zu_ref, vn, ws_ref, b_ref):
    n_chunks = zu_ref.shape[1] // CHUNK
    groups = []
    for g in range(GROUPS):
        rhs = jnp.concatenate(
            [vn[g][c * CHUNK:(c + 1) * CHUNK] for c in range(n_chunks)], axis=1)
        sp = jnp.dot(ws_ref[g], rhs, preferred_element_type=F32)
        bias = b_ref[g]
        groups.append(jnp.concatenate(
            [zu_ref[g, c * CHUNK:(c + 1) * CHUNK, :].astype(F32)
             * (sp[:, c * CHUNK:(c + 1) * CHUNK] + bias)
             for c in range(n_chunks)], axis=0).astype(BF16))
    return jnp.concatenate(groups, axis=1)


def _outproj_kernel(ret_ref, zu_ref, zv_ref, sgu_nw_ref, ws_ref, sb_ref, x_ref, w_ref, nw_ref,
                    x2_ref, h2_ref):
    vn = _sgu_norm(zv_ref, sgu_nw_ref)
    ret = jnp.concatenate([ret_ref[s] for s in range(HEADS)], axis=1)
    x2_ref[...] = x_ref[...] + jnp.dot(ret, w_ref[0:RET_WIDTH, :], preferred_element_type=F32)
    sgu = _sgu_mix(zu_ref, vn, ws_ref, sb_ref)
    x2 = x2_ref[...] + jnp.dot(sgu, w_ref[RET_WIDTH:, :], preferred_element_type=F32)
    x2_ref[...] = x2
    h2_ref[...] = _rms(x2, nw_ref[...]).astype(BF16)


def _out_proj(ret, z, sgu_nw, ws_bf16, sgu_bias, x, w_bf16, norm_w, *, tm=512):
    t = x.shape[0]
    u_base = 4 * HEADS // GROUPS
    const3 = lambda i: (0, 0, 0)
    return pl.pallas_call(
        _outproj_kernel,
        grid=(t // tm,),
        in_specs=[
            pl.BlockSpec((HEADS, tm, LANES), lambda i: (0, i, 0)),
            pl.BlockSpec((GROUPS, tm, LANES), lambda i: (u_base, i, 0)),
            pl.BlockSpec((GROUPS, tm, LANES), lambda i: (u_base + 1, i, 0)),
            pl.BlockSpec((GROUPS, SUBLANES, LANES), const3),
            pl.BlockSpec((GROUPS, CHUNK, CHUNK), const3),
            pl.BlockSpec((GROUPS, CHUNK, LANES), const3),
            pl.BlockSpec((tm, D_MODEL), lambda i: (i, 0)),
            pl.BlockSpec((D_MODEL, D_MODEL), lambda i: (0, 0)),
            pl.BlockSpec((1, D_MODEL), lambda i: (0, 0)),
        ],
        out_specs=[pl.BlockSpec((tm, D_MODEL), lambda i: (i, 0)),
                   pl.BlockSpec((tm, D_MODEL), lambda i: (i, 0))],
        out_shape=[jax.ShapeDtypeStruct((t, D_MODEL), F32),
                   jax.ShapeDtypeStruct((t, D_MODEL), BF16)],
        compiler_params=pltpu.CompilerParams(
            dimension_semantics=("parallel",),
            vmem_limit_bytes=VMEM_LIMIT),
        name="out_proj",
    )(ret, z, z, sgu_nw, ws_bf16, sgu_bias, x, w_bf16, norm_w.reshape(1, D_MODEL))


def _ffn_up_kernel(hp_ref, h_ref, hn_ref, wa_ref, wb_ref, cwa_ref, cwb_ref, cba_ref,
                   cbb_ref, o_ref, lhs_ref, *, tiles_per_seq):
    i = pl.program_id(0)
    j = pl.program_id(1)
    tm = h_ref.shape[0]
    halo = hp_ref.shape[0]

    @pl.when(j == 0)
    def _():
        pos = i % tiles_per_seq
        hp = hp_ref[...]
        hn = hn_ref[...]
        lhs_ref[0:halo, :] = jnp.where(pos == 0, jnp.zeros_like(hp), hp)
        lhs_ref[halo:halo + tm, :] = h_ref[...]
        lhs_ref[halo + tm:, :] = jnp.where(pos == tiles_per_seq - 1, jnp.zeros_like(hn), hn)

    lhs = lhs_ref[...]
    n_rows = tm + 2 * halo

    def conv_branch(w_ref, cw_ref, cb_ref):
        u = jnp.dot(lhs, w_ref[...], preferred_element_type=F32)
        prev = pltpu.roll(u, 1, 0)[halo:halo + tm]
        nxt = pltpu.roll(u, n_rows - 1, 0)[halo:halo + tm]
        cur = u[halo:halo + tm]
        return prev * cw_ref[0:1, :] + cur * cw_ref[1:2, :] + nxt * cw_ref[2:3, :] + cb_ref[...]

    a = conv_branch(wa_ref, cwa_ref, cba_ref)
    b = conv_branch(wb_ref, cwb_ref, cbb_ref)
    o_ref[...] = (_silu(a) * b).astype(o_ref.dtype)


def _ffn_up(h2, w_up_bf16, conv_w, conv_b, *, seq, tm=FFN_ROWS, tf=FFN_TILE):
    t = h2.shape[0]
    halo = BF16_ROWS
    n_j = D_FF // tf
    halo_blocks = tm // halo
    last_halo_block = t // halo - 1
    kernel = functools.partial(_ffn_up_kernel, tiles_per_seq=seq // tm)
    return pl.pallas_call(
        kernel,
        grid=(t // tm, n_j),
        in_specs=[
            pl.BlockSpec((halo, D_MODEL), lambda i, j: (jnp.maximum(i * halo_blocks - 1, 0), 0)),
            pl.BlockSpec((tm, D_MODEL), lambda i, j: (i, 0)),
            pl.BlockSpec((halo, D_MODEL),
                         lambda i, j: (jnp.minimum((i + 1) * halo_blocks, last_halo_block), 0)),
            pl.BlockSpec((D_MODEL, tf), lambda i, j: (0, j)),
            pl.BlockSpec((D_MODEL, tf), lambda i, j: (0, n_j + j)),
            pl.BlockSpec((3, tf), lambda i, j: (0, j)),
            pl.BlockSpec((3, tf), lambda i, j: (0, n_j + j)),
            pl.BlockSpec((1, tf), lambda i, j: (0, j)),
            pl.BlockSpec((1, tf), lambda i, j: (0, n_j + j)),
        ],
        out_specs=pl.BlockSpec((None, tm, tf), lambda i, j: (j, i, 0)),
        out_shape=jax.ShapeDtypeStruct((n_j, t, tf), BF16),
        scratch_shapes=[pltpu.VMEM((tm + 2 * halo, D_MODEL), BF16)],
        compiler_params=pltpu.CompilerParams(
            dimension_semantics=("parallel", "arbitrary"),
            vmem_limit_bytes=VMEM_LIMIT),
        name="ffn_up",
    )(h2, h2, h2, w_up_bf16, w_up_bf16, conv_w, conv_w,
      conv_b.reshape(1, 2 * D_FF), conv_b.reshape(1, 2 * D_FF))


def _ffn_down_kernel(act_ref, w_ref, x2_ref, fw_ref, o_ref):
    act = jnp.concatenate([act_ref[j] for j in range(act_ref.shape[0])], axis=1)
    y = x2_ref[...] + jnp.dot(act, w_ref[...], preferred_element_type=F32)
    o_ref[...] = _rms(y, fw_ref[...])


def _ffn_down(act, w_down_bf16, x2, final_w, *, tm=512):
    n_j, t, tf = act.shape
    return pl.pallas_call(
        _ffn_down_kernel,
        grid=(t // tm,),
        in_specs=[
            pl.BlockSpec((n_j, tm, tf), lambda i: (0, i, 0)),
            pl.BlockSpec((D_FF, D_MODEL), lambda i: (0, 0), pipeline_mode=pl.Buffered(1)),
            pl.BlockSpec((tm, D_MODEL), lambda i: (i, 0)),
            pl.BlockSpec((1, D_MODEL), lambda i: (0, 0)),
        ],
        out_specs=pl.BlockSpec((tm, D_MODEL), lambda i: (i, 0)),
        out_shape=jax.ShapeDtypeStruct((t, D_MODEL), F32),
        compiler_params=pltpu.CompilerParams(
            dimension_semantics=("parallel",),
            vmem_limit_bytes=VMEM_LIMIT),
        name="ffn_down",
    )(act, w_down_bf16, x2, final_w.reshape(1, D_MODEL))


def _ffn(h2, x2, w_up_bf16, conv_w, conv_b, w_down_bf16, final_w, *, seq):
    act = _ffn_up(h2, w_up_bf16, conv_w, conv_b, seq=seq)
    return _ffn_down(act, w_down_bf16, x2, final_w)


def _rope_tables(seq):
    inv_freq = ROPE_BASE ** (-np.arange(0, HEAD_DIM, 2, dtype=np.float64) / HEAD_DIM)
    ang = np.arange(seq, dtype=np.float64)[:, None] * inv_freq[None, :]
    cos = np.cos(ang).astype(np.float32)
    sin = np.sin(ang).astype(np.float32)
    return (jnp.asarray(np.concatenate([cos, cos], axis=-1)),
            jnp.asarray(np.concatenate([-sin, sin], axis=-1)))


def _per_slab(vec, slabs):
    return jnp.broadcast_to(vec.reshape(slabs, 1, LANES), (slabs, SUBLANES, LANES))


def _per_head_scalar(vec):
    return jnp.broadcast_to(vec.astype(F32)[:, None, None], (HEADS, SUBLANES, LANES))


def _trunk(x, p):
    batch, seq, _ = x.shape
    xt = x.reshape(batch * seq, D_MODEL)
    z = _in_proj(xt, p["norm1_w"], p["w_in"], p["cos"], p["sin"], seq=seq)
    ret = _retention(z, p["lgf"], p["lgb"], p["gnw"], batch=batch, seq=seq)
    x2, h2 = _out_proj(ret, z, p["sgu_nw"], p["w_s"], p["sgu_b"], xt, p["w_out"], p["norm2_w"])
    y = _ffn(h2, x2, p["w_up"], p["conv_w"], p["conv_b"], p["w_down"], p["final_w"], seq=seq)
    return y.reshape(batch, seq, D_MODEL)


def kernel(x_prompt, x_sample, norm1_w, w_in, ret_log_decay_fwd, ret_log_decay_bwd, ret_gn_w, sgu_norm_w, sgu_w_s, sgu_b, w_out, norm2_w, w_up, conv_w, conv_b, w_down, final_norm_w):
    max_seq = max(x_prompt.shape[1], x_sample.shape[1])
    cos, sin = _rope_tables(max_seq)
    p = {
        "norm1_w": norm1_w[0],
        "w_in": w_in[0].astype(BF16),
        "cos": cos,
        "sin": sin,
        "lgf": _per_head_scalar(ret_log_decay_fwd[0]),
        "lgb": _per_head_scalar(ret_log_decay_bwd[0]),
        "gnw": _per_slab(ret_gn_w[0], HEADS),
        "sgu_nw": _per_slab(sgu_norm_w[0], GROUPS),
        "w_s": sgu_w_s[0].astype(BF16),
        "sgu_b": jnp.broadcast_to(sgu_b[0][:, :, None], (GROUPS, CHUNK, LANES)),
        "w_out": w_out[0].astype(BF16),
        "norm2_w": norm2_w[0],
        "w_up": w_up[0].astype(BF16),
        "conv_w": conv_w[0],
        "conv_b": conv_b[0],
        "w_down": w_down[0].astype(BF16),
        "final_w": final_norm_w,
    }
    return (_trunk(x_prompt, p), _trunk(x_sample, p))
```

```python
import functools
import math

import jax
import jax.numpy as jnp
import numpy as np
from jax import lax
from jax.experimental import pallas as pl
from jax.experimental.pallas import tpu as pltpu

D_MODEL = 2048
CHUNK = 128
HEADS = 8
HEAD_DIM = 128
RET_WIDTH = HEADS * HEAD_DIM
GROUPS = 8
SGU_WIDTH = GROUPS * CHUNK
IN_COLS = 4 * RET_WIDTH + 2 * SGU_WIDTH
D_FF = 5632
ROPE_BASE = 10000.0
EPS = 1e-6
LANES = 128
SUBLANES = 8
BF16_ROWS = 2 * SUBLANES
IN_SLABS = IN_COLS // LANES
IN_PROJ_TILE = 2048
FFN_TILE = 512
FFN_ROWS = 1024
RET_UNROLL = 32
VMEM_LIMIT = 56 * 1024 * 1024

F32 = jnp.float32
BF16 = jnp.bfloat16


def _rms(x, w):
    ms = jnp.mean(x * x, axis=-1, keepdims=True)
    return x * lax.rsqrt(ms + EPS) * w


def _gelu_tanh(x):
    c = math.sqrt(2.0 / math.pi)
    h = 0.5 * x
    return h + h * jnp.tanh(x * (c + (c * 0.044715) * (x * x)))


def _silu(x):
    h = 0.5 * x
    return h + h * jnp.tanh(h)


def _slab_spec(w, n_steps, step_of):
    rows = w.shape[0] // n_steps
    assert rows * n_steps == w.shape[0] and rows % BF16_ROWS == 0, (w.shape, n_steps)
    return pl.BlockSpec((rows, w.shape[1]), lambda *ids: (step_of(*ids), 0))


def _cast_slabs(src_refs, dst_refs):
    for src, dst in zip(src_refs, dst_refs):
        dst[...] = src[...].astype(dst.dtype)


def _slab_epilogue(slab):
    if slab < 2 * HEADS:
        return "rotary"
    if slab < 3 * HEADS:
        return "none"
    if slab < 4 * HEADS:
        return "silu"
    return "gelu"


def _inproj_kernel(x_ref, nw_ref, w_ref, cos_ref, sin_ref, z_ref, h_ref):
    j = pl.program_id(1)
    n_slabs = z_ref.shape[0]

    @pl.when(j == 0)
    def _():
        h_ref[...] = _rms(x_ref[...], nw_ref[...]).astype(BF16)

    def column_tile(jj):
        res = jnp.dot(h_ref[...], w_ref[...], preferred_element_type=F32)
        for s in range(n_slabs):
            y = res[:, s * LANES:(s + 1) * LANES]
            kind = _slab_epilogue(jj * n_slabs + s)
            if kind == "rotary":
                y = y * cos_ref[...] + pltpu.roll(y, HEAD_DIM // 2, 1) * sin_ref[...]
            elif kind == "silu":
                y = _silu(y)
            elif kind == "gelu":
                y = _gelu_tanh(y)
            z_ref[s] = y.astype(z_ref.dtype)

    for jj in range(IN_SLABS // n_slabs):
        pl.when(j == jj)(functools.partial(column_tile, jj))


def _in_proj(x, norm_w, w_bf16, cos, sin, *, seq, tm=1024, tn=IN_PROJ_TILE):
    t = x.shape[0]
    tiles_per_seq = seq // tm
    return pl.pallas_call(
        _inproj_kernel,
        grid=(t // tm, IN_COLS // tn),
        in_specs=[
            pl.BlockSpec((tm, D_MODEL), lambda i, j: (i, 0)),
            pl.BlockSpec((1, D_MODEL), lambda i, j: (0, 0)),
            pl.BlockSpec((D_MODEL, tn), lambda i, j: (0, j)),
            pl.BlockSpec((tm, LANES), lambda i, j: (i % tiles_per_seq, 0)),
            pl.BlockSpec((tm, LANES), lambda i, j: (i % tiles_per_seq, 0)),
        ],
        out_specs=pl.BlockSpec((tn // LANES, tm, LANES), lambda i, j: (j, i, 0)),
        out_shape=jax.ShapeDtypeStruct((IN_SLABS, t, LANES), BF16),
        scratch_shapes=[pltpu.VMEM((tm, D_MODEL), BF16)],
        compiler_params=pltpu.CompilerParams(
            dimension_semantics=("parallel", "arbitrary"),
            vmem_limit_bytes=VMEM_LIMIT),
        name="in_proj",
    )(x, norm_w.reshape(1, D_MODEL), w_bf16, cos, sin)


def _ret_kernel(*refs, n_cast):
    q_ref, k_ref, v_ref, g_ref, lgf_ref, lgb_ref, gnw_ref = refs[:7]
    o_ref = refs[7 + n_cast]
    qx_ref, kt_ref, kvf_ref, rb_ref = refs[8 + 2 * n_cast:]
    _cast_slabs(refs[7:7 + n_cast], refs[8 + n_cast:8 + 2 * n_cast])
    seq = q_ref.shape[1]
    n_chunks = seq // CHUNK
    lgf = lgf_ref[0, 0:1, :]
    lgb = lgb_ref[0, 0:1, :]
    row = lax.broadcasted_iota(jnp.int32, (CHUNK, CHUNK), 0).astype(F32)
    col = lax.broadcasted_iota(jnp.int32, (CHUNK, CHUNK), 1).astype(F32)
    diff = row - col
    scale = HEAD_DIM ** -0.5
    dcomb = jnp.where(diff >= 0, jnp.exp(lgf * diff), jnp.exp(lgb * (-diff))) * scale
    xi_f = jnp.exp(lgf * (row + 1.0)) * scale
    xi_b = jnp.exp(lgb * (CHUNK - row)) * scale
    zeta_f = jnp.exp(lgf * (CHUNK - 1.0 - row))
    zeta_b = jnp.exp(lgb * row)
    decay_f = jnp.exp(lgf * CHUNK)
    decay_b = jnp.exp(lgb * CHUNK)

    def rows(c):
        return pl.ds(pl.multiple_of(c * CHUNK, CHUNK), CHUNK)

    def prep(i, state):
        c = n_chunks - 1 - i
        sl = rows(c)
        q = q_ref[0, sl, :].astype(F32)
        k = k_ref[0, sl, :].astype(F32)
        kt_ref[c] = k_ref[0, sl, :].T
        qx_ref[sl, :] = jnp.concatenate([q * xi_f, q * xi_b], axis=1).astype(BF16)
        kz = jnp.concatenate([k * zeta_f, k * zeta_b], axis=1).astype(BF16)
        kv = lax.dot_general(kz, v_ref[0, sl, :], (((0,), (0,)), ((), ())),
                             preferred_element_type=F32)
        kvf_ref[c] = kv[:HEAD_DIM]
        rb_ref[c] = state.astype(BF16)
        return state * decay_b + kv[HEAD_DIM:]

    lax.fori_loop(0, n_chunks, prep, jnp.zeros((HEAD_DIM, HEAD_DIM), F32), unroll=RET_UNROLL)

    gnw = gnw_ref[0, 0:1, :]

    def fwd(c, state):
        sl = rows(c)
        s = jnp.dot(q_ref[0, sl, :], kt_ref[c], preferred_element_type=F32) * dcomb
        lhs = jnp.concatenate([s.astype(BF16), qx_ref[sl, :]], axis=1)
        rhs = jnp.concatenate([v_ref[0, sl, :], state.astype(BF16), rb_ref[c]], axis=0)
        o = jnp.dot(lhs, rhs, preferred_element_type=F32)
        mu = jnp.mean(o, axis=-1, keepdims=True)
        d = o - mu
        var = jnp.mean(d * d, axis=-1, keepdims=True)
        on = d * lax.rsqrt(var + EPS) * gnw
        o_ref[0, sl, :] = (g_ref[0, sl, :].astype(F32) * on).astype(o_ref.dtype)
        return state * decay_f + kvf_ref[c]

    lax.fori_loop(0, n_chunks, fwd, jnp.zeros((HEAD_DIM, HEAD_DIM), F32), unroll=RET_UNROLL)


def _retention(z, lgf, lgb, gnw, *, batch, seq, cast=()):
    t = batch * seq
    n_chunks = seq // CHUNK

    def zspec(base):
        return pl.BlockSpec((1, seq, LANES), lambda b, h: (base + h, b, 0))

    per_head = pl.BlockSpec((1, SUBLANES, LANES), lambda b, h: (h, 0, 0))
    slabs = [_slab_spec(w, batch * HEADS, lambda b, h: b * HEADS + h) for w in cast]
    out = pl.pallas_call(
        functools.partial(_ret_kernel, n_cast=len(cast)),
        grid=(batch, HEADS),
        in_specs=[zspec(0), zspec(HEADS), zspec(2 * HEADS), zspec(3 * HEADS),
                  per_head, per_head, per_head] + slabs,
        out_specs=[pl.BlockSpec((1, seq, LANES), lambda b, h: (h, b, 0))] + slabs,
        out_shape=[jax.ShapeDtypeStruct((HEADS, t, LANES), BF16)]
        + [jax.ShapeDtypeStruct(w.shape, BF16) for w in cast],
        scratch_shapes=[
            pltpu.VMEM((seq, 2 * HEAD_DIM), BF16),
            pltpu.VMEM((n_chunks, HEAD_DIM, CHUNK), BF16),
            pltpu.VMEM((n_chunks, HEAD_DIM, HEAD_DIM), F32),
            pltpu.VMEM((n_chunks, HEAD_DIM, HEAD_DIM), BF16),
        ],
        compiler_params=pltpu.CompilerParams(
            dimension_semantics=("parallel", "parallel"),
            vmem_limit_bytes=VMEM_LIMIT),
        name="retention",
    )(z, z, z, z, lgf, lgb, gnw, *cast)
    return out[0], out[1:]


def _sgu_norm(zv_ref, nw_ref):
    v = zv_ref[...].astype(F32)
    ms = jnp.sum(jnp.sum(v * v, axis=0), axis=-1, keepdims=True) * (1.0 / SGU_WIDTH)
    inv = lax.rsqrt(ms + EPS)
    return [(v[g] * inv * nw_ref[g, 0:1, :]).astype(BF16) for g in range(GROUPS)]


def _sgu_mix(zu_ref, vn, ws_ref, b_ref):
    n_chunks = zu_ref.shape[1] // CHUNK
    groups = []
    for g in range(GROUPS):
        rhs = jnp.concatenate(
            [vn[g][c * CHUNK:(c + 1) * CHUNK] for c in range(n_chunks)], axis=1)
        sp = jnp.dot(ws_ref[g], rhs, preferred_element_type=F32)
        bias = b_ref[g]
        groups.append(jnp.concatenate(
            [zu_ref[g, c * CHUNK:(c + 1) * CHUNK, :].astype(F32)
             * (sp[:, c * CHUNK:(c + 1) * CHUNK] + bias)
             for c in range(n_chunks)], axis=0).astype(BF16))
    return jnp.concatenate(groups, axis=1)


def _outproj_kernel(ret_ref, zu_ref, zv_ref, sgu_nw_ref, ws_ref, sb_ref, x_ref, w_ref, nw_ref,
                    x2_ref, h2_ref):
    vn = _sgu_norm(zv_ref, sgu_nw_ref)
    ret = jnp.concatenate([ret_ref[s] for s in range(HEADS)], axis=1)
    x2_ref[...] = x_ref[...] + jnp.dot(ret, w_ref[0:RET_WIDTH, :], preferred_element_type=F32)
    sgu = _sgu_mix(zu_ref, vn, ws_ref, sb_ref)
    x2 = x2_ref[...] + jnp.dot(sgu, w_ref[RET_WIDTH:, :], preferred_element_type=F32)
    x2_ref[...] = x2
    h2_ref[...] = _rms(x2, nw_ref[...]).astype(BF16)


def _out_proj(ret, z, sgu_nw, ws_bf16, sgu_bias, x, w_bf16, norm_w, *, tm=512):
    t = x.shape[0]
    u_base = 4 * HEADS // GROUPS
    const3 = lambda i: (0, 0, 0)
    return pl.pallas_call(
        _outproj_kernel,
        grid=(t // tm,),
        in_specs=[
            pl.BlockSpec((HEADS, tm, LANES), lambda i: (0, i, 0)),
            pl.BlockSpec((GROUPS, tm, LANES), lambda i: (u_base, i, 0)),
            pl.BlockSpec((GROUPS, tm, LANES), lambda i: (u_base + 1, i, 0)),
            pl.BlockSpec((GROUPS, SUBLANES, LANES), const3),
            pl.BlockSpec((GROUPS, CHUNK, CHUNK), const3),
            pl.BlockSpec((GROUPS, CHUNK, LANES), const3),
            pl.BlockSpec((tm, D_MODEL), lambda i: (i, 0)),
            pl.BlockSpec((D_MODEL, D_MODEL), lambda i: (0, 0)),
            pl.BlockSpec((1, D_MODEL), lambda i: (0, 0)),
        ],
        out_specs=[pl.BlockSpec((tm, D_MODEL), lambda i: (i, 0)),
                   pl.BlockSpec((tm, D_MODEL), lambda i: (i, 0))],
        out_shape=[jax.ShapeDtypeStruct((t, D_MODEL), F32),
                   jax.ShapeDtypeStruct((t, D_MODEL), BF16)],
        compiler_params=pltpu.CompilerParams(
            dimension_semantics=("parallel",),
            vmem_limit_bytes=VMEM_LIMIT),
        name="out_proj",
    )(ret, z, z, sgu_nw, ws_bf16, sgu_bias, x, w_bf16, norm_w.reshape(1, D_MODEL))


def _ffn_up_kernel(*refs, tiles_per_seq, n_cast):
    hp_ref, h_ref, hn_ref, wa_ref, wb_ref, cwa_ref, cwb_ref, cba_ref, cbb_ref = refs[:9]
    o_ref = refs[9 + n_cast]
    lhs_ref = refs[10 + 2 * n_cast]
    _cast_slabs(refs[9:9 + n_cast], refs[10 + n_cast:10 + 2 * n_cast])
    i = pl.program_id(0)
    j = pl.program_id(1)
    tm = h_ref.shape[0]
    halo = hp_ref.shape[0]

    @pl.when(j == 0)
    def _():
        pos = i % tiles_per_seq
        hp = hp_ref[...]
        hn = hn_ref[...]
        lhs_ref[0:halo, :] = jnp.where(pos == 0, jnp.zeros_like(hp), hp)
        lhs_ref[halo:halo + tm, :] = h_ref[...]
        lhs_ref[halo + tm:, :] = jnp.where(pos == tiles_per_seq - 1, jnp.zeros_like(hn), hn)

    lhs = lhs_ref[...]
    n_rows = tm + 2 * halo

    def conv_branch(w_ref, cw_ref, cb_ref):
        u = jnp.dot(lhs, w_ref[...], preferred_element_type=F32)
        prev = pltpu.roll(u, 1, 0)[halo:halo + tm]
        nxt = pltpu.roll(u, n_rows - 1, 0)[halo:halo + tm]
        cur = u[halo:halo + tm]
        return prev * cw_ref[0:1, :] + cur * cw_ref[1:2, :] + nxt * cw_ref[2:3, :] + cb_ref[...]

    a = conv_branch(wa_ref, cwa_ref, cba_ref)
    b = conv_branch(wb_ref, cwb_ref, cbb_ref)
    o_ref[...] = (_silu(a) * b).astype(o_ref.dtype)


def _ffn_up(h2, w_up_bf16, conv_w, conv_b, *, seq, tm=FFN_ROWS, tf=FFN_TILE, cast=()):
    t = h2.shape[0]
    halo = BF16_ROWS
    n_j = D_FF // tf
    halo_blocks = tm // halo
    last_halo_block = t // halo - 1
    kernel = functools.partial(_ffn_up_kernel, tiles_per_seq=seq // tm, n_cast=len(cast))
    slabs = [_slab_spec(w, (t // tm) * n_j, lambda i, j: i * n_j + j) for w in cast]
    out = pl.pallas_call(
        kernel,
        grid=(t // tm, n_j),
        in_specs=[
            pl.BlockSpec((halo, D_MODEL), lambda i, j: (jnp.maximum(i * halo_blocks - 1, 0), 0)),
            pl.BlockSpec((tm, D_MODEL), lambda i, j: (i, 0)),
            pl.BlockSpec((halo, D_MODEL),
                         lambda i, j: (jnp.minimum((i + 1) * halo_blocks, last_halo_block), 0)),
            pl.BlockSpec((D_MODEL, tf), lambda i, j: (0, j)),
            pl.BlockSpec((D_MODEL, tf), lambda i, j: (0, n_j + j)),
            pl.BlockSpec((3, tf), lambda i, j: (0, j)),
            pl.BlockSpec((3, tf), lambda i, j: (0, n_j + j)),
            pl.BlockSpec((1, tf), lambda i, j: (0, j)),
            pl.BlockSpec((1, tf), lambda i, j: (0, n_j + j)),
        ] + slabs,
        out_specs=[pl.BlockSpec((None, tm, tf), lambda i, j: (j, i, 0))] + slabs,
        out_shape=[jax.ShapeDtypeStruct((n_j, t, tf), BF16)]
        + [jax.ShapeDtypeStruct(w.shape, BF16) for w in cast],
        scratch_shapes=[pltpu.VMEM((tm + 2 * halo, D_MODEL), BF16)],
        compiler_params=pltpu.CompilerParams(
            dimension_semantics=("parallel", "arbitrary"),
            vmem_limit_bytes=VMEM_LIMIT),
        name="ffn_up",
    )(h2, h2, h2, w_up_bf16, w_up_bf16, conv_w, conv_w,
      conv_b.reshape(1, 2 * D_FF), conv_b.reshape(1, 2 * D_FF), *cast)
    return out[0], out[1:]


def _ffn_down_kernel(act_ref, w_ref, x2_ref, fw_ref, o_ref):
    act = jnp.concatenate([act_ref[j] for j in range(act_ref.shape[0])], axis=1)
    y = x2_ref[...] + jnp.dot(act, w_ref[...], preferred_element_type=F32)
    o_ref[...] = _rms(y, fw_ref[...])


def _ffn_down(act, w_down_bf16, x2, final_w, *, tm=512):
    n_j, t, tf = act.shape
    return pl.pallas_call(
        _ffn_down_kernel,
        grid=(t // tm,),
        in_specs=[
            pl.BlockSpec((n_j, tm, tf), lambda i: (0, i, 0)),
            pl.BlockSpec((D_FF, D_MODEL), lambda i: (0, 0), pipeline_mode=pl.Buffered(1)),
            pl.BlockSpec((tm, D_MODEL), lambda i: (i, 0)),
            pl.BlockSpec((1, D_MODEL), lambda i: (0, 0)),
        ],
        out_specs=pl.BlockSpec((tm, D_MODEL), lambda i: (i, 0)),
        out_shape=jax.ShapeDtypeStruct((t, D_MODEL), F32),
        compiler_params=pltpu.CompilerParams(
            dimension_semantics=("parallel",),
            vmem_limit_bytes=VMEM_LIMIT),
        name="ffn_down",
    )(act, w_down_bf16, x2, final_w.reshape(1, D_MODEL))


def _rope_tables(seq):
    inv_freq = ROPE_BASE ** (-np.arange(0, HEAD_DIM, 2, dtype=np.float64) / HEAD_DIM)
    ang = np.arange(seq, dtype=np.float64)[:, None] * inv_freq[None, :]
    cos = np.cos(ang).astype(np.float32)
    sin = np.sin(ang).astype(np.float32)
    return (jnp.asarray(np.concatenate([cos, cos], axis=-1)),
            jnp.asarray(np.concatenate([-sin, sin], axis=-1)))


def _per_slab(vec, slabs):
    return jnp.broadcast_to(vec.reshape(slabs, 1, LANES), (slabs, SUBLANES, LANES))


def _per_head_scalar(vec):
    return jnp.broadcast_to(vec.astype(F32)[:, None, None], (HEADS, SUBLANES, LANES))


def _trunk(x, p):
    batch, seq, _ = x.shape
    xt = x.reshape(batch * seq, D_MODEL)
    z = _in_proj(xt, p["norm1_w"], p["w_in"], p["cos"], p["sin"], seq=seq)
    if "w_out" in p:
        ret, _ = _retention(z, p["lgf"], p["lgb"], p["gnw"], batch=batch, seq=seq)
    else:
        ret, (w_out, w_up) = _retention(z, p["lgf"], p["lgb"], p["gnw"], batch=batch, seq=seq,
                                        cast=(p["w_out_f32"], p["w_up_f32"]))
        p = dict(p, w_out=w_out, w_up=w_up)
    x2, h2 = _out_proj(ret, z, p["sgu_nw"], p["w_s"], p["sgu_b"], xt, p["w_out"], p["norm2_w"])
    if "w_down" in p:
        act, _ = _ffn_up(h2, p["w_up"], p["conv_w"], p["conv_b"], seq=seq)
    else:
        act, (w_down,) = _ffn_up(h2, p["w_up"], p["conv_w"], p["conv_b"], seq=seq,
                                 cast=(p["w_down_f32"],))
        p = dict(p, w_down=w_down)
    y = _ffn_down(act, p["w_down"], x2, p["final_w"])
    return y.reshape(batch, seq, D_MODEL), p


def kernel(x_prompt, x_sample, norm1_w, w_in, ret_log_decay_fwd, ret_log_decay_bwd, ret_gn_w, sgu_norm_w, sgu_w_s, sgu_b, w_out, norm2_w, w_up, conv_w, conv_b, w_down, final_norm_w):
    max_seq = max(x_prompt.shape[1], x_sample.shape[1])
    cos, sin = _rope_tables(max_seq)
    p = {
        "norm1_w": norm1_w[0],
        "w_in": w_in[0].astype(BF16),
        "cos": cos,
        "sin": sin,
        "lgf": _per_head_scalar(ret_log_decay_fwd[0]),
        "lgb": _per_head_scalar(ret_log_decay_bwd[0]),
        "gnw": _per_slab(ret_gn_w[0], HEADS),
        "sgu_nw": _per_slab(sgu_norm_w[0], GROUPS),
        "w_s": sgu_w_s[0].astype(BF16),
        "sgu_b": jnp.broadcast_to(sgu_b[0][:, :, None], (GROUPS, CHUNK, LANES)),
        "w_out_f32": w_out[0],
        "norm2_w": norm2_w[0],
        "w_up_f32": w_up[0],
        "conv_w": conv_w[0],
        "conv_b": conv_b[0],
        "w_down_f32": w_down[0],
        "final_w": final_norm_w,
    }
    y_prompt, p = _trunk(x_prompt, p)
    y_sample, _ = _trunk(x_sample, p)
    return (y_prompt, y_sample)
```

```python
import functools
import math

import jax
import jax.numpy as jnp
import numpy as np
from jax import lax
from jax.experimental import pallas as pl
from jax.experimental.pallas import tpu as pltpu

D_MODEL = 2048
CHUNK = 128
HEADS = 8
HEAD_DIM = 128
RET_WIDTH = HEADS * HEAD_DIM
GROUPS = 8
SGU_WIDTH = GROUPS * CHUNK
IN_COLS = 4 * RET_WIDTH + 2 * SGU_WIDTH
D_FF = 5632
ROPE_BASE = 10000.0
EPS = 1e-6
LANES = 128
SUBLANES = 8
BF16_ROWS = 2 * SUBLANES
IN_SLABS = IN_COLS // LANES
IN_PROJ_TILE = 2048
FFN_TILE = 512
FFN_ROWS = 1024
RET_UNROLL = 32
VMEM_LIMIT = 56 * 1024 * 1024

F32 = jnp.float32
BF16 = jnp.bfloat16


def _rms(x, w):
    ms = jnp.mean(x * x, axis=-1, keepdims=True)
    return x * lax.rsqrt(ms + EPS) * w


def _gelu_tanh(x):
    c = math.sqrt(2.0 / math.pi)
    h = 0.5 * x
    return h + h * jnp.tanh(x * (c + (c * 0.044715) * (x * x)))


def _silu(x):
    h = 0.5 * x
    return h + h * jnp.tanh(h)


def _slab_spec(w, n_steps, step_of):
    rows = w.shape[0] // n_steps
    assert rows * n_steps == w.shape[0] and rows % BF16_ROWS == 0, (w.shape, n_steps)
    return pl.BlockSpec((rows, w.shape[1]), lambda *ids: (step_of(*ids), 0))


def _cast_slabs(src_refs, dst_refs):
    for src, dst in zip(src_refs, dst_refs):
        dst[...] = src[...].astype(dst.dtype)


def _slab_epilogue(slab):
    if slab < 2 * HEADS:
        return "rotary"
    if slab < 3 * HEADS:
        return "none"
    if slab < 4 * HEADS:
        return "silu"
    return "gelu"


def _inproj_kernel(x_ref, nw_ref, w_ref, cos_ref, sin_ref, z_ref, h_ref):
    j = pl.program_id(1)
    n_slabs = z_ref.shape[0]

    @pl.when(j == 0)
    def _():
        h_ref[...] = _rms(x_ref[...], nw_ref[...]).astype(BF16)

    def column_tile(jj):
        res = jnp.dot(h_ref[...], w_ref[...], preferred_element_type=F32)
        for s in range(n_slabs):
            y = res[:, s * LANES:(s + 1) * LANES]
            kind = _slab_epilogue(jj * n_slabs + s)
            if kind == "rotary":
                y = y * cos_ref[...] + pltpu.roll(y, HEAD_DIM // 2, 1) * sin_ref[...]
            elif kind == "silu":
                y = _silu(y)
            elif kind == "gelu":
                y = _gelu_tanh(y)
            z_ref[s] = y.astype(z_ref.dtype)

    for jj in range(IN_SLABS // n_slabs):
        pl.when(j == jj)(functools.partial(column_tile, jj))


def _in_proj(x, norm_w, w_bf16, cos, sin, *, seq, tm=1024, tn=IN_PROJ_TILE):
    t = x.shape[0]
    tiles_per_seq = seq // tm
    return pl.pallas_call(
        _inproj_kernel,
        grid=(t // tm, IN_COLS // tn),
        in_specs=[
            pl.BlockSpec((tm, D_MODEL), lambda i, j: (i, 0)),
            pl.BlockSpec((1, D_MODEL), lambda i, j: (0, 0)),
            pl.BlockSpec((D_MODEL, tn), lambda i, j: (0, j)),
            pl.BlockSpec((tm, LANES), lambda i, j: (i % tiles_per_seq, 0)),
            pl.BlockSpec((tm, LANES), lambda i, j: (i % tiles_per_seq, 0)),
        ],
        out_specs=pl.BlockSpec((tn // LANES, tm, LANES), lambda i, j: (j, i, 0)),
        out_shape=jax.ShapeDtypeStruct((IN_SLABS, t, LANES), BF16),
        scratch_shapes=[pltpu.VMEM((tm, D_MODEL), BF16)],
        compiler_params=pltpu.CompilerParams(
            dimension_semantics=("parallel", "arbitrary"),
            vmem_limit_bytes=VMEM_LIMIT),
        name="in_proj",
    )(x, norm_w.reshape(1, D_MODEL), w_bf16, cos, sin)


def _ret_kernel(*refs, n_cast):
    q_ref, k_ref, v_ref, g_ref, lgf_ref, lgb_ref, gnw_ref = refs[:7]
    o_ref = refs[7 + n_cast]
    qx_ref, kt_ref, kvf_ref, rb_ref = refs[8 + 2 * n_cast:]
    _cast_slabs(refs[7:7 + n_cast], refs[8 + n_cast:8 + 2 * n_cast])
    seq = q_ref.shape[1]
    n_chunks = seq // CHUNK
    lgf = lgf_ref[0, 0:1, :]
    lgb = lgb_ref[0, 0:1, :]
    row = lax.broadcasted_iota(jnp.int32, (CHUNK, CHUNK), 0).astype(F32)
    col = lax.broadcasted_iota(jnp.int32, (CHUNK, CHUNK), 1).astype(F32)
    diff = row - col
    scale = HEAD_DIM ** -0.5
    dcomb = jnp.where(diff >= 0, jnp.exp(lgf * diff), jnp.exp(lgb * (-diff))) * scale
    xi_f = jnp.exp(lgf * (row + 1.0)) * scale
    xi_b = jnp.exp(lgb * (CHUNK - row)) * scale
    zeta_f = jnp.exp(lgf * (CHUNK - 1.0 - row))
    zeta_b = jnp.exp(lgb * row)
    decay_f = jnp.exp(lgf * CHUNK)
    decay_b = jnp.exp(lgb * CHUNK)

    def rows(c):
        return pl.ds(pl.multiple_of(c * CHUNK, CHUNK), CHUNK)

    def prep(i, state):
        c = n_chunks - 1 - i
        sl = rows(c)
        q = q_ref[0, sl, :].astype(F32)
        k = k_ref[0, sl, :].astype(F32)
        kt_ref[c] = k_ref[0, sl, :].T
        qx_ref[sl, :] = jnp.concatenate([q * xi_f, q * xi_b], axis=1).astype(BF16)
        kz = jnp.concatenate([k * zeta_f, k * zeta_b], axis=1).astype(BF16)
        kv = lax.dot_general(kz, v_ref[0, sl, :], (((0,), (0,)), ((), ())),
                             preferred_element_type=F32)
        kvf_ref[c] = kv[:HEAD_DIM]
        rb_ref[c] = state.astype(BF16)
        return state * decay_b + kv[HEAD_DIM:]

    lax.fori_loop(0, n_chunks, prep, jnp.zeros((HEAD_DIM, HEAD_DIM), F32), unroll=RET_UNROLL)

    gnw = gnw_ref[0, 0:1, :]

    def fwd(c, state):
        sl = rows(c)
        s = jnp.dot(q_ref[0, sl, :], kt_ref[c], preferred_element_type=F32) * dcomb
        lhs = jnp.concatenate([s.astype(BF16), qx_ref[sl, :]], axis=1)
        rhs = jnp.concatenate([v_ref[0, sl, :], state.astype(BF16), rb_ref[c]], axis=0)
        o = jnp.dot(lhs, rhs, preferred_element_type=F32)
        mu = jnp.mean(o, axis=-1, keepdims=True)
        d = o - mu
        var = jnp.mean(d * d, axis=-1, keepdims=True)
        on = d * lax.rsqrt(var + EPS) * gnw
        o_ref[0, sl, :] = (g_ref[0, sl, :].astype(F32) * on).astype(o_ref.dtype)
        return state * decay_f + kvf_ref[c]

    lax.fori_loop(0, n_chunks, fwd, jnp.zeros((HEAD_DIM, HEAD_DIM), F32), unroll=RET_UNROLL)


def _retention(z, lgf, lgb, gnw, *, batch, seq, cast=()):
    t = batch * seq
    n_chunks = seq // CHUNK

    def zspec(base):
        return pl.BlockSpec((1, seq, LANES), lambda b, h: (base + h, b, 0))

    per_head = pl.BlockSpec((1, SUBLANES, LANES), lambda b, h: (h, 0, 0))
    slabs = [_slab_spec(w, batch * HEADS, lambda b, h: b * HEADS + h) for w in cast]
    out = pl.pallas_call(
        functools.partial(_ret_kernel, n_cast=len(cast)),
        grid=(batch, HEADS),
        in_specs=[zspec(0), zspec(HEADS), zspec(2 * HEADS), zspec(3 * HEADS),
                  per_head, per_head, per_head] + slabs,
        out_specs=[pl.BlockSpec((1, seq, LANES), lambda b, h: (h, b, 0))] + slabs,
        out_shape=[jax.ShapeDtypeStruct((HEADS, t, LANES), BF16)]
        + [jax.ShapeDtypeStruct(w.shape, BF16) for w in cast],
        scratch_shapes=[
            pltpu.VMEM((seq, 2 * HEAD_DIM), BF16),
            pltpu.VMEM((n_chunks, HEAD_DIM, CHUNK), BF16),
            pltpu.VMEM((n_chunks, HEAD_DIM, HEAD_DIM), F32),
            pltpu.VMEM((n_chunks, HEAD_DIM, HEAD_DIM), BF16),
        ],
        compiler_params=pltpu.CompilerParams(
            dimension_semantics=("parallel", "parallel"),
            vmem_limit_bytes=VMEM_LIMIT),
        name="retention",
    )(z, z, z, z, lgf, lgb, gnw, *cast)
    return out[0], out[1:]


def _sgu_norm(zv_ref, nw_ref):
    v = zv_ref[...].astype(F32)
    ms = jnp.sum(jnp.sum(v * v, axis=0), axis=-1, keepdims=True) * (1.0 / SGU_WIDTH)
    inv = lax.rsqrt(ms + EPS)
    return [(v[g] * inv * nw_ref[g, 0:1, :]).astype(BF16) for g in range(GROUPS)]


def _sgu_mix(zu_ref, vn, ws_ref, b_ref):
    n_chunks = zu_ref.shape[1] // CHUNK
    groups = []
    for g in range(GROUPS):
        rhs = jnp.concatenate(
            [vn[g][c * CHUNK:(c + 1) * CHUNK] for c in range(n_chunks)], axis=1)
        sp = jnp.dot(ws_ref[g], rhs, preferred_element_type=F32)
        bias = b_ref[g]
        groups.append(jnp.concatenate(
            [zu_ref[g, c * CHUNK:(c + 1) * CHUNK, :].astype(F32)
             * (sp[:, c * CHUNK:(c + 1) * CHUNK] + bias)
             for c in range(n_chunks)], axis=0).astype(BF16))
    return jnp.concatenate(groups, axis=1)


def _outproj_kernel(*refs, n_cast):
    ret_ref, zu_ref, zv_ref, sgu_nw_ref, ws_ref, sb_ref, x_ref, w_ref, nw_ref = refs[:9]
    x2_ref, h2_ref = refs[9 + n_cast:11 + n_cast]
    _cast_slabs(refs[9:9 + n_cast], refs[11 + n_cast:])
    vn = _sgu_norm(zv_ref, sgu_nw_ref)
    ret = jnp.concatenate([ret_ref[s] for s in range(HEADS)], axis=1)
    x2_ref[...] = x_ref[...] + jnp.dot(ret, w_ref[0:RET_WIDTH, :], preferred_element_type=F32)
    sgu = _sgu_mix(zu_ref, vn, ws_ref, sb_ref)
    x2 = x2_ref[...] + jnp.dot(sgu, w_ref[RET_WIDTH:, :], preferred_element_type=F32)
    x2_ref[...] = x2
    h2_ref[...] = _rms(x2, nw_ref[...]).astype(BF16)


def _out_proj(ret, z, sgu_nw, ws_bf16, sgu_bias, x, w_bf16, norm_w, *, tm=512, cast=()):
    t = x.shape[0]
    slabs = [_slab_spec(w, t // tm, lambda i: i) for w in cast]
    u_base = 4 * HEADS // GROUPS
    const3 = lambda i: (0, 0, 0)
    out = pl.pallas_call(
        functools.partial(_outproj_kernel, n_cast=len(cast)),
        grid=(t // tm,),
        in_specs=[
            pl.BlockSpec((HEADS, tm, LANES), lambda i: (0, i, 0)),
            pl.BlockSpec((GROUPS, tm, LANES), lambda i: (u_base, i, 0)),
            pl.BlockSpec((GROUPS, tm, LANES), lambda i: (u_base + 1, i, 0)),
            pl.BlockSpec((GROUPS, SUBLANES, LANES), const3),
            pl.BlockSpec((GROUPS, CHUNK, CHUNK), const3),
            pl.BlockSpec((GROUPS, CHUNK, LANES), const3),
            pl.BlockSpec((tm, D_MODEL), lambda i: (i, 0)),
            pl.BlockSpec((D_MODEL, D_MODEL), lambda i: (0, 0)),
            pl.BlockSpec((1, D_MODEL), lambda i: (0, 0)),
        ] + slabs,
        out_specs=[pl.BlockSpec((tm, D_MODEL), lambda i: (i, 0)),
                   pl.BlockSpec((tm, D_MODEL), lambda i: (i, 0))] + slabs,
        out_shape=[jax.ShapeDtypeStruct((t, D_MODEL), F32),
                   jax.ShapeDtypeStruct((t, D_MODEL), BF16)]
        + [jax.ShapeDtypeStruct(w.shape, BF16) for w in cast],
        compiler_params=pltpu.CompilerParams(
            dimension_semantics=("parallel",),
            vmem_limit_bytes=VMEM_LIMIT),
        name="out_proj",
    )(ret, z, z, sgu_nw, ws_bf16, sgu_bias, x, w_bf16, norm_w.reshape(1, D_MODEL), *cast)
    return out[0], out[1], out[2:]


def _ffn_up_kernel(*refs, tiles_per_seq, n_cast):
    hp_ref, h_ref, hn_ref, wa_ref, wb_ref, cwa_ref, cwb_ref, cba_ref, cbb_ref = refs[:9]
    o_ref = refs[9 + n_cast]
    lhs_ref = refs[10 + 2 * n_cast]
    _cast_slabs(refs[9:9 + n_cast], refs[10 + n_cast:10 + 2 * n_cast])
    i = pl.program_id(0)
    j = pl.program_id(1)
    tm = h_ref.shape[0]
    halo = hp_ref.shape[0]

    @pl.when(j == 0)
    def _():
        pos = i % tiles_per_seq
        hp = hp_ref[...]
        hn = hn_ref[...]
        lhs_ref[0:halo, :] = jnp.where(pos == 0, jnp.zeros_like(hp), hp)
        lhs_ref[halo:halo + tm, :] = h_ref[...]
        lhs_ref[halo + tm:, :] = jnp.where(pos == tiles_per_seq - 1, jnp.zeros_like(hn), hn)

    lhs = lhs_ref[...]
    n_rows = tm + 2 * halo

    def conv_branch(w_ref, cw_ref, cb_ref):
        u = jnp.dot(lhs, w_ref[...], preferred_element_type=F32)
        prev = pltpu.roll(u, 1, 0)[halo:halo + tm]
        nxt = pltpu.roll(u, n_rows - 1, 0)[halo:halo + tm]
        cur = u[halo:halo + tm]
        return prev * cw_ref[0:1, :] + cur * cw_ref[1:2, :] + nxt * cw_ref[2:3, :] + cb_ref[...]

    a = conv_branch(wa_ref, cwa_ref, cba_ref)
    b = conv_branch(wb_ref, cwb_ref, cbb_ref)
    o_ref[...] = (_silu(a) * b).astype(o_ref.dtype)


def _ffn_up(h2, w_up_bf16, conv_w, conv_b, *, seq, tm=FFN_ROWS, tf=FFN_TILE, cast=()):
    t = h2.shape[0]
    halo = BF16_ROWS
    n_j = D_FF // tf
    halo_blocks = tm // halo
    last_halo_block = t // halo - 1
    kernel = functools.partial(_ffn_up_kernel, tiles_per_seq=seq // tm, n_cast=len(cast))
    slabs = [_slab_spec(w, (t // tm) * n_j, lambda i, j: i * n_j + j) for w in cast]
    out = pl.pallas_call(
        kernel,
        grid=(t // tm, n_j),
        in_specs=[
            pl.BlockSpec((halo, D_MODEL), lambda i, j: (jnp.maximum(i * halo_blocks - 1, 0), 0)),
            pl.BlockSpec((tm, D_MODEL), lambda i, j: (i, 0)),
            pl.BlockSpec((halo, D_MODEL),
                         lambda i, j: (jnp.minimum((i + 1) * halo_blocks, last_halo_block), 0)),
            pl.BlockSpec((D_MODEL, tf), lambda i, j: (0, j)),
            pl.BlockSpec((D_MODEL, tf), lambda i, j: (0, n_j + j)),
            pl.BlockSpec((3, tf), lambda i, j: (0, j)),
            pl.BlockSpec((3, tf), lambda i, j: (0, n_j + j)),
            pl.BlockSpec((1, tf), lambda i, j: (0, j)),
            pl.BlockSpec((1, tf), lambda i, j: (0, n_j + j)),
        ] + slabs,
        out_specs=[pl.BlockSpec((None, tm, tf), lambda i, j: (j, i, 0))] + slabs,
        out_shape=[jax.ShapeDtypeStruct((n_j, t, tf), BF16)]
        + [jax.ShapeDtypeStruct(w.shape, BF16) for w in cast],
        scratch_shapes=[pltpu.VMEM((tm + 2 * halo, D_MODEL), BF16)],
        compiler_params=pltpu.CompilerParams(
            dimension_semantics=("parallel", "arbitrary"),
            vmem_limit_bytes=VMEM_LIMIT),
        name="ffn_up",
    )(h2, h2, h2, w_up_bf16, w_up_bf16, conv_w, conv_w,
      conv_b.reshape(1, 2 * D_FF), conv_b.reshape(1, 2 * D_FF), *cast)
    return out[0], out[1:]


def _ffn_down_kernel(act_ref, w_ref, x2_ref, fw_ref, o_ref):
    act = jnp.concatenate([act_ref[j] for j in range(act_ref.shape[0])], axis=1)
    y = x2_ref[...] + jnp.dot(act, w_ref[...], preferred_element_type=F32)
    o_ref[...] = _rms(y, fw_ref[...])


def _ffn_down(act, w_down_bf16, x2, final_w, *, tm=512):
    n_j, t, tf = act.shape
    return pl.pallas_call(
        _ffn_down_kernel,
        grid=(t // tm,),
        in_specs=[
            pl.BlockSpec((n_j, tm, tf), lambda i: (0, i, 0)),
            pl.BlockSpec((D_FF, D_MODEL), lambda i: (0, 0), pipeline_mode=pl.Buffered(1)),
            pl.BlockSpec((tm, D_MODEL), lambda i: (i, 0)),
            pl.BlockSpec((1, D_MODEL), lambda i: (0, 0)),
        ],
        out_specs=pl.BlockSpec((tm, D_MODEL), lambda i: (i, 0)),
        out_shape=jax.ShapeDtypeStruct((t, D_MODEL), F32),
        compiler_params=pltpu.CompilerParams(
            dimension_semantics=("parallel",),
            vmem_limit_bytes=VMEM_LIMIT),
        name="ffn_down",
    )(act, w_down_bf16, x2, final_w.reshape(1, D_MODEL))


def _rope_tables(seq):
    inv_freq = ROPE_BASE ** (-np.arange(0, HEAD_DIM, 2, dtype=np.float64) / HEAD_DIM)
    ang = np.arange(seq, dtype=np.float64)[:, None] * inv_freq[None, :]
    cos = np.cos(ang).astype(np.float32)
    sin = np.sin(ang).astype(np.float32)
    return (jnp.asarray(np.concatenate([cos, cos], axis=-1)),
            jnp.asarray(np.concatenate([-sin, sin], axis=-1)))


def _per_slab(vec, slabs):
    return jnp.broadcast_to(vec.reshape(slabs, 1, LANES), (slabs, SUBLANES, LANES))


def _per_head_scalar(vec):
    return jnp.broadcast_to(vec.astype(F32)[:, None, None], (HEADS, SUBLANES, LANES))


def _trunk(x, p):
    batch, seq, _ = x.shape
    xt = x.reshape(batch * seq, D_MODEL)
    z = _in_proj(xt, p["norm1_w"], p["w_in"], p["cos"], p["sin"], seq=seq)
    if "w_out" in p:
        ret, _ = _retention(z, p["lgf"], p["lgb"], p["gnw"], batch=batch, seq=seq)
    else:
        ret, (w_out,) = _retention(z, p["lgf"], p["lgb"], p["gnw"], batch=batch, seq=seq,
                                   cast=(p["w_out_f32"],))
        p = dict(p, w_out=w_out)
    if "w_up" in p:
        x2, h2, _ = _out_proj(ret, z, p["sgu_nw"], p["w_s"], p["sgu_b"], xt, p["w_out"],
                              p["norm2_w"])
    else:
        x2, h2, (w_up,) = _out_proj(ret, z, p["sgu_nw"], p["w_s"], p["sgu_b"], xt, p["w_out"],
                                    p["norm2_w"], cast=(p["w_up_f32"],))
        p = dict(p, w_up=w_up)
    if "w_down" in p:
        act, _ = _ffn_up(h2, p["w_up"], p["conv_w"], p["conv_b"], seq=seq)
    else:
        act, (w_down,) = _ffn_up(h2, p["w_up"], p["conv_w"], p["conv_b"], seq=seq,
                                 cast=(p["w_down_f32"],))
        p = dict(p, w_down=w_down)
    y = _ffn_down(act, p["w_down"], x2, p["final_w"])
    return y.reshape(batch, seq, D_MODEL), p


def kernel(x_prompt, x_sample, norm1_w, w_in, ret_log_decay_fwd, ret_log_decay_bwd, ret_gn_w, sgu_norm_w, sgu_w_s, sgu_b, w_out, norm2_w, w_up, conv_w, conv_b, w_down, final_norm_w):
    max_seq = max(x_prompt.shape[1], x_sample.shape[1])
    cos, sin = _rope_tables(max_seq)
    p = {
        "norm1_w": norm1_w[0],
        "w_in": w_in[0].astype(BF16),
        "cos": cos,
        "sin": sin,
        "lgf": _per_head_scalar(ret_log_decay_fwd[0]),
        "lgb": _per_head_scalar(ret_log_decay_bwd[0]),
        "gnw": _per_slab(ret_gn_w[0], HEADS),
        "sgu_nw": _per_slab(sgu_norm_w[0], GROUPS),
        "w_s": sgu_w_s[0].astype(BF16),
        "sgu_b": jnp.broadcast_to(sgu_b[0][:, :, None], (GROUPS, CHUNK, LANES)),
        "w_out_f32": w_out[0],
        "norm2_w": norm2_w[0],
        "w_up_f32": w_up[0],
        "conv_w": conv_w[0],
        "conv_b": conv_b[0],
        "w_down_f32": w_down[0],
        "final_w": final_norm_w,
    }
    y_prompt, p = _trunk(x_prompt, p)
    y_sample, _ = _trunk(x_sample, p)
    return (y_prompt, y_sample)
```
